```python
import math
import jax
import jax.numpy as jnp
from jax import lax
import numpy as np

D_MODEL = 1024
BATCH = 2
SEQ = 8192
DEPTH = 2

HEAD_DIM = 64
Q_BLOCK = 128
EPS = 1e-6
NEG = -1e30

NUM_BUCKETS = 32
MAX_DISTANCE = 128
BIAS_HEADS = 16

NSA_HEADS = 8
NSA_KV_HEADS = 2
NSA_GROUP = NSA_HEADS // NSA_KV_HEADS
CMP_BLOCK = 32
CMP_STRIDE = 16
CMP_HIDDEN = 128
SEL_BLOCK = 64
SEL_TOPN = 16
WINDOW = 512
FORCE_BONUS = 1e3

DIFF_HEADS = 4
DIFF_VDIM = 2 * HEAD_DIM

MOBA_HEADS = 16
MOBA_BLOCK = 256
MOBA_TOPK = 3
MOBA_QCHUNK = 32

D_FF = 4 * D_MODEL

NSA_Q = NSA_HEADS * HEAD_DIM
NSA_KV = NSA_KV_HEADS * HEAD_DIM
NSA_GATES = 3 * NSA_HEADS
DIFF_QK = DIFF_HEADS * 2 * HEAD_DIM
DIFF_V = DIFF_HEADS * DIFF_VDIM
D_MIX = NSA_Q + DIFF_V
EVEN_SIZES = (NSA_Q,) + (NSA_KV,) * 6 + (NSA_GATES, DIFF_QK, DIFF_QK, DIFF_V)
EVEN_PROJ = sum(EVEN_SIZES)
ODD_PROJ = 3 * MOBA_HEADS * HEAD_DIM
ODD_MIX = MOBA_HEADS * HEAD_DIM

kernel_name = 'hybrid_nsa_diff_moba_trunk'


def rms_norm(x, g):
    xf = x.astype(jnp.float32)
    y = xf * lax.rsqrt(jnp.mean(xf * xf, axis=-1, keepdims=True) + EPS)
    return (y * g.astype(jnp.float32)).astype(x.dtype)


def t5_bucket(rel):
    n = jnp.maximum(rel, 0)
    max_exact = NUM_BUCKETS // 2
    large = max_exact + (jnp.log(jnp.maximum(n, 1).astype(jnp.float32) / max_exact)
                         / math.log(MAX_DISTANCE / max_exact)
                         * (NUM_BUCKETS - max_exact)).astype(jnp.int32)
    large = jnp.minimum(large, NUM_BUCKETS - 1)
    return jnp.where(n < max_exact, n, large)


def masked_softmax(logits, mask):
    l = jnp.where(mask, logits.astype(jnp.float32), NEG)
    m = jnp.max(l, axis=-1, keepdims=True)
    p = jnp.where(mask, jnp.exp(l - m), 0.0)
    return p / jnp.maximum(jnp.sum(p, axis=-1, keepdims=True), 1e-30)


def to_heads(t, n):
    b, s, _ = t.shape
    return t.reshape(b, s, n, -1).transpose(0, 2, 1, 3)


def compress_kv(kv, pos, w1, w2):
    b, g, s, d = kv.shape
    nc = (s - CMP_BLOCK) // CMP_STRIDE + 1
    tok = jnp.arange(nc)[:, None] * CMP_STRIDE + jnp.arange(CMP_BLOCK)[None, :]
    blk = kv[:, :, tok] + pos
    hid = jax.nn.gelu(blk.reshape(b, g, nc, CMP_BLOCK * d) @ w1)
    return hid @ w2


def nsa_attention(q, kc, vc, ks, vs, kw, vw, gates, tbl_h):
    b, g, r, s, d = q.shape
    nc = kc.shape[2]
    ns = s // SEL_BLOCK
    n_sel = min(SEL_TOPN, ns)
    scale = d ** -0.5
    tbl_gr = tbl_h.reshape(g, r, NUM_BUCKETS)
    cmp_end = jnp.arange(nc) * CMP_STRIDE + CMP_BLOCK - 1
    tok = jnp.arange(nc)[:, None] * CMP_STRIDE + jnp.arange(CMP_BLOCK)[None, :]
    overlap = jax.nn.one_hot(tok // SEL_BLOCK, ns, dtype=jnp.float32).sum(axis=1) / CMP_BLOCK
    ks_blk = ks.reshape(b, g, ns, SEL_BLOCK, d)
    vs_blk = vs.reshape(b, g, ns, SEL_BLOCK, d)
    kw_pad = jnp.pad(kw, ((0, 0), (0, 0), (WINDOW, 0), (0, 0)))
    vw_pad = jnp.pad(vw, ((0, 0), (0, 0), (WINDOW, 0), (0, 0)))
    span = WINDOW + Q_BLOCK
    bi = jnp.arange(b)[:, None, None, None]
    gi = jnp.arange(g)[None, :, None, None]
    gi6 = jnp.arange(g)[None, :, None, None, None, None]
    ri6 = jnp.arange(r)[None, None, :, None, None, None]
    blk_ids = jnp.arange(ns)

    def block(j):
        q0 = j * Q_BLOCK
        t = q0 + jnp.arange(Q_BLOCK)
        qb = lax.dynamic_slice_in_dim(q, q0, Q_BLOCK, axis=3)
        gb = lax.dynamic_slice_in_dim(gates, q0, Q_BLOCK, axis=4)
        rel_c = t[:, None] - cmp_end[None, :]
        lc = jnp.einsum('bgrqd,bgcd->bgrqc', qb, kc) * scale + tbl_gr[:, :, t5_bucket(rel_c)]
        pc = masked_softmax(lc, rel_c >= 0)
        oc = jnp.einsum('bgrqc,bgcd->bgrqd', pc.astype(vc.dtype), vc)
        imp = jnp.einsum('bgrqc,cn->bgqn', pc, overlap)
        cur = t // SEL_BLOCK
        eligible = blk_ids[None, :] <= cur[:, None]
        forced = ((blk_ids[None, :] == 0) | (blk_ids[None, :] == cur[:, None])
                  | (blk_ids[None, :] == cur[:, None] - 1))
        score = jnp.where(eligible, imp + jnp.where(forced, FORCE_BONUS, 0.0), -1.0)
        _, sel = lax.top_k(score, n_sel)
        k_sel = ks_blk[bi, gi, sel]
        v_sel = vs_blk[bi, gi, sel]
        pos_s = sel[..., None] * SEL_BLOCK + jnp.arange(SEL_BLOCK)
        rel_s = t[None, None, :, None, None] - pos_s
        bias_s = tbl_gr[gi6, ri6, t5_bucket(rel_s)[:, :, None]]
        ls = jnp.einsum('bgrqd,bgqnkd->bgrqnk', qb, k_sel) * scale + bias_s
        ms = jnp.broadcast_to((rel_s >= 0)[:, :, None], ls.shape)
        ps = masked_softmax(ls.reshape(b, g, r, Q_BLOCK, -1),
                            ms.reshape(b, g, r, Q_BLOCK, -1)).reshape(ls.shape)
        o_s = jnp.einsum('bgrqnk,bgqnkd->bgrqd', ps.astype(v_sel.dtype), v_sel)
        kwb = lax.dynamic_slice_in_dim(kw_pad, q0, span, axis=2)
        vwb = lax.dynamic_slice_in_dim(vw_pad, q0, span, axis=2)
        pos_w = q0 - WINDOW + jnp.arange(span)
        rel_w = t[:, None] - pos_w[None, :]
        mask_w = (rel_w >= 0) & (rel_w < WINDOW) & (pos_w[None, :] >= 0)
        lw = jnp.einsum('bgrqd,bgkd->bgrqk', qb, kwb) * scale + tbl_gr[:, :, t5_bucket(rel_w)]
        pw = masked_softmax(lw, mask_w)
        ow = jnp.einsum('bgrqk,bgkd->bgrqd', pw.astype(vwb.dtype), vwb)
        return (gb[:, 0][..., None] * oc + gb[:, 1][..., None] * o_s
                + gb[:, 2][..., None] * ow)

    out = lax.map(block, jnp.arange(s // Q_BLOCK))
    return out.transpose(1, 0, 4, 2, 3, 5).reshape(b, s, g * r * d)


def diff_attention(q, k, v, lam, lam_init, subln_g, tbl_h):
    b, s = q.shape[:2]
    q = q.transpose(0, 3, 2, 1, 4)
    k = k.transpose(0, 3, 2, 1, 4)
    v = v.transpose(0, 2, 1, 3)
    kpos = jnp.arange(s)
    scale = HEAD_DIM ** -0.5

    def block(j):
        q0 = j * Q_BLOCK
        t = q0 + jnp.arange(Q_BLOCK)
        qb = lax.dynamic_slice_in_dim(q, q0, Q_BLOCK, axis=3)
        rel = t[:, None] - kpos[None, :]
        logits = jnp.einsum('bmhqd,bmhkd->bmhqk', qb, k) * scale + tbl_h[:, t5_bucket(rel)]
        p = masked_softmax(logits, rel >= 0)
        a = p[:, 0] - lam * p[:, 1]
        return jnp.einsum('bhqk,bhkv->bhqv', a.astype(v.dtype), v)

    out = lax.map(block, jnp.arange(s // Q_BLOCK))
    out = out.transpose(1, 0, 3, 2, 4).reshape(b, s, DIFF_HEADS, DIFF_VDIM)
    out = rms_norm(out, subln_g) * (1.0 - lam_init)
    return out.reshape(b, s, DIFF_HEADS * DIFF_VDIM)


def moba_attention(q, k, v, tbl_h):
    b, h, s, d = q.shape
    nb = -(-s // MOBA_BLOCK)
    pad = nb * MOBA_BLOCK - s
    k_pad = jnp.pad(k, ((0, 0), (0, 0), (0, pad), (0, 0)))
    v_pad = jnp.pad(v, ((0, 0), (0, 0), (0, pad), (0, 0)))
    kb = k_pad.reshape(b, h, nb, MOBA_BLOCK, d)
    vb = v_pad.reshape(b, h, nb, MOBA_BLOCK, d)
    k_mean = jnp.mean(kb.astype(jnp.float32), axis=3).astype(k.dtype)
    n_sel = min(MOBA_TOPK, nb)
    scale = d ** -0.5
    bi = jnp.arange(b)[:, None, None, None]
    hi = jnp.arange(h)[None, :, None, None]
    hi5 = jnp.arange(h)[None, :, None, None, None]
    blk_ids = jnp.arange(nb)

    def block(j):
        q0 = j * MOBA_QCHUNK
        t = q0 + jnp.arange(MOBA_QCHUNK)
        c = q0 // MOBA_BLOCK
        qb = lax.dynamic_slice_in_dim(q, q0, MOBA_QCHUNK, axis=2)
        gate = jnp.einsum('bhqd,bhnd->bhqn', qb, k_mean).astype(jnp.float32)
        past = blk_ids < c
        _, sel = lax.top_k(jnp.where(past, gate, NEG), n_sel)
        valid = sel < c
        k_sel = kb[bi, hi, sel]
        v_sel = vb[bi, hi, sel]
        pos_s = sel[..., None] * MOBA_BLOCK + jnp.arange(MOBA_BLOCK)
        bias_s = tbl_h[hi5, t5_bucket(t[None, None, :, None, None] - pos_s)]
        ls = jnp.einsum('bhqd,bhqnkd->bhqnk', qb, k_sel) * scale + bias_s
        ms = jnp.broadcast_to(valid[..., None], ls.shape)
        k_own = lax.dynamic_slice_in_dim(k_pad, c * MOBA_BLOCK, MOBA_BLOCK, axis=2)
        v_own = lax.dynamic_slice_in_dim(v_pad, c * MOBA_BLOCK, MOBA_BLOCK, axis=2)
        rel_o = t[:, None] - (c * MOBA_BLOCK + jnp.arange(MOBA_BLOCK))[None, :]
        lo = jnp.einsum('bhqd,bhkd->bhqk', qb, k_own) * scale + tbl_h[:, t5_bucket(rel_o)]
        mo = jnp.broadcast_to(rel_o >= 0, lo.shape)
        n_g = n_sel * MOBA_BLOCK
        logits = jnp.concatenate([ls.reshape(b, h, MOBA_QCHUNK, n_g), lo], axis=-1)
        mask = jnp.concatenate([ms.reshape(b, h, MOBA_QCHUNK, n_g), mo], axis=-1)
        p = masked_softmax(logits, mask)
        ps = p[..., :n_g].reshape(ls.shape).astype(v.dtype)
        po = p[..., n_g:].astype(v.dtype)
        return (jnp.einsum('bhqnk,bhqnkd->bhqd', ps, v_sel)
                + jnp.einsum('bhqk,bhkd->bhqd', po, v_own))

    out = lax.map(block, jnp.arange(s // MOBA_QCHUNK))
    return out.transpose(1, 0, 3, 2, 4).reshape(b, s, h * d)


def even_mixer(hn, w_in, w_out, pos_k, w1_k, w2_k, pos_v, w1_v, w2_v,
               lam_q1, lam_k1, lam_q2, lam_k2, subln_g, bias_table, lam_init):
    b, s, _ = hn.shape
    parts = jnp.split(hn @ w_in, np.cumsum(EVEN_SIZES)[:-1].tolist(), axis=-1)
    q, kc, vc, ks, vs, kw, vw, g, dq, dk, dv = parts
    qn = to_heads(q, NSA_HEADS).reshape(b, NSA_KV_HEADS, NSA_GROUP, s, HEAD_DIM)
    kc = compress_kv(to_heads(kc, NSA_KV_HEADS), pos_k, w1_k, w2_k)
    vc = compress_kv(to_heads(vc, NSA_KV_HEADS), pos_v, w1_v, w2_v)
    gates = jax.nn.sigmoid(g.astype(jnp.float32)).astype(hn.dtype)
    gates = gates.reshape(b, s, 3, NSA_KV_HEADS, NSA_GROUP).transpose(0, 2, 3, 4, 1)
    o_a = nsa_attention(qn, kc, vc, to_heads(ks, NSA_KV_HEADS), to_heads(vs, NSA_KV_HEADS),
                        to_heads(kw, NSA_KV_HEADS), to_heads(vw, NSA_KV_HEADS), gates,
                        bias_table[:, :NSA_HEADS].T)
    lam = (jnp.exp(jnp.sum((lam_q1 * lam_k1).astype(jnp.float32)))
           - jnp.exp(jnp.sum((lam_q2 * lam_k2).astype(jnp.float32))) + lam_init)
    o_b = diff_attention(dq.reshape(b, s, DIFF_HEADS, 2, HEAD_DIM),
                         dk.reshape(b, s, DIFF_HEADS, 2, HEAD_DIM),
                         dv.reshape(b, s, DIFF_HEADS, DIFF_VDIM), lam, lam_init, subln_g,
                         bias_table[:, NSA_HEADS:NSA_HEADS + DIFF_HEADS].T)
    return jnp.concatenate([o_a, o_b], axis=-1) @ w_out


def odd_mixer(hn, w_in, w_out, bias_table):
    q, k, v = jnp.split(hn @ w_in, 3, axis=-1)
    o = moba_attention(to_heads(q, MOBA_HEADS), to_heads(k, MOBA_HEADS),
                       to_heads(v, MOBA_HEADS), bias_table[:, :MOBA_HEADS].T)
    return o @ w_out


def sqrelu_mlp(hn, w1, w2):
    return jnp.square(jax.nn.relu(hn @ w1)) @ w2


def setup_inputs(seed: int = 0) -> dict:
    key = jax.random.key(seed)
    k = jax.random.split(key, 22)
    ne = (DEPTH + 1) // 2
    no = DEPTH // 2
    f32 = jnp.float32

    def nrm(i, shape, scale):
        return jax.random.normal(k[i], shape, f32) * scale

    return {
        'x': nrm(0, (BATCH, SEQ, D_MODEL), 1.0),
        'bias_table': nrm(1, (NUM_BUCKETS, BIAS_HEADS), 0.5),
        'norm_mix': 1.0 + nrm(2, (DEPTH, D_MODEL), 0.1),
        'norm_mlp': 1.0 + nrm(3, (DEPTH, D_MODEL), 0.1),
        'norm_final': 1.0 + nrm(4, (D_MODEL,), 0.1),
        'mlp_w1': nrm(5, (DEPTH, D_MODEL, D_FF), D_MODEL ** -0.5),
        'mlp_w2': nrm(6, (DEPTH, D_FF, D_MODEL), D_FF ** -0.5),
        'ev_w_in': nrm(7, (ne, D_MODEL, EVEN_PROJ), D_MODEL ** -0.5),
        'ev_w_out': nrm(8, (ne, D_MIX, D_MODEL), D_MIX ** -0.5),
        'ev_cmp_pos_k': nrm(9, (ne, CMP_BLOCK, HEAD_DIM), 0.1),
        'ev_cmp_w1_k': nrm(10, (ne, CMP_BLOCK * HEAD_DIM, CMP_HIDDEN), (CMP_BLOCK * HEAD_DIM) ** -0.5),
        'ev_cmp_w2_k': nrm(11, (ne, CMP_HIDDEN, HEAD_DIM), CMP_HIDDEN ** -0.5),
        'ev_cmp_pos_v': nrm(12, (ne, CMP_BLOCK, HEAD_DIM), 0.1),
        'ev_cmp_w1_v': nrm(13, (ne, CMP_BLOCK * HEAD_DIM, CMP_HIDDEN), (CMP_BLOCK * HEAD_DIM) ** -0.5),
        'ev_cmp_w2_v': nrm(14, (ne, CMP_HIDDEN, HEAD_DIM), CMP_HIDDEN ** -0.5),
        'ev_lam_q1': nrm(15, (ne, HEAD_DIM), 0.1),
        'ev_lam_k1': nrm(16, (ne, HEAD_DIM), 0.1),
        'ev_lam_q2': nrm(17, (ne, HEAD_DIM), 0.1),
        'ev_lam_k2': nrm(18, (ne, HEAD_DIM), 0.1),
        'ev_subln': 1.0 + nrm(19, (ne, DIFF_VDIM), 0.1),
        'od_w_in': nrm(20, (no, D_MODEL, ODD_PROJ), D_MODEL ** -0.5),
        'od_w_out': nrm(21, (no, ODD_MIX, D_MODEL), ODD_MIX ** -0.5),
    }


def reference(x, bias_table, norm_mix, norm_mlp, norm_final, mlp_w1, mlp_w2,
              ev_w_in, ev_w_out, ev_cmp_pos_k, ev_cmp_w1_k, ev_cmp_w2_k,
              ev_cmp_pos_v, ev_cmp_w1_v, ev_cmp_w2_v, ev_lam_q1, ev_lam_k1,
              ev_lam_q2, ev_lam_k2, ev_subln, od_w_in, od_w_out):
    for i in range(DEPTH):
        hn = rms_norm(x, norm_mix[i])
        if i % 2 == 0:
            e = i // 2
            lam_init = 0.8 - 0.6 * math.exp(-0.3 * i)
            mix = even_mixer(hn, ev_w_in[e], ev_w_out[e], ev_cmp_pos_k[e], ev_cmp_w1_k[e],
                             ev_cmp_w2_k[e], ev_cmp_pos_v[e], ev_cmp_w1_v[e], ev_cmp_w2_v[e],
                             ev_lam_q1[e], ev_lam_k1[e], ev_lam_q2[e], ev_lam_k2[e],
                             ev_subln[e], bias_table, lam_init)
        else:
            o = i // 2
            mix = odd_mixer(hn, od_w_in[o], od_w_out[o], bias_table)
        x = x + mix
        x = x + sqrelu_mlp(rms_norm(x, norm_mlp[i]), mlp_w1[i], mlp_w2[i])
    return rms_norm(x, norm_final)
```

```python
import functools
import math

import numpy as np
import jax
import jax.numpy as jnp
from jax import lax
from jax.experimental import pallas as pl
from jax.experimental.pallas import tpu as pltpu

F32 = jnp.float32
BF16 = jnp.bfloat16

D_MODEL = 1024
HEAD_DIM = 64
EPS = 1e-6
NEG = -1e30
NUM_BUCKETS = 32
MAX_DISTANCE = 128
NSA_HEADS = 8
NSA_KV_HEADS = 2
NSA_GROUP = 4
CMP_BLOCK = 32
CMP_STRIDE = 16
SEL_BLOCK = 64
SEL_TOPN = 16
WINDOW = 512
FORCE_BONUS = 1e3
DIFF_HEADS = 4
MOBA_HEADS = 16
MOBA_BLOCK = 256
MOBA_TOPK = 3
D_FF = 4 * D_MODEL
SCALE = HEAD_DIM ** -0.5

LANES = 128
TQ = 256
TK = 256
FAR_REL = 113
CMP_PER_TQ = TQ // CMP_STRIDE
CMP_NEAR = 128
CMP_PAD = CMP_NEAR - 2 * CMP_PER_TQ
BAND_PAD = 2 * TK
VMEM_LIMIT = 56 * 1024 * 1024


def _dot_nt(a, b):
    return lax.dot_general(a, b, (((1,), (1,)), ((), ())), preferred_element_type=F32)


def _dot(a, b):
    return jnp.dot(a, b, preferred_element_type=F32)


def _rms(x, g):
    return x * lax.rsqrt(jnp.mean(x * x, axis=-1, keepdims=True) + EPS) * g


def _t5_bucket_np(rel):
    n = np.maximum(rel, 0)
    max_exact = NUM_BUCKETS // 2
    large = max_exact + (np.log(np.maximum(n, 1).astype(np.float32) / np.float32(max_exact))
                         / np.float32(math.log(MAX_DISTANCE / max_exact))
                         * np.float32(NUM_BUCKETS - max_exact)).astype(np.int32)
    large = np.minimum(large, NUM_BUCKETS - 1)
    return np.where(n < max_exact, n, large).astype(np.int32)


def _strip_kernel(tbl_ref, bucket_ref, valid_ref, o_ref, *, head0):
    h = pl.program_id(0) + head0
    bucket = bucket_ref[...]
    last = tbl_ref[NUM_BUCKETS - 1, h]
    acc = jnp.zeros(bucket.shape, F32)
    for b in range(NUM_BUCKETS - 1):
        acc = jnp.where(bucket == b, tbl_ref[b, h] - last, acc)
    o_ref[0] = jnp.where(valid_ref[...] != 0, acc, NEG)


def _bias_strips(table, rel, valid, head0, nheads, name):
    rows, cols = rel.shape
    bucket = jnp.asarray(_t5_bucket_np(rel))
    validi = jnp.asarray(valid.astype(np.int32))
    return pl.pallas_call(
        functools.partial(_strip_kernel, head0=head0),
        out_shape=jax.ShapeDtypeStruct((nheads, rows, cols), F32),
        grid=(nheads,),
        in_specs=[pl.BlockSpec(memory_space=pltpu.SMEM),
                  pl.BlockSpec((rows, cols), lambda h: (0, 0)),
                  pl.BlockSpec((rows, cols), lambda h: (0, 0))],
        out_specs=pl.BlockSpec((1, rows, cols), lambda h: (h, 0, 0)),
        name=name,
    )(table, bucket, validi)


def _norm_matmul_kernel(x_ref, g_ref, w_ref, o_ref, xn_ref):
    @pl.when(pl.program_id(1) == 0)
    def _():
        xn_ref[...] = _rms(x_ref[...], g_ref[...]).astype(BF16)

    o_ref[...] = _dot(xn_ref[...], w_ref[...]).astype(o_ref.dtype)


def _norm_matmul(x, g, w, out_dtype, tm, tn, name):
    m, d = x.shape
    n = w.shape[1]
    return pl.pallas_call(
        _norm_matmul_kernel,
        out_shape=jax.ShapeDtypeStruct((m, n), out_dtype),
        grid=(m // tm, n // tn),
        in_specs=[pl.BlockSpec((tm, d), lambda i, j: (i, 0)),
                  pl.BlockSpec((1, d), lambda i, j: (0, 0)),
                  pl.BlockSpec((d, tn), lambda i, j: (0, j))],
        out_specs=pl.BlockSpec((tm, tn), lambda i, j: (i, j)),
        scratch_shapes=[pltpu.VMEM((tm, d), BF16)],
        compiler_params=pltpu.CompilerParams(
            dimension_semantics=("arbitrary", "arbitrary"), vmem_limit_bytes=VMEM_LIMIT),
        name=name,
    )(x, g.reshape(1, d), w)


def _post_kernel(x_ref, a0_ref, a1_ref, wo0_ref, wo1_ref, g_ref, w1_ref, w2_ref, gf_ref,
                 o_ref, acc_ref, xn_ref, *, final_norm):
    f = pl.program_id(1)

    @pl.when(f == 0)
    def _():
        x1 = x_ref[...] + _dot(a0_ref[...], wo0_ref[...]) + _dot(a1_ref[...], wo1_ref[...])
        acc_ref[...] = x1
        xn_ref[...] = _rms(x1, g_ref[...]).astype(BF16)

    h = jnp.square(jnp.maximum(_dot(xn_ref[...], w1_ref[...]), 0.0))
    acc_ref[...] += _dot(h.astype(BF16), w2_ref[...])

    @pl.when(f == pl.num_programs(1) - 1)
    def _():
        y = acc_ref[...]
        if final_norm:
            y = _rms(y, gf_ref[...])
        o_ref[...] = y


def _post(x, a0, a1, a1_colblk, wo0, wo1, g, w1, w2, gf, final_norm, tm, tf, name):
    m, d = x.shape
    k0 = wo0.shape[0]
    k1 = wo1.shape[0]
    ff = w1.shape[1]
    return pl.pallas_call(
        functools.partial(_post_kernel, final_norm=final_norm),
        out_shape=jax.ShapeDtypeStruct((m, d), F32),
        grid=(m // tm, ff // tf),
        in_specs=[pl.BlockSpec((tm, d), lambda i, f: (i, 0)),
                  pl.BlockSpec((tm, k0), lambda i, f: (i, 0)),
                  pl.BlockSpec((tm, k1), lambda i, f: (i, a1_colblk)),
                  pl.BlockSpec((k0, d), lambda i, f: (0, 0)),
                  pl.BlockSpec((k1, d), lambda i, f: (0, 0)),
                  pl.BlockSpec((1, d), lambda i, f: (0, 0)),
                  pl.BlockSpec((d, tf), lambda i, f: (0, f)),
                  pl.BlockSpec((tf, d), lambda i, f: (f, 0)),
                  pl.BlockSpec((1, d), lambda i, f: (0, 0))],
        out_specs=pl.BlockSpec((tm, d), lambda i, f: (i, 0)),
        scratch_shapes=[pltpu.VMEM((tm, d), F32), pltpu.VMEM((tm, d), BF16)],
        compiler_params=pltpu.CompilerParams(
            dimension_semantics=("arbitrary", "arbitrary"), vmem_limit_bytes=VMEM_LIMIT),
        name=name,
    )(x, a0, a1, wo0, wo1, g.reshape(1, d), w1, w2, gf.reshape(1, d))


def _compress_kernel(x_ref, pos_ref, w1_ref, w2_ref, o_ref):
    half = CMP_STRIDE * HEAD_DIM
    x = x_ref[0, 0].astype(BF16)
    w1a = w1_ref[0, :half, :]
    w1b = w1_ref[0, half:, :]
    first = _dot(x, w1a)
    second = _dot(x, w1b)
    nrow = second.shape[0]
    second = pltpu.roll(second, nrow - 1, 0)
    pos = pos_ref[0].astype(BF16)
    pa = jnp.broadcast_to(pos[0:1], (8, half))
    pb = jnp.broadcast_to(pos[1:2], (8, half))
    posterm = (_dot(pa, w1a) + _dot(pb, w1b))[0:1]
    hid = jax.nn.gelu(first + second + posterm)
    o_ref[0, 0] = _dot(hid.astype(BF16), w2_ref[0])


def _compress(xkv, pos, w1, w2):
    _, bg, nch, wide = xkv.shape
    hid = w1.shape[2]
    return pl.pallas_call(
        _compress_kernel,
        out_shape=jax.ShapeDtypeStruct((2, bg, nch, HEAD_DIM), F32),
        grid=(2, bg),
        in_specs=[pl.BlockSpec((1, 1, nch, wide), lambda k, n: (k, n, 0, 0)),
                  pl.BlockSpec((1, 2, wide), lambda k, n: (k, 0, 0)),
                  pl.BlockSpec((1, 2 * wide, hid), lambda k, n: (k, 0, 0)),
                  pl.BlockSpec((1, hid, HEAD_DIM), lambda k, n: (k, 0, 0))],
        out_specs=pl.BlockSpec((1, 1, nch, HEAD_DIM), lambda k, n: (k, n, 0, 0)),
        compiler_params=pltpu.CompilerParams(
            dimension_semantics=("arbitrary", "arbitrary"), vmem_limit_bytes=VMEM_LIMIT),
        name="nsa_compress",
    )(xkv, pos, w1, w2)


def _online_step(s, v_tile, m, l, acc):
    m_new = jnp.maximum(m, jnp.max(s, axis=1, keepdims=True))
    alpha = jnp.exp(m - m_new)
    p = jnp.exp(s - m_new)
    l = alpha * l + jnp.sum(p, axis=1, keepdims=True)
    acc = alpha * acc + _dot(p.astype(BF16), v_tile)
    return m_new, l, acc


def _topk_rows(score, idx, k, floor):
    sel = jnp.zeros(score.shape, F32)
    nrow = score.shape[0]
    for _ in range(k):
        mx = jnp.max(score, axis=0, keepdims=True)
        first = jnp.min(jnp.where(score == mx, idx, nrow), axis=0, keepdims=True)
        pick = idx == first
        sel = jnp.where(pick, 1.0, sel)
        score = jnp.where(pick, floor, score)
    return sel


def _nsa_kernel(q_ref, ks_ref, vs_ref, kw_ref, vw_ref, kc_ref, vc_ref, gate_ref,
                ssel_ref, swin_ref, scmp_ref, ov_ref, o_ref,
                kaug, vaug, kwp, vwp, kcp, vcp, *, seq):
    g = pl.program_id(1)
    i = pl.program_id(2)
    ncp = seq // CMP_STRIDE
    hd = HEAD_DIM
    rows = NSA_GROUP * TQ

    @pl.when(i == 0)
    def _():
        lane = lax.broadcasted_iota(jnp.int32, (BAND_PAD, LANES), 1)
        padflag = jnp.where(lane == hd, 1.0, 0.0).astype(BF16)
        kaug[0:BAND_PAD, 0:LANES] = jnp.zeros((BAND_PAD, LANES), BF16)
        kaug[0:BAND_PAD, LANES:2 * LANES] = padflag
        kwp[0:BAND_PAD, :] = padflag
        vaug[0:BAND_PAD, :] = jnp.zeros((BAND_PAD, LANES), BF16)
        vwp[0:BAND_PAD, :] = jnp.zeros((BAND_PAD, LANES), BF16)
        onescol = jnp.where(lax.broadcasted_iota(jnp.int32, (TK, LANES), 1) == hd, 1.0, 0.0)
        zero_hd = jnp.zeros((TK, hd), F32)
        col = lax.broadcasted_iota(jnp.int32, (TK, LANES), 1)
        row = lax.broadcasted_iota(jnp.int32, (TK, LANES), 0)

        def fill(c, _):
            src = pl.multiple_of(c * TK, TK)
            dst = pl.multiple_of(BAND_PAD + c * TK, TK)

            def pick(ref):
                both = ref[pl.ds(src, TK), :].astype(F32)
                return jnp.concatenate([jnp.where(g == 0, both[:, :hd], both[:, hd:]), zero_hd], axis=1)

            kaug[pl.ds(dst, TK), 0:LANES] = jnp.where((src + row) // SEL_BLOCK == col, 1.0, 0.0).astype(BF16)
            kaug[pl.ds(dst, TK), LANES:2 * LANES] = pick(ks_ref).astype(BF16)
            kwp[pl.ds(dst, TK), :] = pick(kw_ref).astype(BF16)
            vaug[pl.ds(dst, TK), :] = (pick(vs_ref) + onescol).astype(BF16)
            vwp[pl.ds(dst, TK), :] = (pick(vw_ref) + onescol).astype(BF16)
            return 0

        lax.fori_loop(0, seq // TK, fill, 0)
        lane_c = lax.broadcasted_iota(jnp.int32, (CMP_PAD, LANES), 1)
        kcp[0:CMP_PAD, :] = jnp.where(lane_c == hd, 1.0, 0.0).astype(BF16)
        vcp[0:CMP_PAD, :] = jnp.zeros((CMP_PAD, LANES), BF16)
        zc = jnp.zeros((ncp, hd), F32)
        onesc = jnp.where(lax.broadcasted_iota(jnp.int32, (ncp, LANES), 1) == hd, 1.0, 0.0)
        kcp[CMP_PAD:CMP_PAD + ncp, :] = jnp.concatenate([kc_ref[0, 0], zc], axis=1).astype(BF16)
        vcp[CMP_PAD:CMP_PAD + ncp, :] = (jnp.concatenate([vc_ref[0, 0], zc], axis=1) + onesc).astype(BF16)
        tail = kcp.shape[0] - CMP_PAD - ncp
        kcp[CMP_PAD + ncp:, :] = jnp.zeros((tail, LANES), BF16)
        vcp[CMP_PAD + ncp:, :] = jnp.zeros((tail, LANES), BF16)

    qblk = q_ref[...].astype(F32) * SCALE
    lane_t = lax.broadcasted_iota(jnp.int32, (TQ, hd), 1)
    qtail = jnp.where(lane_t == 0, NEG, 0.0)
    qs = [qblk[:, r * hd:(r + 1) * hd] for r in range(NSA_GROUP)]
    q128 = jnp.concatenate([jnp.concatenate([qs[r], qtail], axis=1) for r in range(NSA_GROUP)],
                           axis=0).astype(BF16)

    def stack(ref, lo, hi):
        return jnp.concatenate([ref[r, :, lo:hi] for r in range(NSA_GROUP)], axis=0)

    s_far = _dot_nt(q128, kcp[CMP_PAD:CMP_PAD + ncp, :])
    c_idx = lax.broadcasted_iota(jnp.int32, (1, ncp), 1)
    far_ok = c_idx < CMP_PER_TQ * i - CMP_PAD
    near0 = pl.multiple_of(CMP_PER_TQ * i, CMP_PER_TQ)
    s_near = _dot_nt(q128, kcp[pl.ds(near0, CMP_NEAR), :]) + stack(scmp_ref, 0, CMP_NEAR)
    near_ok = s_near > 0.5 * NEG
    s_far = jnp.where(far_ok, s_far, NEG)
    m_c = jnp.maximum(jnp.max(s_far, axis=1, keepdims=True), jnp.max(s_near, axis=1, keepdims=True))
    p_far = jnp.where(far_ok, jnp.exp(s_far - m_c), 0.0)
    p_near = jnp.where(near_ok, jnp.exp(s_near - m_c), 0.0)
    den = jnp.maximum(jnp.sum(p_far, axis=1, keepdims=True) + jnp.sum(p_near, axis=1, keepdims=True), 1e-30)
    pc_far = p_far / den
    pc_near = p_near / den
    o_cmp = (_dot(pc_far.astype(BF16), vcp[CMP_PAD:CMP_PAD + ncp, :])
             + _dot(pc_near.astype(BF16), vcp[pl.ds(near0, CMP_NEAR), :]))[:, :hd]

    pcs_far = sum(pc_far[r * TQ:(r + 1) * TQ] for r in range(NSA_GROUP))
    pcs_near = sum(pc_near[r * TQ:(r + 1) * TQ] for r in range(NSA_GROUP))

    def hi_lo_dot(p, w):
        hi = p.astype(BF16)
        lo = (p - hi.astype(F32)).astype(BF16)
        return _dot(hi, w) + _dot(lo, w)

    imp = (hi_lo_dot(pcs_far, ov_ref[CMP_PAD:CMP_PAD + ncp, :])
           + hi_lo_dot(pcs_near, ov_ref[pl.ds(near0, CMP_NEAR), :]))
    imp_t = imp.T
    n_idx = lax.broadcasted_iota(jnp.int32, (LANES, TQ), 0)
    t_idx = TQ * i + lax.broadcasted_iota(jnp.int32, (LANES, TQ), 1)
    cur = t_idx // SEL_BLOCK
    forced = (n_idx == 0) | (n_idx == cur) | (n_idx == cur - 1)
    score = jnp.where(n_idx <= cur, imp_t + jnp.where(forced, FORCE_BONUS, 0.0), -1.0)
    sel_t = _topk_rows(score, n_idx, SEL_TOPN, -3.0e38)
    maskneg = jnp.where(sel_t > 0.5, 0.0, NEG).T
    qaug = jnp.concatenate(
        [jnp.concatenate([maskneg, qs[r], qtail], axis=1) for r in range(NSA_GROUP)],
        axis=0).astype(BF16)

    init = (jnp.full((rows, 1), NEG, F32), jnp.zeros((rows, LANES), F32))

    def flash(carry, s, v_tile):
        m, acc = carry
        m_new = jnp.maximum(m, jnp.max(s, axis=1, keepdims=True))
        alpha = jnp.exp(m - m_new)
        p = jnp.exp(s - m_new)
        return m_new, alpha * acc + _dot(p.astype(BF16), v_tile)

    def far_body(j, carry):
        r0 = pl.multiple_of(BAND_PAD + j * TK, TK)
        return flash(carry, _dot_nt(qaug, kaug[pl.ds(r0, TK), :]), vaug[pl.ds(r0, TK), :])

    carry = lax.fori_loop(0, jnp.maximum(i - 2, 0), far_body, init)
    band0 = pl.multiple_of(TQ * i, TK)
    for ch in range(3):
        r0 = band0 + ch * TK
        s = _dot_nt(qaug, kaug[pl.ds(r0, TK), :]) + stack(ssel_ref, ch * TK, (ch + 1) * TK)
        carry = flash(carry, s, vaug[pl.ds(r0, TK), :])
    o_sel = carry[1][:, :hd] / carry[1][:, hd:hd + 1]

    carry = init
    for ch in range(3):
        r0 = band0 + ch * TK
        s = _dot_nt(q128, kwp[pl.ds(r0, TK), :]) + stack(swin_ref, ch * TK, (ch + 1) * TK)
        carry = flash(carry, s, vwp[pl.ds(r0, TK), :])
    o_win = carry[1][:, :hd] / carry[1][:, hd:hd + 1]

    sig = jax.nn.sigmoid(gate_ref[...])
    outs = []
    for r in range(NSA_GROUP):
        def gcol(branch):
            c0 = branch * NSA_HEADS + r
            c1 = c0 + NSA_GROUP
            return jnp.where(g == 0, sig[:, c0:c0 + 1], sig[:, c1:c1 + 1])
        sl = slice(r * TQ, (r + 1) * TQ)
        outs.append(gcol(0) * o_cmp[sl] + gcol(1) * o_sel[sl] + gcol(2) * o_win[sl])
    o_ref[...] = jnp.concatenate(outs, axis=1).astype(o_ref.dtype)


def _nsa(main, aux, ckv, ssel, swin, scmp, ov, batch, seq):
    nq = seq // TQ
    ncp = seq // CMP_STRIDE
    gw = NSA_GROUP * HEAD_DIM
    kv_spec = lambda colblk: pl.BlockSpec((seq, LANES), lambda b, g, i: (b, colblk))
    strip_spec = lambda cols: pl.BlockSpec((NSA_GROUP, TQ, cols), lambda b, g, i: (g, 0, 0))
    return pl.pallas_call(
        functools.partial(_nsa_kernel, seq=seq),
        out_shape=jax.ShapeDtypeStruct((batch * seq, NSA_HEADS * HEAD_DIM), BF16),
        grid=(batch, NSA_KV_HEADS, nq),
        in_specs=[pl.BlockSpec((TQ, gw), lambda b, g, i: (b * nq + i, g)),
                  kv_spec(4), kv_spec(5), kv_spec(6), kv_spec(7),
                  pl.BlockSpec((1, 1, ncp, HEAD_DIM), lambda b, g, i: (0, b * NSA_KV_HEADS + g, 0, 0)),
                  pl.BlockSpec((1, 1, ncp, HEAD_DIM), lambda b, g, i: (1, b * NSA_KV_HEADS + g, 0, 0)),
                  pl.BlockSpec((TQ, LANES), lambda b, g, i: (b * nq + i, 2)),
                  strip_spec(3 * TK), strip_spec(3 * TK), strip_spec(CMP_NEAR),
                  pl.BlockSpec(ov.shape, lambda b, g, i: (0, 0))],
        out_specs=pl.BlockSpec((TQ, gw), lambda b, g, i: (b * nq + i, g)),
        scratch_shapes=[pltpu.VMEM((BAND_PAD + seq, 2 * LANES), BF16),
                        pltpu.VMEM((BAND_PAD + seq, LANES), BF16),
                        pltpu.VMEM((BAND_PAD + seq, LANES), BF16),
                        pltpu.VMEM((BAND_PAD + seq, LANES), BF16),
                        pltpu.VMEM((ov.shape[0], LANES), BF16),
                        pltpu.VMEM((ov.shape[0], LANES), BF16)],
        compiler_params=pltpu.CompilerParams(
            dimension_semantics=("arbitrary", "arbitrary", "arbitrary"), vmem_limit_bytes=VMEM_LIMIT),
        name="nsa_attention",
    )(main, main, main, main, main, ckv, ckv, aux, ssel, swin, scmp, ov)


def _diff_kernel(q_ref, k_ref, v_ref, lam_ref, sub_ref, strip_ref, o_ref, *, lam_init):
    i = pl.program_id(2)
    hd = HEAD_DIM
    q = q_ref[...].astype(F32) * SCALE
    lane = lax.broadcasted_iota(jnp.int32, (TQ, 2 * hd), 1)
    qst = jnp.concatenate([jnp.where(lane < hd, q, 0.0), jnp.where(lane < hd, 0.0, q)],
                          axis=0).astype(BF16)
    init = (jnp.full((2 * TQ, 1), NEG, F32), jnp.zeros((2 * TQ, 1), F32), jnp.zeros((2 * TQ, 2 * hd), F32))

    def far_body(j, carry):
        r0 = pl.multiple_of(j * TK, TK)
        return _online_step(_dot_nt(qst, k_ref[pl.ds(r0, TK), :]), v_ref[pl.ds(r0, TK), :], *carry)

    carry = lax.fori_loop(0, jnp.maximum(i - 1, 0), far_body, init)
    strip = strip_ref[0]
    prev0 = pl.multiple_of(jnp.maximum(i - 1, 0) * TK, TK)
    nopad = jnp.where(i == 0, NEG, 0.0)
    s = _dot_nt(qst, k_ref[pl.ds(prev0, TK), :]) + jnp.concatenate([strip[:, :TK]] * 2, axis=0) + nopad
    carry = _online_step(s, v_ref[pl.ds(prev0, TK), :], *carry)
    diag0 = pl.multiple_of(i * TK, TK)
    s = _dot_nt(qst, k_ref[pl.ds(diag0, TK), :]) + jnp.concatenate([strip[:, TK:]] * 2, axis=0)
    _, l, acc = _online_step(s, v_ref[pl.ds(diag0, TK), :], *carry)
    o = acc / l
    lam_p = lam_ref[...]
    lam = (jnp.exp(jnp.sum(lam_p[0:1] * lam_p[1:2], axis=1, keepdims=True))
           - jnp.exp(jnp.sum(lam_p[2:3] * lam_p[3:4], axis=1, keepdims=True)) + lam_init)
    d = o[:TQ] - lam * o[TQ:]
    o_ref[...] = (_rms(d, sub_ref[...]) * (1.0 - lam_init)).astype(o_ref.dtype)


def _diff(main, lam_p, subln, strips, batch, seq, lam_init):
    nq = seq // TQ
    blk = 2 * HEAD_DIM
    q0, k0, v0 = 1024 // blk, 1536 // blk, 2048 // blk
    return pl.pallas_call(
        functools.partial(_diff_kernel, lam_init=lam_init),
        out_shape=jax.ShapeDtypeStruct((batch * seq, DIFF_HEADS * blk), BF16),
        grid=(batch, DIFF_HEADS, nq),
        in_specs=[pl.BlockSpec((TQ, blk), lambda b, h, i: (b * nq + i, q0 + h)),
                  pl.BlockSpec((seq, blk), lambda b, h, i: (b, k0 + h)),
                  pl.BlockSpec((seq, blk), lambda b, h, i: (b, v0 + h)),
                  pl.BlockSpec((4, HEAD_DIM), lambda b, h, i: (0, 0)),
                  pl.BlockSpec((1, blk), lambda b, h, i: (0, 0)),
                  pl.BlockSpec((1, TQ, 2 * TK), lambda b, h, i: (NSA_HEADS + h, 0, 0))],
        out_specs=pl.BlockSpec((TQ, blk), lambda b, h, i: (b * nq + i, h)),
        compiler_params=pltpu.CompilerParams(
            dimension_semantics=("arbitrary", "arbitrary", "arbitrary"), vmem_limit_bytes=VMEM_LIMIT),
        name="diff_attention",
    )(main, main, main, lam_p, subln.reshape(1, blk), strips)


def _moba_kernel(q_ref, k_ref, v_ref, strip_ref, o_ref, kaug, kmean, *, seq):
    i = pl.program_id(2)
    hd = HEAD_DIM
    nb = seq // MOBA_BLOCK

    @pl.when(i == 0)
    def _():
        kmean[...] = jnp.zeros(kmean.shape, F32)
        col = lax.broadcasted_iota(jnp.int32, (MOBA_BLOCK, LANES), 1)

        def fill(n, _):
            r0 = pl.multiple_of(n * MOBA_BLOCK, MOBA_BLOCK)
            kb = k_ref[pl.ds(r0, MOBA_BLOCK), :]
            kaug[pl.ds(r0, MOBA_BLOCK), 0:LANES] = jnp.where(col == n, 1.0, 0.0).astype(BF16)
            kaug[pl.ds(r0, MOBA_BLOCK), LANES:2 * LANES] = kb
            kmean[pl.ds(n, 1), :] = jnp.mean(kb.astype(F32), axis=0, keepdims=True)
            return 0

        lax.fori_loop(0, nb, fill, 0)

    qblk = q_ref[...]
    qf = qblk.astype(F32) * SCALE
    lane = lax.broadcasted_iota(jnp.int32, (TQ, LANES), 1)
    n_idx = lax.broadcasted_iota(jnp.int32, (LANES, TQ), 0)
    past = n_idx < i
    km = kmean[...]
    lane_k = lax.broadcasted_iota(jnp.int32, km.shape, 1)
    qrows = []
    for hh in range(2):
        mine = (lane < hd) if hh == 0 else (lane >= hd)
        mine_k = (lane_k < hd) if hh == 0 else (lane_k >= hd)
        gate_t = _dot_nt(jnp.where(mine_k, km, 0.0).astype(BF16), qblk)
        sel_t = _topk_rows(jnp.where(past, gate_t, NEG), n_idx, MOBA_TOPK, -3.0e38)
        allowed = ((sel_t > 0.5) & past) | (n_idx == i)
        maskneg = jnp.where(allowed, 0.0, NEG).T
        qrows.append(jnp.concatenate([maskneg, jnp.where(mine, qf, 0.0)], axis=1))
    qaug = jnp.concatenate(qrows, axis=0).astype(BF16)
    init = (jnp.full((2 * TQ, 1), NEG, F32), jnp.zeros((2 * TQ, 1), F32), jnp.zeros((2 * TQ, LANES), F32))

    def far_body(j, carry):
        r0 = pl.multiple_of(j * TK, TK)
        return _online_step(_dot_nt(qaug, kaug[pl.ds(r0, TK), :]), v_ref[pl.ds(r0, TK), :], *carry)

    carry = lax.fori_loop(0, jnp.maximum(i - 1, 0), far_body, init)
    prev0 = pl.multiple_of(jnp.maximum(i - 1, 0) * TK, TK)
    nopad = jnp.where(i == 0, NEG, 0.0)
    s = (_dot_nt(qaug, kaug[pl.ds(prev0, TK), :])
         + jnp.concatenate([strip_ref[0, :, :TK], strip_ref[1, :, :TK]], axis=0) + nopad)
    carry = _online_step(s, v_ref[pl.ds(prev0, TK), :], *carry)
    diag0 = pl.multiple_of(i * TK, TK)
    s = (_dot_nt(qaug, kaug[pl.ds(diag0, TK), :])
         + jnp.concatenate([strip_ref[0, :, TK:], strip_ref[1, :, TK:]], axis=0))
    _, l, acc = _online_step(s, v_ref[pl.ds(diag0, TK), :], *carry)
    o = acc / l
    o_ref[...] = jnp.where(lane < hd, o[:TQ], o[TQ:]).astype(o_ref.dtype)


def _moba(main, strips, batch, seq):
    nq = seq // TQ
    pairs = MOBA_HEADS // 2
    return pl.pallas_call(
        functools.partial(_moba_kernel, seq=seq),
        out_shape=jax.ShapeDtypeStruct((batch * seq, MOBA_HEADS * HEAD_DIM), BF16),
        grid=(batch, pairs, nq),
        in_specs=[pl.BlockSpec((TQ, LANES), lambda b, h, i: (b * nq + i, h)),
                  pl.BlockSpec((seq, LANES), lambda b, h, i: (b, pairs + h)),
                  pl.BlockSpec((seq, LANES), lambda b, h, i: (b, 2 * pairs + h)),
                  pl.BlockSpec((2, TQ, 2 * TK), lambda b, h, i: (h, 0, 0))],
        out_specs=pl.BlockSpec((TQ, LANES), lambda b, h, i: (b * nq + i, h)),
        scratch_shapes=[pltpu.VMEM((seq, 2 * LANES), BF16), pltpu.VMEM((LANES, LANES), F32)],
        compiler_params=pltpu.CompilerParams(
            dimension_semantics=("arbitrary", "arbitrary", "arbitrary"), vmem_limit_bytes=VMEM_LIMIT),
        name="moba_attention",
    )(main, main, main, strips)


def _strip_geometry():
    r = np.arange(TQ)[:, None]
    c2 = np.arange(2 * TK)[None, :]
    c3 = np.arange(3 * TK)[None, :]
    rel_a = r + TK - c2
    rel_b = r + 2 * TK - c3
    cc = np.arange(CMP_NEAR)[None, :]
    rel_c = r + CMP_STRIDE * CMP_PAD - (CMP_BLOCK - 1) - CMP_STRIDE * cc
    return (rel_a, rel_a >= 0), (rel_b, rel_b >= 0), (rel_b, (rel_b >= 0) & (rel_b < WINDOW)), (rel_c, rel_c >= 0)


def _overlap_padded(seq):
    nc = (seq - CMP_BLOCK) // CMP_STRIDE + 1
    ncp = seq // CMP_STRIDE
    tok = np.arange(nc)[:, None] * CMP_STRIDE + np.arange(CMP_BLOCK)[None, :]
    ov = np.zeros((CMP_PAD + ncp + 2 * CMP_PER_TQ, LANES), np.float32)
    np.add.at(ov, (CMP_PAD + np.repeat(np.arange(nc), CMP_BLOCK), (tok // SEL_BLOCK).reshape(-1)),
              1.0 / CMP_BLOCK)
    return jnp.asarray(ov, dtype=BF16)


def _strips(bias_table):
    geo_a, geo_sel, geo_win, geo_cmp = _strip_geometry()
    return (_bias_strips(bias_table, geo_a[0], geo_a[1], 0, MOBA_HEADS, "bias_strip_diag"),
            _bias_strips(bias_table, geo_sel[0], geo_sel[1], 0, NSA_HEADS, "bias_strip_sel"),
            _bias_strips(bias_table, geo_win[0], geo_win[1], 0, NSA_HEADS, "bias_strip_win"),
            _bias_strips(bias_table, geo_cmp[0], geo_cmp[1], 0, NSA_HEADS, "bias_strip_cmp"))


def _even_heads(xf, batch, seq, bias_table, g_mix, w_in, pos_k, w1_k, w2_k, pos_v, w1_v, w2_v,
                lam_q1, lam_k1, lam_q2, lam_k2, subln, strips=None):
    d = xf.shape[1]
    strip_a, strip_sel, strip_win, strip_cmp = strips if strips is not None else _strips(bias_table)
    w_main = jnp.concatenate([w_in[:, 0:512], w_in[:, 768:1280], w_in[:, 1304:2840]], axis=1).astype(BF16)
    w_aux = jnp.concatenate([w_in[:, 512:768], w_in[:, 1280:1304],
                             jnp.zeros((d, 3 * LANES - 280), F32)], axis=1).astype(BF16)
    main0 = _norm_matmul(xf, g_mix, w_main, BF16, 1024, 512, "in_proj_even")
    aux0 = _norm_matmul(xf, g_mix, w_aux, F32, 1024, 3 * LANES, "in_proj_even_aux")

    nch = seq // CMP_STRIDE
    xkv = aux0[:, :2 * LANES].reshape(batch, nch, CMP_STRIDE, 2, NSA_KV_HEADS, HEAD_DIM)
    xkv = xkv.transpose(3, 0, 4, 1, 2, 5).reshape(2, batch * NSA_KV_HEADS, nch, CMP_STRIDE * HEAD_DIM)
    pos = jnp.stack([pos_k, pos_v]).reshape(2, 2, CMP_STRIDE * HEAD_DIM)
    ckv = _compress(xkv, pos, jnp.stack([w1_k, w1_v]).astype(BF16), jnp.stack([w2_k, w2_v]).astype(BF16))

    o_a = _nsa(main0, aux0, ckv, strip_sel, strip_win, strip_cmp, _overlap_padded(seq), batch, seq)
    lam_p = jnp.stack([lam_q1, lam_k1, lam_q2, lam_k2])
    lam_init = 0.8 - 0.6 * math.exp(-0.3 * 0)
    o_b = _diff(main0, lam_p, subln, strip_a, batch, seq, lam_init)
    return o_a, o_b, ckv


def _odd_heads(xf, batch, seq, bias_table, g_mix, w_in, strips=None):
    strip_a = (strips if strips is not None else _strips(bias_table))[0]
    main1 = _norm_matmul(xf, g_mix, w_in.astype(BF16), BF16, 1024, 512, "in_proj_odd")
    return _moba(main1, strip_a, batch, seq)


def kernel(x, bias_table, norm_mix, norm_mlp, norm_final, mlp_w1, mlp_w2, ev_w_in, ev_w_out,
           ev_cmp_pos_k, ev_cmp_w1_k, ev_cmp_w2_k, ev_cmp_pos_v, ev_cmp_w1_v, ev_cmp_w2_v,
           ev_lam_q1, ev_lam_k1, ev_lam_q2, ev_lam_k2, ev_subln, od_w_in, od_w_out):
    batch, seq, d = x.shape
    assert d == D_MODEL and (batch * seq) % 1024 == 0 and seq % TQ == 0
    assert SEL_TOPN * SEL_BLOCK <= seq <= LANES * SEL_BLOCK and seq >= MOBA_TOPK * MOBA_BLOCK
    xf = x.reshape(batch * seq, d)
    strips = _strips(bias_table)

    o_a, o_b, _ = _even_heads(xf, batch, seq, bias_table, norm_mix[0], ev_w_in[0],
                              ev_cmp_pos_k[0], ev_cmp_w1_k[0], ev_cmp_w2_k[0],
                              ev_cmp_pos_v[0], ev_cmp_w1_v[0], ev_cmp_w2_v[0],
                              ev_lam_q1[0], ev_lam_k1[0], ev_lam_q2[0], ev_lam_k2[0], ev_subln[0], strips)
    wo = ev_w_out[0].astype(BF16)
    x1 = _post(xf, o_a, o_b, 0, wo[:512], wo[512:], norm_mlp[0], mlp_w1[0].astype(BF16),
               mlp_w2[0].astype(BF16), norm_final, False, 512, 1024, "post_even")

    o_m = _odd_heads(x1, batch, seq, bias_table, norm_mix[1], od_w_in[0], strips)
    wo = od_w_out[0].astype(BF16)
    out = _post(x1, o_m, o_m, 1, wo[:512], wo[512:], norm_mlp[1], mlp_w1[1].astype(BF16),
                mlp_w2[1].astype(BF16), norm_final, True, 512, 1024, "post_odd")
    return out.reshape(batch, seq, d)
```

```python
import functools
import math

import numpy as np
import jax
import jax.numpy as jnp
from jax import lax
from jax.experimental import pallas as pl
from jax.experimental.pallas import tpu as pltpu

F32 = jnp.float32
BF16 = jnp.bfloat16

D_MODEL = 1024
HEAD_DIM = 64
EPS = 1e-6
NEG = -1e30
NUM_BUCKETS = 32
MAX_DISTANCE = 128
NSA_HEADS = 8
NSA_KV_HEADS = 2
NSA_GROUP = 4
CMP_BLOCK = 32
CMP_STRIDE = 16
SEL_BLOCK = 64
SEL_TOPN = 16
WINDOW = 512
FORCE_BONUS = 1e3
DIFF_HEADS = 4
MOBA_HEADS = 16
MOBA_BLOCK = 256
MOBA_TOPK = 3
D_FF = 4 * D_MODEL
SCALE = HEAD_DIM ** -0.5

LANES = 128
TQ = 256
TK = 256
FAR_REL = 113
CMP_PER_TQ = TQ // CMP_STRIDE
CMP_NEAR = 128
CMP_PAD = CMP_NEAR - 2 * CMP_PER_TQ
GROUP = 4
SUB = 2 * TK
BAND_PAD = (GROUP + 1) * TK
FRONT = GROUP * TK
LOG2E = math.log2(math.e)
QSCALE = SCALE * LOG2E
VMEM_LIMIT = 56 * 1024 * 1024


def _dot_nt(a, b):
    return lax.dot_general(a, b, (((1,), (1,)), ((), ())), preferred_element_type=F32)


def _dot(a, b):
    return jnp.dot(a, b, preferred_element_type=F32)


def _rms(x, g):
    return x * lax.rsqrt(jnp.mean(x * x, axis=-1, keepdims=True) + EPS) * g


def _t5_bucket_np(rel):
    n = np.maximum(rel, 0)
    max_exact = NUM_BUCKETS // 2
    large = max_exact + (np.log(np.maximum(n, 1).astype(np.float32) / np.float32(max_exact))
                         / np.float32(math.log(MAX_DISTANCE / max_exact))
                         * np.float32(NUM_BUCKETS - max_exact)).astype(np.int32)
    large = np.minimum(large, NUM_BUCKETS - 1)
    return np.where(n < max_exact, n, large).astype(np.int32)


def _strip_kernel(tbl_ref, bucket_ref, valid_ref, o_ref, *, head0):
    h = pl.program_id(0) + head0
    bucket = bucket_ref[...]
    last = tbl_ref[NUM_BUCKETS - 1, h]
    acc = jnp.zeros(bucket.shape, F32)
    for b in range(NUM_BUCKETS - 1):
        acc = jnp.where(bucket == b, tbl_ref[b, h] - last, acc)
    o_ref[0] = jnp.where(valid_ref[...] != 0, acc * LOG2E, NEG)


def _bias_strips(table, rel, valid, head0, nheads, name):
    rows, cols = rel.shape
    bucket = jnp.asarray(_t5_bucket_np(rel))
    validi = jnp.asarray(valid.astype(np.int32))
    return pl.pallas_call(
        functools.partial(_strip_kernel, head0=head0),
        out_shape=jax.ShapeDtypeStruct((nheads, rows, cols), F32),
        grid=(nheads,),
        in_specs=[pl.BlockSpec(memory_space=pltpu.SMEM),
                  pl.BlockSpec((rows, cols), lambda h: (0, 0)),
                  pl.BlockSpec((rows, cols), lambda h: (0, 0))],
        out_specs=pl.BlockSpec((1, rows, cols), lambda h: (h, 0, 0)),
        name=name,
    )(table, bucket, validi)


def _norm_matmul_kernel(x_ref, g_ref, w_ref, o_ref, xn_ref):
    @pl.when(pl.program_id(1) == 0)
    def _():
        xn_ref[...] = _rms(x_ref[...], g_ref[...]).astype(BF16)

    o_ref[...] = _dot(xn_ref[...], w_ref[...]).astype(o_ref.dtype)


def _norm_matmul(x, g, w, out_dtype, tm, tn, name):
    m, d = x.shape
    n = w.shape[1]
    return pl.pallas_call(
        _norm_matmul_kernel,
        out_shape=jax.ShapeDtypeStruct((m, n), out_dtype),
        grid=(m // tm, n // tn),
        in_specs=[pl.BlockSpec((tm, d), lambda i, j: (i, 0)),
                  pl.BlockSpec((1, d), lambda i, j: (0, 0)),
                  pl.BlockSpec((d, tn), lambda i, j: (0, j))],
        out_specs=pl.BlockSpec((tm, tn), lambda i, j: (i, j)),
        scratch_shapes=[pltpu.VMEM((tm, d), BF16)],
        compiler_params=pltpu.CompilerParams(
            dimension_semantics=("arbitrary", "arbitrary"), vmem_limit_bytes=VMEM_LIMIT),
        name=name,
    )(x, g.reshape(1, d), w)


def _post_kernel(x_ref, a0_ref, a1_ref, wo0_ref, wo1_ref, g_ref, w1_ref, w2_ref, gf_ref,
                 o_ref, acc_ref, xn_ref, *, final_norm):
    f = pl.program_id(1)

    @pl.when(f == 0)
    def _():
        x1 = x_ref[...] + _dot(a0_ref[...], wo0_ref[...]) + _dot(a1_ref[...], wo1_ref[...])
        acc_ref[...] = x1
        xn_ref[...] = _rms(x1, g_ref[...]).astype(BF16)

    h = jnp.square(jnp.maximum(_dot(xn_ref[...], w1_ref[...]), 0.0))
    acc_ref[...] += _dot(h.astype(BF16), w2_ref[...])

    @pl.when(f == pl.num_programs(1) - 1)
    def _():
        y = acc_ref[...]
        if final_norm:
            y = _rms(y, gf_ref[...])
        o_ref[...] = y


def _post(x, a0, a1, a1_colblk, wo0, wo1, g, w1, w2, gf, final_norm, tm, tf, name):
    m, d = x.shape
    k0 = wo0.shape[0]
    k1 = wo1.shape[0]
    ff = w1.shape[1]
    return pl.pallas_call(
        functools.partial(_post_kernel, final_norm=final_norm),
        out_shape=jax.ShapeDtypeStruct((m, d), F32),
        grid=(m // tm, ff // tf),
        in_specs=[pl.BlockSpec((tm, d), lambda i, f: (i, 0)),
                  pl.BlockSpec((tm, k0), lambda i, f: (i, 0)),
                  pl.BlockSpec((tm, k1), lambda i, f: (i, a1_colblk)),
                  pl.BlockSpec((k0, d), lambda i, f: (0, 0)),
                  pl.BlockSpec((k1, d), lambda i, f: (0, 0)),
                  pl.BlockSpec((1, d), lambda i, f: (0, 0)),
                  pl.BlockSpec((d, tf), lambda i, f: (0, f)),
                  pl.BlockSpec((tf, d), lambda i, f: (f, 0)),
                  pl.BlockSpec((1, d), lambda i, f: (0, 0))],
        out_specs=pl.BlockSpec((tm, d), lambda i, f: (i, 0)),
        scratch_shapes=[pltpu.VMEM((tm, d), F32), pltpu.VMEM((tm, d), BF16)],
        compiler_params=pltpu.CompilerParams(
            dimension_semantics=("arbitrary", "arbitrary"), vmem_limit_bytes=VMEM_LIMIT),
        name=name,
    )(x, a0, a1, wo0, wo1, g.reshape(1, d), w1, w2, gf.reshape(1, d))


def _compress_kernel(x_ref, pos_ref, w1_ref, w2_ref, o_ref):
    half = CMP_STRIDE * HEAD_DIM
    x = x_ref[0, 0].astype(BF16)
    w1a = w1_ref[0, :half, :]
    w1b = w1_ref[0, half:, :]
    first = _dot(x, w1a)
    second = _dot(x, w1b)
    nrow = second.shape[0]
    second = pltpu.roll(second, nrow - 1, 0)
    pos = pos_ref[0].astype(BF16)
    pa = jnp.broadcast_to(pos[0:1], (8, half))
    pb = jnp.broadcast_to(pos[1:2], (8, half))
    posterm = (_dot(pa, w1a) + _dot(pb, w1b))[0:1]
    hid = jax.nn.gelu(first + second + posterm)
    o_ref[0, 0] = _dot(hid.astype(BF16), w2_ref[0])


def _compress(xkv, pos, w1, w2):
    _, bg, nch, wide = xkv.shape
    hid = w1.shape[2]
    return pl.pallas_call(
        _compress_kernel,
        out_shape=jax.ShapeDtypeStruct((2, bg, nch, HEAD_DIM), F32),
        grid=(2, bg),
        in_specs=[pl.BlockSpec((1, 1, nch, wide), lambda k, n: (k, n, 0, 0)),
                  pl.BlockSpec((1, 2, wide), lambda k, n: (k, 0, 0)),
                  pl.BlockSpec((1, 2 * wide, hid), lambda k, n: (k, 0, 0)),
                  pl.BlockSpec((1, hid, HEAD_DIM), lambda k, n: (k, 0, 0))],
        out_specs=pl.BlockSpec((1, 1, nch, HEAD_DIM), lambda k, n: (k, n, 0, 0)),
        compiler_params=pltpu.CompilerParams(
            dimension_semantics=("arbitrary", "arbitrary"), vmem_limit_bytes=VMEM_LIMIT),
        name="nsa_compress",
    )(xkv, pos, w1, w2)


def _softmax_part(m, s):
    m_new = jnp.maximum(m, jnp.max(s, axis=1, keepdims=True))
    return m_new, jnp.exp2(m - m_new), jnp.exp2(s - m_new).astype(BF16)


def _flash_chain(q, k_sc, v_sc, s_sc, p_sc, far0, n_full, tail):
    rows = q.shape[0]
    assert GROUP * TK == 2 * SUB and tail[0][1] == SUB
    tail0 = tail[0][0]

    def qk(r0):
        return _dot_nt(q, k_sc[pl.ds(pl.multiple_of(r0, TK), SUB), :])

    s_sc[...] = qk(jnp.where(n_full > 0, far0, tail0))
    p_sc[...] = jnp.zeros(p_sc.shape, BF16)
    init = (jnp.full((rows, 1), NEG, F32), jnp.zeros((rows, v_sc.shape[1]), F32))

    def far_body(j, carry):
        m, acc = carry
        base = pl.multiple_of(far0 + j * (GROUP * TK), SUB)
        acc = acc + _dot(p_sc[...], v_sc[pl.ds(base - SUB, SUB), :])
        m, alpha, p = _softmax_part(m, s_sc[...])
        s_next = qk(base + SUB)
        acc = alpha * acc + _dot(p, v_sc[pl.ds(base, SUB), :])
        m, alpha, p = _softmax_part(m, s_next)
        s_sc[...] = qk(jnp.where(j + 1 < n_full, base + GROUP * TK, tail0))
        p_sc[...] = p
        return m, alpha * acc

    m, acc = lax.fori_loop(0, n_full, far_body, init)
    base = pl.multiple_of(far0 + n_full * (GROUP * TK), SUB)
    acc = acc + _dot(p_sc[...], v_sc[pl.ds(base - SUB, SUB), :])
    m, alpha, p = _softmax_part(m, s_sc[...] + tail[0][2])
    acc = alpha * acc + _dot(p, v_sc[pl.ds(tail0, SUB), :])
    return _flash_tail((m, acc), q, k_sc, v_sc, tail[1:])


def _flash_tail(carry, q, k_sc, v_sc, tail):
    m, acc = carry
    for r0, width, extra in tail:
        m, alpha, p = _softmax_part(m, _dot_nt(q, k_sc[pl.ds(r0, width), :]) + extra)
        acc = alpha * acc + _dot(p, v_sc[pl.ds(r0, width), :])
    return acc


def _tile_gate(cond):
    return jnp.where(cond, 0.0, NEG)


def _gate_pair(cond_a, cond_b):
    lane = lax.broadcasted_iota(jnp.int32, (1, 2 * TK), 1)
    return jnp.where(lane < TK, _tile_gate(cond_a), _tile_gate(cond_b))


def _topk_rows(score, idx, k, floor):
    sel = jnp.zeros(score.shape, F32)
    nrow = score.shape[0]
    for _ in range(k):
        mx = jnp.max(score, axis=0, keepdims=True)
        first = jnp.min(jnp.where(score == mx, idx, nrow), axis=0, keepdims=True)
        pick = idx == first
        sel = jnp.where(pick, 1.0, sel)
        score = jnp.where(pick, floor, score)
    return sel


def _nsa_kernel(q_ref, ks_ref, vs_ref, kw_ref, vw_ref, kc_ref, vc_ref, gate_ref,
                ssel_ref, swin_ref, scmp_ref, ov_ref, o_ref,
                kaug, vaug, kwp, vwp, kcp, vcp, s_sc, p_sc, *, seq):
    g = pl.program_id(1)
    i = pl.program_id(2)
    ncp = seq // CMP_STRIDE
    hd = HEAD_DIM
    rows = NSA_GROUP * TQ

    @pl.when(i == 0)
    def _():
        lane = lax.broadcasted_iota(jnp.int32, (BAND_PAD, LANES), 1)
        padflag = jnp.where(lane == hd, 1.0, 0.0).astype(BF16)
        kaug[0:BAND_PAD, 0:LANES] = jnp.zeros((BAND_PAD, LANES), BF16)
        kaug[0:BAND_PAD, LANES:2 * LANES] = padflag
        kwp[0:BAND_PAD, :] = padflag
        vaug[0:BAND_PAD, :] = jnp.zeros((BAND_PAD, LANES), BF16)
        vwp[0:BAND_PAD, :] = jnp.zeros((BAND_PAD, LANES), BF16)
        onescol = jnp.where(lax.broadcasted_iota(jnp.int32, (TK, LANES), 1) == hd, 1.0, 0.0)
        zero_hd = jnp.zeros((TK, hd), F32)
        col = lax.broadcasted_iota(jnp.int32, (TK, LANES), 1)
        row = lax.broadcasted_iota(jnp.int32, (TK, LANES), 0)

        def fill(c, _):
            src = pl.multiple_of(c * TK, TK)
            dst = pl.multiple_of(BAND_PAD + c * TK, TK)

            def pick(ref):
                both = ref[pl.ds(src, TK), :].astype(F32)
                return jnp.concatenate([jnp.where(g == 0, both[:, :hd], both[:, hd:]), zero_hd], axis=1)

            kaug[pl.ds(dst, TK), 0:LANES] = jnp.where((src + row) // SEL_BLOCK == col, 1.0, 0.0).astype(BF16)
            kaug[pl.ds(dst, TK), LANES:2 * LANES] = pick(ks_ref).astype(BF16)
            kwp[pl.ds(dst, TK), :] = pick(kw_ref).astype(BF16)
            vaug[pl.ds(dst, TK), :] = (pick(vs_ref) + onescol).astype(BF16)
            vwp[pl.ds(dst, TK), :] = (pick(vw_ref) + onescol).astype(BF16)
            return 0

        lax.fori_loop(0, seq // TK, fill, 0)
        lane_c = lax.broadcasted_iota(jnp.int32, (CMP_PAD, LANES), 1)
        kcp[0:CMP_PAD, :] = jnp.where(lane_c == hd, 1.0, 0.0).astype(BF16)
        vcp[0:CMP_PAD, :] = jnp.zeros((CMP_PAD, LANES), BF16)
        zc = jnp.zeros((ncp, hd), F32)
        onesc = jnp.where(lax.broadcasted_iota(jnp.int32, (ncp, LANES), 1) == hd, 1.0, 0.0)
        kcp[CMP_PAD:CMP_PAD + ncp, :] = jnp.concatenate([kc_ref[0, 0], zc], axis=1).astype(BF16)
        vcp[CMP_PAD:CMP_PAD + ncp, :] = (jnp.concatenate([vc_ref[0, 0], zc], axis=1) + onesc).astype(BF16)
        tail = kcp.shape[0] - CMP_PAD - ncp
        kcp[CMP_PAD + ncp:, :] = jnp.zeros((tail, LANES), BF16)
        vcp[CMP_PAD + ncp:, :] = jnp.zeros((tail, LANES), BF16)

    qblk = q_ref[...].astype(F32) * QSCALE
    lane_t = lax.broadcasted_iota(jnp.int32, (TQ, hd), 1)
    qtail = jnp.where(lane_t == 0, NEG, 0.0)
    qs = [qblk[:, r * hd:(r + 1) * hd] for r in range(NSA_GROUP)]
    q128 = jnp.concatenate([jnp.concatenate([qs[r], qtail], axis=1) for r in range(NSA_GROUP)],
                           axis=0).astype(BF16)

    def stack(ref, lo, hi):
        return jnp.concatenate([ref[r, :, lo:hi] for r in range(NSA_GROUP)], axis=0)

    s_far = _dot_nt(q128, kcp[CMP_PAD:CMP_PAD + ncp, :])
    c_idx = lax.broadcasted_iota(jnp.int32, (1, ncp), 1)
    far_ok = c_idx < CMP_PER_TQ * i - CMP_PAD
    near0 = pl.multiple_of(CMP_PER_TQ * i, CMP_PER_TQ)
    s_near = _dot_nt(q128, kcp[pl.ds(near0, CMP_NEAR), :]) + stack(scmp_ref, 0, CMP_NEAR)
    near_ok = s_near > 0.5 * NEG
    s_far = jnp.where(far_ok, s_far, NEG)
    m_c = jnp.maximum(jnp.max(s_far, axis=1, keepdims=True), jnp.max(s_near, axis=1, keepdims=True))
    p_far = jnp.where(far_ok, jnp.exp2(s_far - m_c), 0.0)
    p_near = jnp.where(near_ok, jnp.exp2(s_near - m_c), 0.0)
    den = jnp.maximum(jnp.sum(p_far, axis=1, keepdims=True) + jnp.sum(p_near, axis=1, keepdims=True), 1e-30)
    pc_far = p_far / den
    pc_near = p_near / den
    o_cmp = (_dot(pc_far.astype(BF16), vcp[CMP_PAD:CMP_PAD + ncp, :])
             + _dot(pc_near.astype(BF16), vcp[pl.ds(near0, CMP_NEAR), :]))[:, :hd]

    pcs_far = sum(pc_far[r * TQ:(r + 1) * TQ] for r in range(NSA_GROUP))
    pcs_near = sum(pc_near[r * TQ:(r + 1) * TQ] for r in range(NSA_GROUP))

    def hi_lo_dot(p, w):
        hi = p.astype(BF16)
        lo = (p - hi.astype(F32)).astype(BF16)
        return _dot(hi, w) + _dot(lo, w)

    imp = (hi_lo_dot(pcs_far, ov_ref[CMP_PAD:CMP_PAD + ncp, :])
           + hi_lo_dot(pcs_near, ov_ref[pl.ds(near0, CMP_NEAR), :]))
    imp_t = imp.T
    n_idx = lax.broadcasted_iota(jnp.int32, (LANES, TQ), 0)
    t_idx = TQ * i + lax.broadcasted_iota(jnp.int32, (LANES, TQ), 1)
    cur = t_idx // SEL_BLOCK
    forced = (n_idx == 0) | (n_idx == cur) | (n_idx == cur - 1)
    score = jnp.where(n_idx <= cur, imp_t + jnp.where(forced, FORCE_BONUS, 0.0), -1.0)
    sel_t = _topk_rows(score, n_idx, SEL_TOPN, -3.0e38)
    maskneg = jnp.where(sel_t > 0.5, 0.0, NEG).T
    qaug = jnp.concatenate(
        [jnp.concatenate([maskneg, qs[r], qtail], axis=1) for r in range(NSA_GROUP)],
        axis=0).astype(BF16)

    n_full = jnp.maximum(i - 2, 0) // GROUP
    done = GROUP * n_full
    win0 = pl.multiple_of(TQ * i, TK)
    mid = jnp.concatenate([jnp.broadcast_to(_tile_gate(i - 3 >= done), (rows, TK)), stack(ssel_ref, 0, TK)],
                          axis=1)
    tail = [(win0, 2 * TK, _gate_pair(i - 5 >= done, i - 4 >= done)),
            (win0 + 2 * TK, 2 * TK, mid),
            (win0 + 4 * TK, 2 * TK, stack(ssel_ref, TK, 3 * TK))]
    acc = _flash_chain(qaug, kaug, vaug, s_sc, p_sc, BAND_PAD, n_full, tail)
    o_sel = acc[:, :hd] / acc[:, hd:hd + 1]

    band0 = win0 + 3 * TK
    tail = [(band0, 2 * TK, stack(swin_ref, 0, 2 * TK)), (band0 + 2 * TK, TK, stack(swin_ref, 2 * TK, 3 * TK))]
    acc = _flash_tail((jnp.full((rows, 1), NEG, F32), jnp.zeros((rows, LANES), F32)), q128, kwp, vwp, tail)
    o_win = acc[:, :hd] / acc[:, hd:hd + 1]

    sig = jax.nn.sigmoid(gate_ref[...])
    outs = []
    for r in range(NSA_GROUP):
        def gcol(branch):
            c0 = branch * NSA_HEADS + r
            c1 = c0 + NSA_GROUP
            return jnp.where(g == 0, sig[:, c0:c0 + 1], sig[:, c1:c1 + 1])
        sl = slice(r * TQ, (r + 1) * TQ)
        outs.append(gcol(0) * o_cmp[sl] + gcol(1) * o_sel[sl] + gcol(2) * o_win[sl])
    o_ref[...] = jnp.concatenate(outs, axis=1).astype(o_ref.dtype)


def _nsa(main, aux, ckv, ssel, swin, scmp, ov, batch, seq):
    nq = seq // TQ
    ncp = seq // CMP_STRIDE
    gw = NSA_GROUP * HEAD_DIM
    kv_spec = lambda colblk: pl.BlockSpec((seq, LANES), lambda b, g, i: (b, colblk))
    strip_spec = lambda cols: pl.BlockSpec((NSA_GROUP, TQ, cols), lambda b, g, i: (g, 0, 0))
    return pl.pallas_call(
        functools.partial(_nsa_kernel, seq=seq),
        out_shape=jax.ShapeDtypeStruct((batch * seq, NSA_HEADS * HEAD_DIM), BF16),
        grid=(batch, NSA_KV_HEADS, nq),
        in_specs=[pl.BlockSpec((TQ, gw), lambda b, g, i: (b * nq + i, g)),
                  kv_spec(4), kv_spec(5), kv_spec(6), kv_spec(7),
                  pl.BlockSpec((1, 1, ncp, HEAD_DIM), lambda b, g, i: (0, b * NSA_KV_HEADS + g, 0, 0)),
                  pl.BlockSpec((1, 1, ncp, HEAD_DIM), lambda b, g, i: (1, b * NSA_KV_HEADS + g, 0, 0)),
                  pl.BlockSpec((TQ, LANES), lambda b, g, i: (b * nq + i, 2)),
                  strip_spec(3 * TK), strip_spec(3 * TK), strip_spec(CMP_NEAR),
                  pl.BlockSpec(ov.shape, lambda b, g, i: (0, 0))],
        out_specs=pl.BlockSpec((TQ, gw), lambda b, g, i: (b * nq + i, g)),
        scratch_shapes=[pltpu.VMEM((BAND_PAD + seq, 2 * LANES), BF16),
                        pltpu.VMEM((BAND_PAD + seq, LANES), BF16),
                        pltpu.VMEM((BAND_PAD + seq, LANES), BF16),
                        pltpu.VMEM((BAND_PAD + seq, LANES), BF16),
                        pltpu.VMEM((ov.shape[0], LANES), BF16),
                        pltpu.VMEM((ov.shape[0], LANES), BF16),
                        pltpu.VMEM((NSA_GROUP * TQ, SUB), F32), pltpu.VMEM((NSA_GROUP * TQ, SUB), BF16)],
        compiler_params=pltpu.CompilerParams(
            dimension_semantics=("arbitrary", "arbitrary", "arbitrary"), vmem_limit_bytes=VMEM_LIMIT),
        name="nsa_attention",
    )(main, main, main, main, main, ckv, ckv, aux, ssel, swin, scmp, ov)


def _ones_column():
    return jnp.where(lax.broadcasted_iota(jnp.int32, (TK, LANES), 1) == 0, 1.0, 0.0).astype(BF16)


def _causal_flash(q, k_sc, v_sc, s_sc, p_sc, strip2, i):
    rows = q.shape[0]
    n_full = jnp.maximum(i - 1, 0) // GROUP
    done = GROUP * n_full
    win0 = pl.multiple_of(i * TK, TK)
    mid = jnp.concatenate([jnp.broadcast_to(_tile_gate(i - 2 >= done), (rows, TK)),
                           strip2[:, :TK] + _tile_gate(i >= 1)], axis=1)
    tail = [(win0, 2 * TK, _gate_pair(i - 4 >= done, i - 3 >= done)),
            (win0 + 2 * TK, 2 * TK, mid),
            (win0 + 4 * TK, TK, strip2[:, TK:])]
    return _flash_chain(q, k_sc, v_sc, s_sc, p_sc, FRONT, n_full, tail)


def _diff_kernel(q_ref, k_ref, v_ref, lam_ref, sub_ref, strip_ref, o_ref, k_sc, v_sc, s_sc, p_sc, *,
                 lam_init, seq):
    i = pl.program_id(2)
    hd = HEAD_DIM

    @pl.when(i == 0)
    def _():
        k_sc[0:FRONT, :] = jnp.zeros((FRONT, k_sc.shape[1]), BF16)
        v_sc[0:FRONT, :] = jnp.zeros((FRONT, v_sc.shape[1]), BF16)
        ones = _ones_column()

        def fill(c, _):
            r0 = pl.multiple_of(c * TK, TK)
            k_sc[pl.ds(FRONT + r0, TK), :] = k_ref[pl.ds(r0, TK), :]
            v_sc[pl.ds(FRONT + r0, TK), 0:LANES] = v_ref[pl.ds(r0, TK), :]
            v_sc[pl.ds(FRONT + r0, TK), LANES:2 * LANES] = ones
            return 0

        lax.fori_loop(0, seq // TK, fill, 0)

    q = q_ref[...].astype(F32) * QSCALE
    lane = lax.broadcasted_iota(jnp.int32, (TQ, 2 * hd), 1)
    qst = jnp.concatenate([jnp.where(lane < hd, q, 0.0), jnp.where(lane < hd, 0.0, q)],
                          axis=0).astype(BF16)
    strip = strip_ref[0]
    acc = _causal_flash(qst, k_sc, v_sc, s_sc, p_sc, jnp.concatenate([strip, strip], axis=0), i)
    o = acc[:, :2 * hd] / acc[:, 2 * hd:2 * hd + 1]
    lam_p = lam_ref[...]
    lam = (jnp.exp(jnp.sum(lam_p[0:1] * lam_p[1:2], axis=1, keepdims=True))
           - jnp.exp(jnp.sum(lam_p[2:3] * lam_p[3:4], axis=1, keepdims=True)) + lam_init)
    d = o[:TQ] - lam * o[TQ:]
    o_ref[...] = (_rms(d, sub_ref[...]) * (1.0 - lam_init)).astype(o_ref.dtype)


def _diff(main, lam_p, subln, strips, batch, seq, lam_init):
    nq = seq // TQ
    blk = 2 * HEAD_DIM
    q0, k0, v0 = 1024 // blk, 1536 // blk, 2048 // blk
    return pl.pallas_call(
        functools.partial(_diff_kernel, lam_init=lam_init, seq=seq),
        out_shape=jax.ShapeDtypeStruct((batch * seq, DIFF_HEADS * blk), BF16),
        grid=(batch, DIFF_HEADS, nq),
        in_specs=[pl.BlockSpec((TQ, blk), lambda b, h, i: (b * nq + i, q0 + h)),
                  pl.BlockSpec((seq, blk), lambda b, h, i: (b, k0 + h)),
                  pl.BlockSpec((seq, blk), lambda b, h, i: (b, v0 + h)),
                  pl.BlockSpec((4, HEAD_DIM), lambda b, h, i: (0, 0)),
                  pl.BlockSpec((1, blk), lambda b, h, i: (0, 0)),
                  pl.BlockSpec((1, TQ, 2 * TK), lambda b, h, i: (NSA_HEADS + h, 0, 0))],
        out_specs=pl.BlockSpec((TQ, blk), lambda b, h, i: (b * nq + i, h)),
        scratch_shapes=[pltpu.VMEM((FRONT + seq, LANES), BF16), pltpu.VMEM((FRONT + seq, 2 * LANES), BF16),
                        pltpu.VMEM((2 * TQ, SUB), F32), pltpu.VMEM((2 * TQ, SUB), BF16)],
        compiler_params=pltpu.CompilerParams(
            dimension_semantics=("arbitrary", "arbitrary", "arbitrary"), vmem_limit_bytes=VMEM_LIMIT),
        name="diff_attention",
    )(main, main, main, lam_p, subln.reshape(1, blk), strips)


def _moba_kernel(q_ref, k_ref, v_ref, strip_ref, o_ref, kaug, vaug, kmean, s_sc, p_sc, *, seq):
    i = pl.program_id(2)
    hd = HEAD_DIM
    nb = seq // MOBA_BLOCK

    @pl.when(i == 0)
    def _():
        kmean[...] = jnp.zeros(kmean.shape, F32)
        kaug[0:FRONT, :] = jnp.zeros((FRONT, kaug.shape[1]), BF16)
        vaug[0:FRONT, :] = jnp.zeros((FRONT, vaug.shape[1]), BF16)
        col = lax.broadcasted_iota(jnp.int32, (MOBA_BLOCK, LANES), 1)
        ones = _ones_column()

        def fill(n, _):
            r0 = pl.multiple_of(n * MOBA_BLOCK, MOBA_BLOCK)
            kb = k_ref[pl.ds(r0, MOBA_BLOCK), :]
            kaug[pl.ds(FRONT + r0, MOBA_BLOCK), 0:LANES] = jnp.where(col == n, 1.0, 0.0).astype(BF16)
            kaug[pl.ds(FRONT + r0, MOBA_BLOCK), LANES:2 * LANES] = kb
            vaug[pl.ds(FRONT + r0, MOBA_BLOCK), 0:LANES] = v_ref[pl.ds(r0, MOBA_BLOCK), :]
            vaug[pl.ds(FRONT + r0, MOBA_BLOCK), LANES:2 * LANES] = ones
            kmean[pl.ds(n, 1), :] = jnp.mean(kb.astype(F32), axis=0, keepdims=True)
            return 0

        lax.fori_loop(0, nb, fill, 0)

    qblk = q_ref[...]
    qf = qblk.astype(F32) * QSCALE
    lane = lax.broadcasted_iota(jnp.int32, (TQ, LANES), 1)
    n_idx = lax.broadcasted_iota(jnp.int32, (LANES, TQ), 0)
    past = n_idx < i
    km = kmean[...]
    lane_k = lax.broadcasted_iota(jnp.int32, km.shape, 1)
    qrows = []
    for hh in range(2):
        mine = (lane < hd) if hh == 0 else (lane >= hd)
        mine_k = (lane_k < hd) if hh == 0 else (lane_k >= hd)
        gate_t = _dot_nt(jnp.where(mine_k, km, 0.0).astype(BF16), qblk)
        sel_t = _topk_rows(jnp.where(past, gate_t, NEG), n_idx, MOBA_TOPK, -3.0e38)
        allowed = ((sel_t > 0.5) & past) | (n_idx == i)
        maskneg = jnp.where(allowed, 0.0, NEG).T
        qrows.append(jnp.concatenate([maskneg, jnp.where(mine, qf, 0.0)], axis=1))
    qaug = jnp.concatenate(qrows, axis=0).astype(BF16)
    acc = _causal_flash(qaug, kaug, vaug, s_sc, p_sc, jnp.concatenate([strip_ref[0], strip_ref[1]], axis=0), i)
    o = acc[:, :LANES] / acc[:, LANES:LANES + 1]
    o_ref[...] = jnp.where(lane < hd, o[:TQ], o[TQ:]).astype(o_ref.dtype)


def _moba(main, strips, batch, seq):
    nq = seq // TQ
    pairs = MOBA_HEADS // 2
    return pl.pallas_call(
        functools.partial(_moba_kernel, seq=seq),
        out_shape=jax.ShapeDtypeStruct((batch * seq, MOBA_HEADS * HEAD_DIM), BF16),
        grid=(batch, pairs, nq),
        in_specs=[pl.BlockSpec((TQ, LANES), lambda b, h, i: (b * nq + i, h)),
                  pl.BlockSpec((seq, LANES), lambda b, h, i: (b, pairs + h)),
                  pl.BlockSpec((seq, LANES), lambda b, h, i: (b, 2 * pairs + h)),
                  pl.BlockSpec((2, TQ, 2 * TK), lambda b, h, i: (h, 0, 0))],
        out_specs=pl.BlockSpec((TQ, LANES), lambda b, h, i: (b * nq + i, h)),
        scratch_shapes=[pltpu.VMEM((FRONT + seq, 2 * LANES), BF16), pltpu.VMEM((FRONT + seq, 2 * LANES), BF16),
                        pltpu.VMEM((LANES, LANES), F32), pltpu.VMEM((2 * TQ, SUB), F32),
                        pltpu.VMEM((2 * TQ, SUB), BF16)],
        compiler_params=pltpu.CompilerParams(
            dimension_semantics=("arbitrary", "arbitrary", "arbitrary"), vmem_limit_bytes=VMEM_LIMIT),
        name="moba_attention",
    )(main, main, main, strips)


def _strip_geometry():
    r = np.arange(TQ)[:, None]
    c2 = np.arange(2 * TK)[None, :]
    c3 = np.arange(3 * TK)[None, :]
    rel_a = r + TK - c2
    rel_b = r + 2 * TK - c3
    cc = np.arange(CMP_NEAR)[None, :]
    rel_c = r + CMP_STRIDE * CMP_PAD - (CMP_BLOCK - 1) - CMP_STRIDE * cc
    return (rel_a, rel_a >= 0), (rel_b, rel_b >= 0), (rel_b, (rel_b >= 0) & (rel_b < WINDOW)), (rel_c, rel_c >= 0)


def _overlap_padded(seq):
    nc = (seq - CMP_BLOCK) // CMP_STRIDE + 1
    ncp = seq // CMP_STRIDE
    tok = np.arange(nc)[:, None] * CMP_STRIDE + np.arange(CMP_BLOCK)[None, :]
    ov = np.zeros((CMP_PAD + ncp + 2 * CMP_PER_TQ, LANES), np.float32)
    np.add.at(ov, (CMP_PAD + np.repeat(np.arange(nc), CMP_BLOCK), (tok // SEL_BLOCK).reshape(-1)),
              1.0 / CMP_BLOCK)
    return jnp.asarray(ov, dtype=BF16)


def _strips(bias_table):
    geo_a, geo_sel, geo_win, geo_cmp = _strip_geometry()
    return (_bias_strips(bias_table, geo_a[0], geo_a[1], 0, MOBA_HEADS, "bias_strip_diag"),
            _bias_strips(bias_table, geo_sel[0], geo_sel[1], 0, NSA_HEADS, "bias_strip_sel"),
            _bias_strips(bias_table, geo_win[0], geo_win[1], 0, NSA_HEADS, "bias_strip_win"),
            _bias_strips(bias_table, geo_cmp[0], geo_cmp[1], 0, NSA_HEADS, "bias_strip_cmp"))


def _even_heads(xf, batch, seq, bias_table, g_mix, w_in, pos_k, w1_k, w2_k, pos_v, w1_v, w2_v,
                lam_q1, lam_k1, lam_q2, lam_k2, subln, strips=None):
    d = xf.shape[1]
    strip_a, strip_sel, strip_win, strip_cmp = strips if strips is not None else _strips(bias_table)
    w_main = jnp.concatenate([w_in[:, 0:512], w_in[:, 768:1280], w_in[:, 1304:2840]], axis=1).astype(BF16)
    w_aux = jnp.concatenate([w_in[:, 512:768], w_in[:, 1280:1304],
                             jnp.zeros((d, 3 * LANES - 280), F32)], axis=1).astype(BF16)
    main0 = _norm_matmul(xf, g_mix, w_main, BF16, 1024, 512, "in_proj_even")
    aux0 = _norm_matmul(xf, g_mix, w_aux, F32, 1024, 3 * LANES, "in_proj_even_aux")

    nch = seq // CMP_STRIDE
    xkv = aux0[:, :2 * LANES].reshape(batch, nch, CMP_STRIDE, 2, NSA_KV_HEADS, HEAD_DIM)
    xkv = xkv.transpose(3, 0, 4, 1, 2, 5).reshape(2, batch * NSA_KV_HEADS, nch, CMP_STRIDE * HEAD_DIM)
    pos = jnp.stack([pos_k, pos_v]).reshape(2, 2, CMP_STRIDE * HEAD_DIM)
    ckv = _compress(xkv, pos, jnp.stack([w1_k, w1_v]).astype(BF16), jnp.stack([w2_k, w2_v]).astype(BF16))

    o_a = _nsa(main0, aux0, ckv, strip_sel, strip_win, strip_cmp, _overlap_padded(seq), batch, seq)
    lam_p = jnp.stack([lam_q1, lam_k1, lam_q2, lam_k2])
    lam_init = 0.8 - 0.6 * math.exp(-0.3 * 0)
    o_b = _diff(main0, lam_p, subln, strip_a, batch, seq, lam_init)
    return o_a, o_b, ckv


def _odd_heads(xf, batch, seq, bias_table, g_mix, w_in, strips=None):
    strip_a = (strips if strips is not None else _strips(bias_table))[0]
    main1 = _norm_matmul(xf, g_mix, w_in.astype(BF16), BF16, 1024, 512, "in_proj_odd")
    return _moba(main1, strip_a, batch, seq)


def kernel(x, bias_table, norm_mix, norm_mlp, norm_final, mlp_w1, mlp_w2, ev_w_in, ev_w_out,
           ev_cmp_pos_k, ev_cmp_w1_k, ev_cmp_w2_k, ev_cmp_pos_v, ev_cmp_w1_v, ev_cmp_w2_v,
           ev_lam_q1, ev_lam_k1, ev_lam_q2, ev_lam_k2, ev_subln, od_w_in, od_w_out):
    batch, seq, d = x.shape
    assert d == D_MODEL and (batch * seq) % 1024 == 0 and seq % TQ == 0
    assert SEL_TOPN * SEL_BLOCK <= seq <= LANES * SEL_BLOCK and seq >= MOBA_TOPK * MOBA_BLOCK
    xf = x.reshape(batch * seq, d)
    strips = _strips(bias_table)

    o_a, o_b, _ = _even_heads(xf, batch, seq, bias_table, norm_mix[0], ev_w_in[0],
                              ev_cmp_pos_k[0], ev_cmp_w1_k[0], ev_cmp_w2_k[0],
                              ev_cmp_pos_v[0], ev_cmp_w1_v[0], ev_cmp_w2_v[0],
                              ev_lam_q1[0], ev_lam_k1[0], ev_lam_q2[0], ev_lam_k2[0], ev_subln[0], strips)
    wo = ev_w_out[0].astype(BF16)
    x1 = _post(xf, o_a, o_b, 0, wo[:512], wo[512:], norm_mlp[0], mlp_w1[0].astype(BF16),
               mlp_w2[0].astype(BF16), norm_final, False, 512, 1024, "post_even")

    o_m = _odd_heads(x1, batch, seq, bias_table, norm_mix[1], od_w_in[0], strips)
    wo = od_w_out[0].astype(BF16)
    out = _post(x1, o_m, o_m, 1, wo[:512], wo[512:], norm_mlp[1], mlp_w1[1].astype(BF16),
                mlp_w2[1].astype(BF16), norm_final, True, 512, 1024, "post_odd")
    return out.reshape(batch, seq, d)
```

```python
import functools
import math

import numpy as np
import jax
import jax.numpy as jnp
from jax import lax
from jax.experimental import pallas as pl
from jax.experimental.pallas import tpu as pltpu

F32 = jnp.float32
BF16 = jnp.bfloat16

D_MODEL = 1024
HEAD_DIM = 64
EPS = 1e-6
NEG = -1e30
NUM_BUCKETS = 32
MAX_DISTANCE = 128
NSA_HEADS = 8
NSA_KV_HEADS = 2
NSA_GROUP = 4
CMP_BLOCK = 32
CMP_STRIDE = 16
SEL_BLOCK = 64
SEL_TOPN = 16
WINDOW = 512
FORCE_BONUS = 1e3
DIFF_HEADS = 4
MOBA_HEADS = 16
MOBA_BLOCK = 256
MOBA_TOPK = 3
D_FF = 4 * D_MODEL
SCALE = HEAD_DIM ** -0.5

LANES = 128
TQ = 256
TQD = 512
TK = 256
FAR_REL = 113
CMP_PER_TQ = TQ // CMP_STRIDE
CMP_NEAR = 128
CMP_PAD = CMP_NEAR - 2 * CMP_PER_TQ
GROUP = 4
SUB = 2 * TK
BAND_PAD = (GROUP + 1) * TK
FRONT = GROUP * TK
LOG2E = math.log2(math.e)
QSCALE = SCALE * LOG2E
VMEM_LIMIT = 56 * 1024 * 1024


def _attention_params():
    return pltpu.CompilerParams(dimension_semantics=("arbitrary", "arbitrary", "arbitrary"),
                                vmem_limit_bytes=VMEM_LIMIT)


def _dot_nt(a, b):
    return lax.dot_general(a, b, (((1,), (1,)), ((), ())), preferred_element_type=F32)


def _dot(a, b):
    return jnp.dot(a, b, preferred_element_type=F32)


def _rms(x, g):
    return x * lax.rsqrt(jnp.mean(x * x, axis=-1, keepdims=True) + EPS) * g


def _t5_bucket_np(rel):
    n = np.maximum(rel, 0)
    max_exact = NUM_BUCKETS // 2
    large = max_exact + (np.log(np.maximum(n, 1).astype(np.float32) / np.float32(max_exact))
                         / np.float32(math.log(MAX_DISTANCE / max_exact))
                         * np.float32(NUM_BUCKETS - max_exact)).astype(np.int32)
    large = np.minimum(large, NUM_BUCKETS - 1)
    return np.where(n < max_exact, n, large).astype(np.int32)


def _strip_kernel(tbl_ref, bucket_ref, valid_ref, o_ref, *, head0):
    h = pl.program_id(0) + head0
    bucket = bucket_ref[...]
    last = tbl_ref[NUM_BUCKETS - 1, h]
    acc = jnp.zeros(bucket.shape, F32)
    for b in range(NUM_BUCKETS - 1):
        acc = jnp.where(bucket == b, tbl_ref[b, h] - last, acc)
    o_ref[0] = jnp.where(valid_ref[...] != 0, acc * LOG2E, NEG)


def _bias_strips(table, rel, valid, head0, nheads, name):
    rows, cols = rel.shape
    bucket = jnp.asarray(_t5_bucket_np(rel))
    validi = jnp.asarray(valid.astype(np.int32))
    return pl.pallas_call(
        functools.partial(_strip_kernel, head0=head0),
        out_shape=jax.ShapeDtypeStruct((nheads, rows, cols), F32),
        grid=(nheads,),
        in_specs=[pl.BlockSpec(memory_space=pltpu.SMEM),
                  pl.BlockSpec((rows, cols), lambda h: (0, 0)),
                  pl.BlockSpec((rows, cols), lambda h: (0, 0))],
        out_specs=pl.BlockSpec((1, rows, cols), lambda h: (h, 0, 0)),
        name=name,
    )(table, bucket, validi)


def _norm_matmul_kernel(x_ref, g_ref, w_ref, o_ref, xn_ref):
    @pl.when(pl.program_id(1) == 0)
    def _():
        xn_ref[...] = _rms(x_ref[...], g_ref[...]).astype(BF16)

    o_ref[...] = _dot(xn_ref[...], w_ref[...]).astype(o_ref.dtype)


def _norm_matmul(x, g, w, out_dtype, tm, tn, name):
    m, d = x.shape
    n = w.shape[1]
    return pl.pallas_call(
        _norm_matmul_kernel,
        out_shape=jax.ShapeDtypeStruct((m, n), out_dtype),
        grid=(m // tm, n // tn),
        in_specs=[pl.BlockSpec((tm, d), lambda i, j: (i, 0)),
                  pl.BlockSpec((1, d), lambda i, j: (0, 0)),
                  pl.BlockSpec((d, tn), lambda i, j: (0, j))],
        out_specs=pl.BlockSpec((tm, tn), lambda i, j: (i, j)),
        scratch_shapes=[pltpu.VMEM((tm, d), BF16)],
        compiler_params=pltpu.CompilerParams(
            dimension_semantics=("arbitrary", "arbitrary"), vmem_limit_bytes=VMEM_LIMIT),
        name=name,
    )(x, g.reshape(1, d), w)


def _in_proj_even_kernel(x_ref, g_ref, w_ref, wa_ref, o_ref, kc0_ref, kc1_ref, vc0_ref, vc1_ref, gate_ref,
                         xn_ref):
    hd = HEAD_DIM

    @pl.when(pl.program_id(1) == 0)
    def _():
        xn = _rms(x_ref[...], g_ref[...]).astype(BF16)
        xn_ref[...] = xn
        aux = _dot(xn, wa_ref[...])
        kc0_ref[...] = aux[:, 0:hd]
        kc1_ref[...] = aux[:, hd:2 * hd]
        vc0_ref[...] = aux[:, 2 * hd:3 * hd]
        vc1_ref[...] = aux[:, 3 * hd:4 * hd]
        gate_ref[...] = aux[:, 4 * hd:]

    o_ref[...] = _dot(xn_ref[...], w_ref[...]).astype(o_ref.dtype)


def _in_proj_even(x, g, w_main, w_aux, tm, tn):
    m, d = x.shape
    n = w_main.shape[1]
    na = w_aux.shape[1]
    row_spec = lambda width: pl.BlockSpec((tm, width), lambda i, j: (i, 0))
    cmp_shape = jax.ShapeDtypeStruct((m, HEAD_DIM), F32)
    return pl.pallas_call(
        _in_proj_even_kernel,
        out_shape=(jax.ShapeDtypeStruct((m, n), BF16), cmp_shape, cmp_shape, cmp_shape, cmp_shape,
                   jax.ShapeDtypeStruct((m, na - 4 * HEAD_DIM), F32)),
        grid=(m // tm, n // tn),
        in_specs=[row_spec(d),
                  pl.BlockSpec((1, d), lambda i, j: (0, 0)),
                  pl.BlockSpec((d, tn), lambda i, j: (0, j)),
                  pl.BlockSpec((d, na), lambda i, j: (0, 0))],
        out_specs=(pl.BlockSpec((tm, tn), lambda i, j: (i, j)), row_spec(HEAD_DIM), row_spec(HEAD_DIM),
                   row_spec(HEAD_DIM), row_spec(HEAD_DIM), row_spec(na - 4 * HEAD_DIM)),
        scratch_shapes=[pltpu.VMEM((tm, d), BF16)],
        compiler_params=pltpu.CompilerParams(
            dimension_semantics=("arbitrary", "arbitrary"), vmem_limit_bytes=VMEM_LIMIT),
        name="in_proj_even",
    )(x, g.reshape(1, d), w_main, w_aux)


def _post_kernel(x_ref, a0_ref, a1_ref, wo0_ref, wo1_ref, g_ref, w1_ref, w2_ref, gf_ref,
                 o_ref, acc_ref, xn_ref, *, final_norm):
    f = pl.program_id(1)

    @pl.when(f == 0)
    def _():
        x1 = x_ref[...] + _dot(a0_ref[...], wo0_ref[...]) + _dot(a1_ref[...], wo1_ref[...])
        acc_ref[...] = x1
        xn_ref[...] = _rms(x1, g_ref[...]).astype(BF16)

    h = jnp.square(jnp.maximum(_dot(xn_ref[...], w1_ref[...]), 0.0))
    acc_ref[...] += _dot(h.astype(BF16), w2_ref[...])

    @pl.when(f == pl.num_programs(1) - 1)
    def _():
        y = acc_ref[...]
        if final_norm:
            y = _rms(y, gf_ref[...])
        o_ref[...] = y


def _post(x, a0, a1, a1_colblk, wo0, wo1, g, w1, w2, gf, final_norm, tm, tf, name):
    m, d = x.shape
    k0 = wo0.shape[0]
    k1 = wo1.shape[0]
    ff = w1.shape[1]
    return pl.pallas_call(
        functools.partial(_post_kernel, final_norm=final_norm),
        out_shape=jax.ShapeDtypeStruct((m, d), F32),
        grid=(m // tm, ff // tf),
        in_specs=[pl.BlockSpec((tm, d), lambda i, f: (i, 0)),
                  pl.BlockSpec((tm, k0), lambda i, f: (i, 0)),
                  pl.BlockSpec((tm, k1), lambda i, f: (i, a1_colblk)),
                  pl.BlockSpec((k0, d), lambda i, f: (0, 0)),
                  pl.BlockSpec((k1, d), lambda i, f: (0, 0)),
                  pl.BlockSpec((1, d), lambda i, f: (0, 0)),
                  pl.BlockSpec((d, tf), lambda i, f: (0, f)),
                  pl.BlockSpec((tf, d), lambda i, f: (f, 0)),
                  pl.BlockSpec((1, d), lambda i, f: (0, 0))],
        out_specs=pl.BlockSpec((tm, d), lambda i, f: (i, 0)),
        scratch_shapes=[pltpu.VMEM((tm, d), F32), pltpu.VMEM((tm, d), BF16)],
        compiler_params=pltpu.CompilerParams(
            dimension_semantics=("arbitrary", "arbitrary"), vmem_limit_bytes=VMEM_LIMIT),
        name=name,
    )(x, a0, a1, wo0, wo1, g.reshape(1, d), w1, w2, gf.reshape(1, d))


def _compress_kernel(kc0_ref, kc1_ref, vc0_ref, vc1_ref, pos_ref, w1_ref, w2_ref, o_ref):
    half = CMP_STRIDE * HEAD_DIM
    for kv, refs in enumerate(((kc0_ref, kc1_ref), (vc0_ref, vc1_ref))):
        w1a = w1_ref[kv, :half, :]
        w1b = w1_ref[kv, half:, :]
        pos = pos_ref[kv].astype(BF16)
        pa = jnp.broadcast_to(pos[0:1], (8, half))
        pb = jnp.broadcast_to(pos[1:2], (8, half))
        posterm = (_dot(pa, w1a) + _dot(pb, w1b))[0:1]
        for grp, x_ref in enumerate(refs):
            x = x_ref[0].astype(BF16)
            first = _dot(x, w1a)
            second = _dot(x, w1b)
            second = pltpu.roll(second, second.shape[0] - 1, 0)
            hid = jax.nn.gelu(first + second + posterm)
            o_ref[kv, grp] = _dot(hid.astype(BF16), w2_ref[kv])


def _compress(kc0, kc1, vc0, vc1, pos, w1, w2):
    batch, nch, wide = kc0.shape
    x_spec = pl.BlockSpec((1, nch, wide), lambda b: (b, 0, 0))
    full = lambda a: pl.BlockSpec(a.shape, lambda b: (0,) * a.ndim)
    return pl.pallas_call(
        _compress_kernel,
        out_shape=jax.ShapeDtypeStruct((2, batch * NSA_KV_HEADS, nch, HEAD_DIM), F32),
        grid=(batch,),
        in_specs=[x_spec, x_spec, x_spec, x_spec, full(pos), full(w1), full(w2)],
        out_specs=pl.BlockSpec((2, NSA_KV_HEADS, nch, HEAD_DIM), lambda b: (0, b, 0, 0)),
        compiler_params=pltpu.CompilerParams(
            dimension_semantics=("arbitrary",), vmem_limit_bytes=VMEM_LIMIT),
        name="nsa_compress",
    )(kc0, kc1, vc0, vc1, pos, w1, w2)


def _softmax_part(m, s):
    m_new = jnp.maximum(m, jnp.max(s, axis=1, keepdims=True))
    return m_new, jnp.exp2(m - m_new), jnp.exp2(s - m_new).astype(BF16)


def _flash_chain(q, k_sc, v_sc, s_sc, p_sc, far0, n_full, tail):
    rows = q.shape[0]
    assert GROUP * TK == 2 * SUB and tail[0][1] == SUB
    tail0 = tail[0][0]

    def qk(r0):
        return _dot_nt(q, k_sc[pl.ds(pl.multiple_of(r0, TK), SUB), :])

    s_sc[...] = qk(jnp.where(n_full > 0, far0, tail0))
    p_sc[...] = jnp.zeros(p_sc.shape, BF16)
    init = (jnp.full((rows, 1), NEG, F32), jnp.zeros((rows, v_sc.shape[1]), F32))

    def far_body(j, carry):
        m, acc = carry
        base = pl.multiple_of(far0 + j * (GROUP * TK), SUB)
        acc = acc + _dot(p_sc[...], v_sc[pl.ds(base - SUB, SUB), :])
        m, alpha, p = _softmax_part(m, s_sc[...])
        s_next = qk(base + SUB)
        acc = alpha * acc + _dot(p, v_sc[pl.ds(base, SUB), :])
        m, alpha, p = _softmax_part(m, s_next)
        s_sc[...] = qk(jnp.where(j + 1 < n_full, base + GROUP * TK, tail0))
        p_sc[...] = p
        return m, alpha * acc

    m, acc = lax.fori_loop(0, n_full, far_body, init)
    base = pl.multiple_of(far0 + n_full * (GROUP * TK), SUB)
    acc = acc + _dot(p_sc[...], v_sc[pl.ds(base - SUB, SUB), :])
    m, alpha, p = _softmax_part(m, s_sc[...] + tail[0][2])
    acc = alpha * acc + _dot(p, v_sc[pl.ds(tail0, SUB), :])
    return _flash_tail((m, acc), q, k_sc, v_sc, tail[1:])


def _flash_tail(carry, q, k_sc, v_sc, tail):
    m, acc = carry
    for r0, width, extra in tail:
        m, alpha, p = _softmax_part(m, _dot_nt(q, k_sc[pl.ds(r0, width), :]) + extra)
        acc = alpha * acc + _dot(p, v_sc[pl.ds(r0, width), :])
    return acc


def _tile_gate(cond):
    return jnp.where(cond, 0.0, NEG)


def _gate_pair(cond_a, cond_b):
    lane = lax.broadcasted_iota(jnp.int32, (1, 2 * TK), 1)
    return jnp.where(lane < TK, _tile_gate(cond_a), _tile_gate(cond_b))


def _tail_steps(win0, t0, done, strip, near_ok):
    rows = strip.shape[0]
    mid = jnp.concatenate([jnp.broadcast_to(_tile_gate(t0 + 2 >= done), (rows, TK)),
                           strip[:, :TK] + _tile_gate(near_ok)], axis=1)
    return [(win0, SUB, _gate_pair(t0 >= done, t0 + 1 >= done)),
            (win0 + SUB, SUB, mid),
            (win0 + 2 * SUB, SUB, strip[:, TK:])]


def _topk_rows(score, idx, k, floor):
    sel = jnp.zeros(score.shape, F32)
    nrow = score.shape[0]
    for _ in range(k):
        mx = jnp.max(score, axis=0, keepdims=True)
        first = jnp.min(jnp.where(score == mx, idx, nrow), axis=0, keepdims=True)
        pick = idx == first
        sel = jnp.where(pick, 1.0, sel)
        score = jnp.where(pick, floor, score)
    return sel


def _nsa_kernel(q_ref, ks_ref, vs_ref, kw_ref, vw_ref, kc_ref, vc_ref, gate_ref,
                ssel_ref, swin_ref, scmp_ref, ov_ref, o_ref,
                kaug, vaug, kwp, vwp, kcp, vcp, s_sc, p_sc, *, seq):
    g = pl.program_id(1)
    i = pl.program_id(2)
    ncp = seq // CMP_STRIDE
    hd = HEAD_DIM
    rows = NSA_GROUP * TQ

    @pl.when(i == 0)
    def _():
        lane = lax.broadcasted_iota(jnp.int32, (BAND_PAD, LANES), 1)
        padflag = jnp.where(lane == hd, 1.0, 0.0).astype(BF16)
        kaug[0:BAND_PAD, 0:LANES] = jnp.zeros((BAND_PAD, LANES), BF16)
        kaug[0:BAND_PAD, LANES:2 * LANES] = padflag
        kwp[0:BAND_PAD, :] = padflag
        vaug[0:BAND_PAD, :] = jnp.zeros((BAND_PAD, LANES), BF16)
        vwp[0:BAND_PAD, :] = jnp.zeros((BAND_PAD, LANES), BF16)
        onescol = jnp.where(lax.broadcasted_iota(jnp.int32, (TK, LANES), 1) == hd, 1.0, 0.0)
        zero_hd = jnp.zeros((TK, hd), F32)
        col = lax.broadcasted_iota(jnp.int32, (TK, LANES), 1)
        row = lax.broadcasted_iota(jnp.int32, (TK, LANES), 0)

        def fill(c, _):
            src = pl.multiple_of(c * TK, TK)
            dst = pl.multiple_of(BAND_PAD + c * TK, TK)

            def pick(ref):
                both = ref[pl.ds(src, TK), :].astype(F32)
                return jnp.concatenate([jnp.where(g == 0, both[:, :hd], both[:, hd:]), zero_hd], axis=1)

            kaug[pl.ds(dst, TK), 0:LANES] = jnp.where((src + row) // SEL_BLOCK == col, 1.0, 0.0).astype(BF16)
            kaug[pl.ds(dst, TK), LANES:2 * LANES] = pick(ks_ref).astype(BF16)
            kwp[pl.ds(dst, TK), :] = pick(kw_ref).astype(BF16)
            vaug[pl.ds(dst, TK), :] = (pick(vs_ref) + onescol).astype(BF16)
            vwp[pl.ds(dst, TK), :] = (pick(vw_ref) + onescol).astype(BF16)
            return 0

        lax.fori_loop(0, seq // TK, fill, 0)
        lane_c = lax.broadcasted_iota(jnp.int32, (CMP_PAD, LANES), 1)
        kcp[0:CMP_PAD, :] = jnp.where(lane_c == hd, 1.0, 0.0).astype(BF16)
        vcp[0:CMP_PAD, :] = jnp.zeros((CMP_PAD, LANES), BF16)
        zc = jnp.zeros((ncp, hd), F32)
        onesc = jnp.where(lax.broadcasted_iota(jnp.int32, (ncp, LANES), 1) == hd, 1.0, 0.0)
        kcp[CMP_PAD:CMP_PAD + ncp, :] = jnp.concatenate([kc_ref[0, 0], zc], axis=1).astype(BF16)
        vcp[CMP_PAD:CMP_PAD + ncp, :] = (jnp.concatenate([vc_ref[0, 0], zc], axis=1) + onesc).astype(BF16)
        tail = kcp.shape[0] - CMP_PAD - ncp
        kcp[CMP_PAD + ncp:, :] = jnp.zeros((tail, LANES), BF16)
        vcp[CMP_PAD + ncp:, :] = jnp.zeros((tail, LANES), BF16)

    qblk = q_ref[...].astype(F32) * QSCALE
    lane_t = lax.broadcasted_iota(jnp.int32, (TQ, hd), 1)
    qtail = jnp.where(lane_t == 0, NEG, 0.0)
    qs = [qblk[:, r * hd:(r + 1) * hd] for r in range(NSA_GROUP)]
    q128 = jnp.concatenate([jnp.concatenate([qs[r], qtail], axis=1) for r in range(NSA_GROUP)],
                           axis=0).astype(BF16)

    def stack(ref, lo, hi):
        return jnp.concatenate([ref[r, :, lo:hi] for r in range(NSA_GROUP)], axis=0)

    s_far = _dot_nt(q128, kcp[CMP_PAD:CMP_PAD + ncp, :])
    c_idx = lax.broadcasted_iota(jnp.int32, (1, ncp), 1)
    far_ok = c_idx < CMP_PER_TQ * i - CMP_PAD
    near0 = pl.multiple_of(CMP_PER_TQ * i, CMP_PER_TQ)
    s_near = _dot_nt(q128, kcp[pl.ds(near0, CMP_NEAR), :]) + stack(scmp_ref, 0, CMP_NEAR)
    near_ok = s_near > 0.5 * NEG
    s_far = jnp.where(far_ok, s_far, NEG)
    m_c = jnp.maximum(jnp.max(s_far, axis=1, keepdims=True), jnp.max(s_near, axis=1, keepdims=True))
    p_far = jnp.where(far_ok, jnp.exp2(s_far - m_c), 0.0)
    p_near = jnp.where(near_ok, jnp.exp2(s_near - m_c), 0.0)
    den = jnp.maximum(jnp.sum(p_far, axis=1, keepdims=True) + jnp.sum(p_near, axis=1, keepdims=True), 1e-30)
    pc_far = p_far / den
    pc_near = p_near / den
    o_cmp = (_dot(pc_far.astype(BF16), vcp[CMP_PAD:CMP_PAD + ncp, :])
             + _dot(pc_near.astype(BF16), vcp[pl.ds(near0, CMP_NEAR), :]))[:, :hd]

    pcs_far = sum(pc_far[r * TQ:(r + 1) * TQ] for r in range(NSA_GROUP))
    pcs_near = sum(pc_near[r * TQ:(r + 1) * TQ] for r in range(NSA_GROUP))

    def hi_lo_dot(p, w):
        hi = p.astype(BF16)
        lo = (p - hi.astype(F32)).astype(BF16)
        return _dot(hi, w) + _dot(lo, w)

    imp = (hi_lo_dot(pcs_far, ov_ref[CMP_PAD:CMP_PAD + ncp, :])
           + hi_lo_dot(pcs_near, ov_ref[pl.ds(near0, CMP_NEAR), :]))
    imp_t = imp.T
    n_idx = lax.broadcasted_iota(jnp.int32, (LANES, TQ), 0)
    t_idx = TQ * i + lax.broadcasted_iota(jnp.int32, (LANES, TQ), 1)
    cur = t_idx // SEL_BLOCK
    forced = (n_idx == 0) | (n_idx == cur) | (n_idx == cur - 1)
    score = jnp.where(n_idx <= cur, imp_t + jnp.where(forced, FORCE_BONUS, 0.0), -1.0)
    sel_t = _topk_rows(score, n_idx, SEL_TOPN, -3.0e38)
    maskneg = jnp.where(sel_t > 0.5, 0.0, NEG).T
    qaug = jnp.concatenate(
        [jnp.concatenate([maskneg, qs[r], qtail], axis=1) for r in range(NSA_GROUP)],
        axis=0).astype(BF16)

    n_full = jnp.maximum(i - 2, 0) // GROUP
    win0 = pl.multiple_of(TQ * i, TK)
    tail = _tail_steps(win0, i - 5, GROUP * n_full, stack(ssel_ref, 0, 3 * TK), True)
    acc = _flash_chain(qaug, kaug, vaug, s_sc, p_sc, BAND_PAD, n_full, tail)
    o_sel = acc[:, :hd] / acc[:, hd:hd + 1]

    band0 = win0 + 3 * TK
    tail = [(band0, 2 * TK, stack(swin_ref, 0, 2 * TK)), (band0 + 2 * TK, TK, stack(swin_ref, 2 * TK, 3 * TK))]
    acc = _flash_tail((jnp.full((rows, 1), NEG, F32), jnp.zeros((rows, LANES), F32)), q128, kwp, vwp, tail)
    o_win = acc[:, :hd] / acc[:, hd:hd + 1]

    sig = jax.nn.sigmoid(gate_ref[...])
    outs = []
    for r in range(NSA_GROUP):
        def gcol(branch):
            c0 = branch * NSA_HEADS + r
            c1 = c0 + NSA_GROUP
            return jnp.where(g == 0, sig[:, c0:c0 + 1], sig[:, c1:c1 + 1])
        sl = slice(r * TQ, (r + 1) * TQ)
        outs.append(gcol(0) * o_cmp[sl] + gcol(1) * o_sel[sl] + gcol(2) * o_win[sl])
    o_ref[...] = jnp.concatenate(outs, axis=1).astype(o_ref.dtype)


def _nsa(main, aux, ckv, ssel, swin, scmp, ov, batch, seq):
    nq = seq // TQ
    ncp = seq // CMP_STRIDE
    gw = NSA_GROUP * HEAD_DIM
    kv_spec = lambda colblk: pl.BlockSpec((seq, LANES), lambda b, g, i: (b, colblk))
    strip_spec = lambda cols: pl.BlockSpec((NSA_GROUP, TQ, cols), lambda b, g, i: (g, 0, 0))
    return pl.pallas_call(
        functools.partial(_nsa_kernel, seq=seq),
        out_shape=jax.ShapeDtypeStruct((batch * seq, NSA_HEADS * HEAD_DIM), BF16),
        grid=(batch, NSA_KV_HEADS, nq),
        in_specs=[pl.BlockSpec((TQ, gw), lambda b, g, i: (b * nq + i, g)),
                  kv_spec(4), kv_spec(5), kv_spec(6), kv_spec(7),
                  pl.BlockSpec((1, 1, ncp, HEAD_DIM), lambda b, g, i: (0, b * NSA_KV_HEADS + g, 0, 0)),
                  pl.BlockSpec((1, 1, ncp, HEAD_DIM), lambda b, g, i: (1, b * NSA_KV_HEADS + g, 0, 0)),
                  pl.BlockSpec((TQ, LANES), lambda b, g, i: (b * nq + i, 0)),
                  strip_spec(3 * TK), strip_spec(3 * TK), strip_spec(CMP_NEAR),
                  pl.BlockSpec(ov.shape, lambda b, g, i: (0, 0))],
        out_specs=pl.BlockSpec((TQ, gw), lambda b, g, i: (b * nq + i, g)),
        scratch_shapes=[pltpu.VMEM((BAND_PAD + seq, 2 * LANES), BF16),
                        pltpu.VMEM((BAND_PAD + seq, LANES), BF16),
                        pltpu.VMEM((BAND_PAD + seq, LANES), BF16),
                        pltpu.VMEM((BAND_PAD + seq, LANES), BF16),
                        pltpu.VMEM((ov.shape[0], LANES), BF16),
                        pltpu.VMEM((ov.shape[0], LANES), BF16),
                        pltpu.VMEM((NSA_GROUP * TQ, SUB), F32), pltpu.VMEM((NSA_GROUP * TQ, SUB), BF16)],
        compiler_params=_attention_params(),
        name="nsa_attention",
    )(main, main, main, main, main, ckv, ckv, aux, ssel, swin, scmp, ov)


def _ones_column():
    return jnp.where(lax.broadcasted_iota(jnp.int32, (TK, LANES), 1) == 0, 1.0, 0.0).astype(BF16)


def _causal_flash(q, k_sc, v_sc, s_sc, p_sc, strip, i):
    n_full = jnp.maximum(2 * i - 1, 0) // GROUP
    win0 = pl.multiple_of(i * TQD, SUB)
    tail = _tail_steps(win0, 2 * i - 4, GROUP * n_full, strip, i >= 1)
    return _flash_chain(q, k_sc, v_sc, s_sc, p_sc, FRONT, n_full, tail)


def _diff_kernel(q_ref, k_ref, v_ref, lam_ref, sub_ref, strip_ref, o_ref, k_sc, v_sc, s_sc, p_sc, *,
                 lam_init, seq):
    i = pl.program_id(2)
    hd = HEAD_DIM

    @pl.when(i == 0)
    def _():
        k_sc[0:FRONT, :] = jnp.zeros((FRONT, k_sc.shape[1]), BF16)
        v_sc[0:FRONT, :] = jnp.zeros((FRONT, v_sc.shape[1]), BF16)
        ones = _ones_column()

        def fill(c, _):
            r0 = pl.multiple_of(c * TK, TK)
            k_sc[pl.ds(FRONT + r0, TK), :] = k_ref[pl.ds(r0, TK), :]
            v_sc[pl.ds(FRONT + r0, TK), 0:LANES] = v_ref[pl.ds(r0, TK), :]
            v_sc[pl.ds(FRONT + r0, TK), LANES:2 * LANES] = ones
            return 0

        lax.fori_loop(0, seq // TK, fill, 0)

    q = q_ref[...].astype(F32) * QSCALE
    lane = lax.broadcasted_iota(jnp.int32, (TQD, 2 * hd), 1)
    qst = jnp.concatenate([jnp.where(lane < hd, q, 0.0), jnp.where(lane < hd, 0.0, q)],
                          axis=0).astype(BF16)
    strip = strip_ref[0]
    acc = _causal_flash(qst, k_sc, v_sc, s_sc, p_sc, jnp.concatenate([strip, strip], axis=0), i)
    o = acc[:, :2 * hd] / acc[:, 2 * hd:2 * hd + 1]
    lam_p = lam_ref[...]
    lam = (jnp.exp(jnp.sum(lam_p[0:1] * lam_p[1:2], axis=1, keepdims=True))
           - jnp.exp(jnp.sum(lam_p[2:3] * lam_p[3:4], axis=1, keepdims=True)) + lam_init)
    d = o[:TQD] - lam * o[TQD:]
    o_ref[...] = (_rms(d, sub_ref[...]) * (1.0 - lam_init)).astype(o_ref.dtype)


def _diff(main, lam_p, subln, strips, batch, seq, lam_init):
    nq = seq // TQD
    blk = 2 * HEAD_DIM
    q0, k0, v0 = 1024 // blk, 1536 // blk, 2048 // blk
    return pl.pallas_call(
        functools.partial(_diff_kernel, lam_init=lam_init, seq=seq),
        out_shape=jax.ShapeDtypeStruct((batch * seq, DIFF_HEADS * blk), BF16),
        grid=(batch, DIFF_HEADS, nq),
        in_specs=[pl.BlockSpec((TQD, blk), lambda b, h, i: (b * nq + i, q0 + h)),
                  pl.BlockSpec((seq, blk), lambda b, h, i: (b, k0 + h)),
                  pl.BlockSpec((seq, blk), lambda b, h, i: (b, v0 + h)),
                  pl.BlockSpec((4, HEAD_DIM), lambda b, h, i: (0, 0)),
                  pl.BlockSpec((1, blk), lambda b, h, i: (0, 0)),
                  pl.BlockSpec((1, TQD, 3 * TK), lambda b, h, i: (NSA_HEADS + h, 0, 0))],
        out_specs=pl.BlockSpec((TQD, blk), lambda b, h, i: (b * nq + i, h)),
        scratch_shapes=[pltpu.VMEM((FRONT + seq, LANES), BF16), pltpu.VMEM((FRONT + seq, 2 * LANES), BF16),
                        pltpu.VMEM((2 * TQD, SUB), F32), pltpu.VMEM((2 * TQD, SUB), BF16)],
        compiler_params=_attention_params(),
        name="diff_attention",
    )(main, main, main, lam_p, subln.reshape(1, blk), strips)


def _moba_kernel(q_ref, k_ref, v_ref, strip_ref, o_ref, kaug, vaug, kmean, s_sc, p_sc, *, seq):
    i = pl.program_id(2)
    hd = HEAD_DIM
    nb = seq // MOBA_BLOCK

    @pl.when(i == 0)
    def _():
        kmean[...] = jnp.zeros(kmean.shape, F32)
        kaug[0:FRONT, :] = jnp.zeros((FRONT, kaug.shape[1]), BF16)
        vaug[0:FRONT, :] = jnp.zeros((FRONT, vaug.shape[1]), BF16)
        col = lax.broadcasted_iota(jnp.int32, (MOBA_BLOCK, LANES), 1)
        ones = _ones_column()

        def fill(n, _):
            r0 = pl.multiple_of(n * MOBA_BLOCK, MOBA_BLOCK)
            kb = k_ref[pl.ds(r0, MOBA_BLOCK), :]
            kaug[pl.ds(FRONT + r0, MOBA_BLOCK), 0:LANES] = jnp.where(col == n, 1.0, 0.0).astype(BF16)
            kaug[pl.ds(FRONT + r0, MOBA_BLOCK), LANES:2 * LANES] = kb
            vaug[pl.ds(FRONT + r0, MOBA_BLOCK), 0:LANES] = v_ref[pl.ds(r0, MOBA_BLOCK), :]
            vaug[pl.ds(FRONT + r0, MOBA_BLOCK), LANES:2 * LANES] = ones
            kmean[pl.ds(n, 1), :] = jnp.mean(kb.astype(F32), axis=0, keepdims=True)
            return 0

        lax.fori_loop(0, nb, fill, 0)

    qblk = q_ref[...]
    qf = qblk.astype(F32) * QSCALE
    lane = lax.broadcasted_iota(jnp.int32, (TQD, LANES), 1)
    n_idx = lax.broadcasted_iota(jnp.int32, (nb, TQD), 0)
    own = (i * TQD + lax.broadcasted_iota(jnp.int32, (nb, TQD), 1)) // MOBA_BLOCK
    past = n_idx < own
    km = kmean[0:nb, :]
    lane_k = lax.broadcasted_iota(jnp.int32, km.shape, 1)
    no_block = jnp.full((LANES - nb, TQD), NEG, F32)
    qrows = []
    for hh in range(2):
        mine = (lane < hd) if hh == 0 else (lane >= hd)
        mine_k = (lane_k < hd) if hh == 0 else (lane_k >= hd)
        gate_t = _dot_nt(jnp.where(mine_k, km, 0.0).astype(BF16), qblk)
        sel_t = _topk_rows(jnp.where(past, gate_t, NEG), n_idx, MOBA_TOPK, -3.0e38)
        allowed = ((sel_t > 0.5) & past) | (n_idx == own)
        maskneg = jnp.concatenate([jnp.where(allowed, 0.0, NEG), no_block], axis=0).T
        qrows.append(jnp.concatenate([maskneg, jnp.where(mine, qf, 0.0)], axis=1))
    qaug = jnp.concatenate(qrows, axis=0).astype(BF16)
    acc = _causal_flash(qaug, kaug, vaug, s_sc, p_sc, jnp.concatenate([strip_ref[0], strip_ref[1]], axis=0), i)
    o = acc[:, :LANES] / acc[:, LANES:LANES + 1]
    o_ref[...] = jnp.where(lane < hd, o[:TQD], o[TQD:]).astype(o_ref.dtype)


def _moba(main, strips, batch, seq):
    nq = seq // TQD
    pairs = MOBA_HEADS // 2
    return pl.pallas_call(
        functools.partial(_moba_kernel, seq=seq),
        out_shape=jax.ShapeDtypeStruct((batch * seq, MOBA_HEADS * HEAD_DIM), BF16),
        grid=(batch, pairs, nq),
        in_specs=[pl.BlockSpec((TQD, LANES), lambda b, h, i: (b * nq + i, h)),
                  pl.BlockSpec((seq, LANES), lambda b, h, i: (b, pairs + h)),
                  pl.BlockSpec((seq, LANES), lambda b, h, i: (b, 2 * pairs + h)),
                  pl.BlockSpec((2, TQD, 3 * TK), lambda b, h, i: (h, 0, 0))],
        out_specs=pl.BlockSpec((TQD, LANES), lambda b, h, i: (b * nq + i, h)),
        scratch_shapes=[pltpu.VMEM((FRONT + seq, 2 * LANES), BF16), pltpu.VMEM((FRONT + seq, 2 * LANES), BF16),
                        pltpu.VMEM((LANES, LANES), F32), pltpu.VMEM((2 * TQD, SUB), F32),
                        pltpu.VMEM((2 * TQD, SUB), BF16)],
        compiler_params=_attention_params(),
        name="moba_attention",
    )(main, main, main, strips)


def _strip_geometry():
    r = np.arange(TQ)[:, None]
    c3 = np.arange(3 * TK)[None, :]
    rel_a = np.arange(TQD)[:, None] + TK - c3
    rel_b = r + 2 * TK - c3
    cc = np.arange(CMP_NEAR)[None, :]
    rel_c = r + CMP_STRIDE * CMP_PAD - (CMP_BLOCK - 1) - CMP_STRIDE * cc
    return (rel_a, rel_a >= 0), (rel_b, rel_b >= 0), (rel_b, (rel_b >= 0) & (rel_b < WINDOW)), (rel_c, rel_c >= 0)


def _overlap_padded(seq):
    nc = (seq - CMP_BLOCK) // CMP_STRIDE + 1
    ncp = seq // CMP_STRIDE
    tok = np.arange(nc)[:, None] * CMP_STRIDE + np.arange(CMP_BLOCK)[None, :]
    ov = np.zeros((CMP_PAD + ncp + 2 * CMP_PER_TQ, LANES), np.float32)
    np.add.at(ov, (CMP_PAD + np.repeat(np.arange(nc), CMP_BLOCK), (tok // SEL_BLOCK).reshape(-1)),
              1.0 / CMP_BLOCK)
    return jnp.asarray(ov, dtype=BF16)


def _strips(bias_table):
    geo_a, geo_sel, geo_win, geo_cmp = _strip_geometry()
    return (_bias_strips(bias_table, geo_a[0], geo_a[1], 0, MOBA_HEADS, "bias_strip_diag"),
            _bias_strips(bias_table, geo_sel[0], geo_sel[1], 0, NSA_HEADS, "bias_strip_sel"),
            _bias_strips(bias_table, geo_win[0], geo_win[1], 0, NSA_HEADS, "bias_strip_win"),
            _bias_strips(bias_table, geo_cmp[0], geo_cmp[1], 0, NSA_HEADS, "bias_strip_cmp"))


def _even_heads(xf, batch, seq, bias_table, g_mix, w_in, pos_k, w1_k, w2_k, pos_v, w1_v, w2_v,
                lam_q1, lam_k1, lam_q2, lam_k2, subln, strips=None):
    d = xf.shape[1]
    strip_a, strip_sel, strip_win, strip_cmp = strips if strips is not None else _strips(bias_table)
    w_main = jnp.concatenate([w_in[:, 0:512], w_in[:, 768:1280], w_in[:, 1304:2840]], axis=1).astype(BF16)
    w_aux = jnp.concatenate([w_in[:, 512:768], w_in[:, 1280:1304],
                             jnp.zeros((d, 3 * LANES - 280), F32)], axis=1).astype(BF16)
    main0, kc0, kc1, vc0, vc1, gates = _in_proj_even(xf, g_mix, w_main, w_aux, 1024, w_main.shape[1] // 2)

    nch = seq // CMP_STRIDE
    chunks = lambda a: a.reshape(batch, nch, CMP_STRIDE * HEAD_DIM)
    pos = jnp.stack([pos_k, pos_v]).reshape(2, 2, CMP_STRIDE * HEAD_DIM)
    ckv = _compress(chunks(kc0), chunks(kc1), chunks(vc0), chunks(vc1), pos,
                    jnp.stack([w1_k, w1_v]).astype(BF16), jnp.stack([w2_k, w2_v]).astype(BF16))

    o_a = _nsa(main0, gates, ckv, strip_sel, strip_win, strip_cmp, _overlap_padded(seq), batch, seq)
    lam_p = jnp.stack([lam_q1, lam_k1, lam_q2, lam_k2])
    lam_init = 0.8 - 0.6 * math.exp(-0.3 * 0)
    o_b = _diff(main0, lam_p, subln, strip_a, batch, seq, lam_init)
    return o_a, o_b, ckv


def _odd_heads(xf, batch, seq, bias_table, g_mix, w_in, strips=None):
    strip_a = (strips if strips is not None else _strips(bias_table))[0]
    main1 = _norm_matmul(xf, g_mix, w_in.astype(BF16), BF16, 1024, w_in.shape[1] // 2, "in_proj_odd")
    return _moba(main1, strip_a, batch, seq)


def kernel(x, bias_table, norm_mix, norm_mlp, norm_final, mlp_w1, mlp_w2, ev_w_in, ev_w_out,
           ev_cmp_pos_k, ev_cmp_w1_k, ev_cmp_w2_k, ev_cmp_pos_v, ev_cmp_w1_v, ev_cmp_w2_v,
           ev_lam_q1, ev_lam_k1, ev_lam_q2, ev_lam_k2, ev_subln, od_w_in, od_w_out):
    batch, seq, d = x.shape
    assert d == D_MODEL and (batch * seq) % 1024 == 0 and seq % TQD == 0
    assert SEL_TOPN * SEL_BLOCK <= seq <= LANES * SEL_BLOCK and seq >= MOBA_TOPK * MOBA_BLOCK
    xf = x.reshape(batch * seq, d)
    strips = _strips(bias_table)

    o_a, o_b, _ = _even_heads(xf, batch, seq, bias_table, norm_mix[0], ev_w_in[0],
                              ev_cmp_pos_k[0], ev_cmp_w1_k[0], ev_cmp_w2_k[0],
                              ev_cmp_pos_v[0], ev_cmp_w1_v[0], ev_cmp_w2_v[0],
                              ev_lam_q1[0], ev_lam_k1[0], ev_lam_q2[0], ev_lam_k2[0], ev_subln[0], strips)
    wo = ev_w_out[0].astype(BF16)
    x1 = _post(xf, o_a, o_b, 0, wo[:512], wo[512:], norm_mlp[0], mlp_w1[0].astype(BF16),
               mlp_w2[0].astype(BF16), norm_final, False, 1024, 1024, "post_even")

    o_m = _odd_heads(x1, batch, seq, bias_table, norm_mix[1], od_w_in[0], strips)
    wo = od_w_out[0].astype(BF16)
    out = _post(x1, o_m, o_m, 1, wo[:512], wo[512:], norm_mlp[1], mlp_w1[1].astype(BF16),
                mlp_w2[1].astype(BF16), norm_final, True, 1024, 1024, "post_odd")
    return out.reshape(batch, seq, d)
```

```python
import functools
import math

import numpy as np
import jax
import jax.numpy as jnp
from jax import lax
from jax.experimental import pallas as pl
from jax.experimental.pallas import tpu as pltpu

F32 = jnp.float32
BF16 = jnp.bfloat16

D_MODEL = 1024
HEAD_DIM = 64
EPS = 1e-6
NEG = -1e30
NUM_BUCKETS = 32
MAX_DISTANCE = 128
NSA_HEADS = 8
NSA_KV_HEADS = 2
NSA_GROUP = 4
CMP_BLOCK = 32
CMP_STRIDE = 16
SEL_BLOCK = 64
SEL_TOPN = 16
WINDOW = 512
FORCE_BONUS = 1e3
DIFF_HEADS = 4
MOBA_HEADS = 16
MOBA_BLOCK = 256
MOBA_TOPK = 3
D_FF = 4 * D_MODEL
SCALE = HEAD_DIM ** -0.5

LANES = 128
TQ = 256
TQD = 512
TK = 256
FAR_REL = 113
CMP_PER_TQ = TQ // CMP_STRIDE
CMP_NEAR = 128
CMP_PAD = CMP_NEAR - 2 * CMP_PER_TQ
GROUP = 4
SUB = 2 * TK
BAND_PAD = (GROUP + 1) * TK
FRONT = GROUP * TK
LOG2E = math.log2(math.e)
QSCALE = SCALE * LOG2E
VMEM_LIMIT = 56 * 1024 * 1024


def _attention_params():
    return pltpu.CompilerParams(dimension_semantics=("arbitrary", "arbitrary", "arbitrary"),
                                vmem_limit_bytes=VMEM_LIMIT)


def _dot_nt(a, b):
    return lax.dot_general(a, b, (((1,), (1,)), ((), ())), preferred_element_type=F32)


def _dot(a, b):
    return jnp.dot(a, b, preferred_element_type=F32)


def _rms(x, g):
    return x * lax.rsqrt(jnp.mean(x * x, axis=-1, keepdims=True) + EPS) * g


def _t5_bucket_np(rel):
    n = np.maximum(rel, 0)
    max_exact = NUM_BUCKETS // 2
    large = max_exact + (np.log(np.maximum(n, 1).astype(np.float32) / np.float32(max_exact))
                         / np.float32(math.log(MAX_DISTANCE / max_exact))
                         * np.float32(NUM_BUCKETS - max_exact)).astype(np.int32)
    large = np.minimum(large, NUM_BUCKETS - 1)
    return np.where(n < max_exact, n, large).astype(np.int32)


def _strip_kernel(tbl_ref, bucket_ref, valid_ref, o_ref, *, head0):
    h = pl.program_id(0) + head0
    bucket = bucket_ref[...]
    last = tbl_ref[NUM_BUCKETS - 1, h]
    acc = jnp.zeros(bucket.shape, F32)
    for b in range(NUM_BUCKETS - 1):
        acc = jnp.where(bucket == b, tbl_ref[b, h] - last, acc)
    o_ref[0] = jnp.where(valid_ref[...] != 0, acc * LOG2E, NEG)


def _bias_strips(table, rel, valid, head0, nheads, name):
    rows, cols = rel.shape
    bucket = jnp.asarray(_t5_bucket_np(rel))
    validi = jnp.asarray(valid.astype(np.int32))
    return pl.pallas_call(
        functools.partial(_strip_kernel, head0=head0),
        out_shape=jax.ShapeDtypeStruct((nheads, rows, cols), F32),
        grid=(nheads,),
        in_specs=[pl.BlockSpec(memory_space=pltpu.SMEM),
                  pl.BlockSpec((rows, cols), lambda h: (0, 0)),
                  pl.BlockSpec((rows, cols), lambda h: (0, 0))],
        out_specs=pl.BlockSpec((1, rows, cols), lambda h: (h, 0, 0)),
        name=name,
    )(table, bucket, validi)


def _norm_matmul_kernel(x_ref, g_ref, w_ref, o_ref, xn_ref):
    @pl.when(pl.program_id(1) == 0)
    def _():
        xn_ref[...] = _rms(x_ref[...], g_ref[...]).astype(BF16)

    o_ref[...] = _dot(xn_ref[...], w_ref[...]).astype(o_ref.dtype)


def _norm_matmul(x, g, w, out_dtype, tm, tn, name):
    m, d = x.shape
    n = w.shape[1]
    return pl.pallas_call(
        _norm_matmul_kernel,
        out_shape=jax.ShapeDtypeStruct((m, n), out_dtype),
        grid=(m // tm, n // tn),
        in_specs=[pl.BlockSpec((tm, d), lambda i, j: (i, 0)),
                  pl.BlockSpec((1, d), lambda i, j: (0, 0)),
                  pl.BlockSpec((d, tn), lambda i, j: (0, j))],
        out_specs=pl.BlockSpec((tm, tn), lambda i, j: (i, j)),
        scratch_shapes=[pltpu.VMEM((tm, d), BF16)],
        compiler_params=pltpu.CompilerParams(
            dimension_semantics=("arbitrary", "arbitrary"), vmem_limit_bytes=VMEM_LIMIT),
        name=name,
    )(x, g.reshape(1, d), w)


def _in_proj_even_kernel(x_ref, g_ref, w_ref, wa_ref, o_ref, kc0_ref, kc1_ref, vc0_ref, vc1_ref, gate_ref,
                         xn_ref):
    hd = HEAD_DIM

    @pl.when(pl.program_id(1) == 0)
    def _():
        xn = _rms(x_ref[...], g_ref[...]).astype(BF16)
        xn_ref[...] = xn
        aux = _dot(xn, wa_ref[...])
        kc0_ref[...] = aux[:, 0:hd]
        kc1_ref[...] = aux[:, hd:2 * hd]
        vc0_ref[...] = aux[:, 2 * hd:3 * hd]
        vc1_ref[...] = aux[:, 3 * hd:4 * hd]
        gate_ref[...] = aux[:, 4 * hd:]

    o_ref[...] = _dot(xn_ref[...], w_ref[...]).astype(o_ref.dtype)


def _in_proj_even(x, g, w_main, w_aux, tm, tn):
    m, d = x.shape
    n = w_main.shape[1]
    na = w_aux.shape[1]
    row_spec = lambda width: pl.BlockSpec((tm, width), lambda i, j: (i, 0))
    cmp_shape = jax.ShapeDtypeStruct((m, HEAD_DIM), F32)
    return pl.pallas_call(
        _in_proj_even_kernel,
        out_shape=(jax.ShapeDtypeStruct((m, n), BF16), cmp_shape, cmp_shape, cmp_shape, cmp_shape,
                   jax.ShapeDtypeStruct((m, na - 4 * HEAD_DIM), F32)),
        grid=(m // tm, n // tn),
        in_specs=[row_spec(d),
                  pl.BlockSpec((1, d), lambda i, j: (0, 0)),
                  pl.BlockSpec((d, tn), lambda i, j: (0, j)),
                  pl.BlockSpec((d, na), lambda i, j: (0, 0))],
        out_specs=(pl.BlockSpec((tm, tn), lambda i, j: (i, j)), row_spec(HEAD_DIM), row_spec(HEAD_DIM),
                   row_spec(HEAD_DIM), row_spec(HEAD_DIM), row_spec(na - 4 * HEAD_DIM)),
        scratch_shapes=[pltpu.VMEM((tm, d), BF16)],
        compiler_params=pltpu.CompilerParams(
            dimension_semantics=("arbitrary", "arbitrary"), vmem_limit_bytes=VMEM_LIMIT),
        name="in_proj_even",
    )(x, g.reshape(1, d), w_main, w_aux)


def _post_kernel(x_ref, a0_ref, a1_ref, wo0_ref, wo1_ref, g_ref, w1_ref, w2_ref, gf_ref,
                 o_ref, acc_ref, xn_ref, *, final_norm):
    f = pl.program_id(1)

    @pl.when(f == 0)
    def _():
        x1 = x_ref[...] + _dot(a0_ref[...], wo0_ref[...]) + _dot(a1_ref[...], wo1_ref[...])
        acc_ref[...] = x1
        xn_ref[...] = _rms(x1, g_ref[...]).astype(BF16)

    h = jnp.square(jnp.maximum(_dot(xn_ref[...], w1_ref[...]), 0.0))
    acc_ref[...] += _dot(h.astype(BF16), w2_ref[...])

    @pl.when(f == pl.num_programs(1) - 1)
    def _():
        y = acc_ref[...]
        if final_norm:
            y = _rms(y, gf_ref[...])
        o_ref[...] = y


def _post(x, a0, a1, a1_colblk, wo0, wo1, g, w1, w2, gf, final_norm, tm, tf, name):
    m, d = x.shape
    k0 = wo0.shape[0]
    k1 = wo1.shape[0]
    ff = w1.shape[1]
    return pl.pallas_call(
        functools.partial(_post_kernel, final_norm=final_norm),
        out_shape=jax.ShapeDtypeStruct((m, d), F32),
        grid=(m // tm, ff // tf),
        in_specs=[pl.BlockSpec((tm, d), lambda i, f: (i, 0)),
                  pl.BlockSpec((tm, k0), lambda i, f: (i, 0)),
                  pl.BlockSpec((tm, k1), lambda i, f: (i, a1_colblk)),
                  pl.BlockSpec((k0, d), lambda i, f: (0, 0)),
                  pl.BlockSpec((k1, d), lambda i, f: (0, 0)),
                  pl.BlockSpec((1, d), lambda i, f: (0, 0)),
                  pl.BlockSpec((d, tf), lambda i, f: (0, f)),
                  pl.BlockSpec((tf, d), lambda i, f: (f, 0)),
                  pl.BlockSpec((1, d), lambda i, f: (0, 0))],
        out_specs=pl.BlockSpec((tm, d), lambda i, f: (i, 0)),
        scratch_shapes=[pltpu.VMEM((tm, d), F32), pltpu.VMEM((tm, d), BF16)],
        compiler_params=pltpu.CompilerParams(
            dimension_semantics=("arbitrary", "arbitrary"), vmem_limit_bytes=VMEM_LIMIT),
        name=name,
    )(x, a0, a1, wo0, wo1, g.reshape(1, d), w1, w2, gf.reshape(1, d))


def _compress_kernel(kc0_ref, kc1_ref, vc0_ref, vc1_ref, pos_ref, w1_ref, w2_ref, o_ref):
    half = CMP_STRIDE * HEAD_DIM
    for kv, refs in enumerate(((kc0_ref, kc1_ref), (vc0_ref, vc1_ref))):
        w1a = w1_ref[kv, :half, :]
        w1b = w1_ref[kv, half:, :]
        pos = pos_ref[kv].astype(BF16)
        pa = jnp.broadcast_to(pos[0:1], (8, half))
        pb = jnp.broadcast_to(pos[1:2], (8, half))
        posterm = (_dot(pa, w1a) + _dot(pb, w1b))[0:1]
        for grp, x_ref in enumerate(refs):
            x = x_ref[0].astype(BF16)
            first = _dot(x, w1a)
            second = _dot(x, w1b)
            second = pltpu.roll(second, second.shape[0] - 1, 0)
            hid = jax.nn.gelu(first + second + posterm)
            o_ref[kv, grp] = _dot(hid.astype(BF16), w2_ref[kv])


def _compress(kc0, kc1, vc0, vc1, pos, w1, w2):
    batch, nch, wide = kc0.shape
    x_spec = pl.BlockSpec((1, nch, wide), lambda b: (b, 0, 0))
    full = lambda a: pl.BlockSpec(a.shape, lambda b: (0,) * a.ndim)
    return pl.pallas_call(
        _compress_kernel,
        out_shape=jax.ShapeDtypeStruct((2, batch * NSA_KV_HEADS, nch, HEAD_DIM), F32),
        grid=(batch,),
        in_specs=[x_spec, x_spec, x_spec, x_spec, full(pos), full(w1), full(w2)],
        out_specs=pl.BlockSpec((2, NSA_KV_HEADS, nch, HEAD_DIM), lambda b: (0, b, 0, 0)),
        compiler_params=pltpu.CompilerParams(
            dimension_semantics=("arbitrary",), vmem_limit_bytes=VMEM_LIMIT),
        name="nsa_compress",
    )(kc0, kc1, vc0, vc1, pos, w1, w2)


def _softmax_part(m, s):
    m_new = jnp.maximum(m, jnp.max(s, axis=1, keepdims=True))
    return m_new, jnp.exp2(m - m_new), jnp.exp2(s - m_new).astype(BF16)


def _flash_chain(q, k_sc, v_sc, bufs, far0, n_full, tail):
    s_sc, p_sc, acc_sc = bufs
    rows = q.shape[0]
    assert GROUP * TK == 2 * SUB and tail[0][1] == SUB
    tail0 = tail[0][0]

    def qk(r0):
        return _dot_nt(q, k_sc[pl.ds(pl.multiple_of(r0, TK), SUB), :])

    s_sc[...] = qk(jnp.where(n_full > 0, far0, tail0))
    p_sc[...] = jnp.zeros(p_sc.shape, BF16)
    acc_sc[...] = jnp.zeros(acc_sc.shape, F32)

    def far_body(j, m):
        base = pl.multiple_of(far0 + j * (GROUP * TK), SUB)
        acc = acc_sc[...] + _dot(p_sc[...], v_sc[pl.ds(base - SUB, SUB), :])
        m, alpha, p = _softmax_part(m, s_sc[...])
        s_next = qk(base + SUB)
        acc = alpha * acc + _dot(p, v_sc[pl.ds(base, SUB), :])
        m, alpha, p = _softmax_part(m, s_next)
        s_sc[...] = qk(jnp.where(j + 1 < n_full, base + GROUP * TK, tail0))
        p_sc[...] = p
        acc_sc[...] = alpha * acc
        return m

    m = lax.fori_loop(0, n_full, far_body, jnp.full((rows, 1), NEG, F32))
    base = pl.multiple_of(far0 + n_full * (GROUP * TK), SUB)
    acc = acc_sc[...] + _dot(p_sc[...], v_sc[pl.ds(base - SUB, SUB), :])
    m, alpha, p = _softmax_part(m, s_sc[...] + tail[0][2])
    acc = alpha * acc + _dot(p, v_sc[pl.ds(tail0, SUB), :])
    return _flash_tail((m, acc), q, k_sc, v_sc, tail[1:])


def _flash_buffers(rows, ncols):
    return [pltpu.VMEM((rows, SUB), F32), pltpu.VMEM((rows, SUB), BF16), pltpu.VMEM((rows, ncols), F32)]


def _flash_tail(carry, q, k_sc, v_sc, tail):
    m, acc = carry
    for r0, width, extra in tail:
        m, alpha, p = _softmax_part(m, _dot_nt(q, k_sc[pl.ds(r0, width), :]) + extra)
        acc = alpha * acc + _dot(p, v_sc[pl.ds(r0, width), :])
    return acc


def _tile_gate(cond):
    return jnp.where(cond, 0.0, NEG)


def _gate_pair(cond_a, cond_b):
    lane = lax.broadcasted_iota(jnp.int32, (1, 2 * TK), 1)
    return jnp.where(lane < TK, _tile_gate(cond_a), _tile_gate(cond_b))


def _tail_steps(win0, t0, done, strip, near_ok):
    rows = strip.shape[0]
    mid = jnp.concatenate([jnp.broadcast_to(_tile_gate(t0 + 2 >= done), (rows, TK)),
                           strip[:, :TK] + _tile_gate(near_ok)], axis=1)
    return [(win0, SUB, _gate_pair(t0 >= done, t0 + 1 >= done)),
            (win0 + SUB, SUB, mid),
            (win0 + 2 * SUB, SUB, strip[:, TK:])]


def _topk_rows(score, idx, k, floor):
    sel = jnp.zeros(score.shape, F32)
    nrow = score.shape[0]
    for _ in range(k):
        mx = jnp.max(score, axis=0, keepdims=True)
        first = jnp.min(jnp.where(score == mx, idx, nrow), axis=0, keepdims=True)
        pick = idx == first
        sel = jnp.where(pick, 1.0, sel)
        score = jnp.where(pick, floor, score)
    return sel


def _nsa_kernel(q_ref, ks_ref, vs_ref, kw_ref, vw_ref, kc_ref, vc_ref, gate_ref,
                ssel_ref, swin_ref, scmp_ref, ov_ref, o_ref,
                kaug, vaug, kwp, vwp, kcp, vcp, *bufs, seq):
    g = pl.program_id(1)
    i = pl.program_id(2)
    ncp = seq // CMP_STRIDE
    hd = HEAD_DIM
    rows = NSA_GROUP * TQ

    @pl.when(i == 0)
    def _():
        lane = lax.broadcasted_iota(jnp.int32, (BAND_PAD, LANES), 1)
        padflag = jnp.where(lane == hd, 1.0, 0.0).astype(BF16)
        kaug[0:BAND_PAD, 0:LANES] = jnp.zeros((BAND_PAD, LANES), BF16)
        kaug[0:BAND_PAD, LANES:2 * LANES] = padflag
        kwp[0:BAND_PAD, :] = padflag
        vaug[0:BAND_PAD, :] = jnp.zeros((BAND_PAD, LANES), BF16)
        vwp[0:BAND_PAD, :] = jnp.zeros((BAND_PAD, LANES), BF16)
        onescol = jnp.where(lax.broadcasted_iota(jnp.int32, (TK, LANES), 1) == hd, 1.0, 0.0)
        zero_hd = jnp.zeros((TK, hd), F32)
        col = lax.broadcasted_iota(jnp.int32, (TK, LANES), 1)
        row = lax.broadcasted_iota(jnp.int32, (TK, LANES), 0)

        def fill(c, _):
            src = pl.multiple_of(c * TK, TK)
            dst = pl.multiple_of(BAND_PAD + c * TK, TK)

            def pick(ref):
                both = ref[pl.ds(src, TK), :].astype(F32)
                return jnp.concatenate([jnp.where(g == 0, both[:, :hd], both[:, hd:]), zero_hd], axis=1)

            kaug[pl.ds(dst, TK), 0:LANES] = jnp.where((src + row) // SEL_BLOCK == col, 1.0, 0.0).astype(BF16)
            kaug[pl.ds(dst, TK), LANES:2 * LANES] = pick(ks_ref).astype(BF16)
            kwp[pl.ds(dst, TK), :] = pick(kw_ref).astype(BF16)
            vaug[pl.ds(dst, TK), :] = (pick(vs_ref) + onescol).astype(BF16)
            vwp[pl.ds(dst, TK), :] = (pick(vw_ref) + onescol).astype(BF16)
            return 0

        lax.fori_loop(0, seq // TK, fill, 0)
        lane_c = lax.broadcasted_iota(jnp.int32, (CMP_PAD, LANES), 1)
        kcp[0:CMP_PAD, :] = jnp.where(lane_c == hd, 1.0, 0.0).astype(BF16)
        vcp[0:CMP_PAD, :] = jnp.zeros((CMP_PAD, LANES), BF16)
        zc = jnp.zeros((ncp, hd), F32)
        onesc = jnp.where(lax.broadcasted_iota(jnp.int32, (ncp, LANES), 1) == hd, 1.0, 0.0)
        kcp[CMP_PAD:CMP_PAD + ncp, :] = jnp.concatenate([kc_ref[0, 0], zc], axis=1).astype(BF16)
        vcp[CMP_PAD:CMP_PAD + ncp, :] = (jnp.concatenate([vc_ref[0, 0], zc], axis=1) + onesc).astype(BF16)
        tail = kcp.shape[0] - CMP_PAD - ncp
        kcp[CMP_PAD + ncp:, :] = jnp.zeros((tail, LANES), BF16)
        vcp[CMP_PAD + ncp:, :] = jnp.zeros((tail, LANES), BF16)

    qblk = q_ref[...].astype(F32) * QSCALE
    lane_t = lax.broadcasted_iota(jnp.int32, (TQ, hd), 1)
    qtail = jnp.where(lane_t == 0, NEG, 0.0)
    qs = [qblk[:, r * hd:(r + 1) * hd] for r in range(NSA_GROUP)]
    q128 = jnp.concatenate([jnp.concatenate([qs[r], qtail], axis=1) for r in range(NSA_GROUP)],
                           axis=0).astype(BF16)

    def stack(ref, lo, hi):
        return jnp.concatenate([ref[r, :, lo:hi] for r in range(NSA_GROUP)], axis=0)

    s_far = _dot_nt(q128, kcp[CMP_PAD:CMP_PAD + ncp, :])
    c_idx = lax.broadcasted_iota(jnp.int32, (1, ncp), 1)
    far_ok = c_idx < CMP_PER_TQ * i - CMP_PAD
    near0 = pl.multiple_of(CMP_PER_TQ * i, CMP_PER_TQ)
    s_near = _dot_nt(q128, kcp[pl.ds(near0, CMP_NEAR), :]) + stack(scmp_ref, 0, CMP_NEAR)
    near_ok = s_near > 0.5 * NEG
    s_far = jnp.where(far_ok, s_far, NEG)
    m_c = jnp.maximum(jnp.max(s_far, axis=1, keepdims=True), jnp.max(s_near, axis=1, keepdims=True))
    p_far = jnp.where(far_ok, jnp.exp2(s_far - m_c), 0.0)
    p_near = jnp.where(near_ok, jnp.exp2(s_near - m_c), 0.0)
    den = jnp.maximum(jnp.sum(p_far, axis=1, keepdims=True) + jnp.sum(p_near, axis=1, keepdims=True), 1e-30)
    pc_far = p_far / den
    pc_near = p_near / den
    o_cmp = (_dot(pc_far.astype(BF16), vcp[CMP_PAD:CMP_PAD + ncp, :])
             + _dot(pc_near.astype(BF16), vcp[pl.ds(near0, CMP_NEAR), :]))[:, :hd]

    win0 = pl.multiple_of(TQ * i, TK)
    band0 = win0 + 3 * TK
    tail = [(band0, 2 * TK, stack(swin_ref, 0, 2 * TK)), (band0 + 2 * TK, TK, stack(swin_ref, 2 * TK, 3 * TK))]
    acc = _flash_tail((jnp.full((rows, 1), NEG, F32), jnp.zeros((rows, LANES), F32)), q128, kwp, vwp, tail)
    o_win = acc[:, :hd] / acc[:, hd:hd + 1]

    pcs_far = sum(pc_far[r * TQ:(r + 1) * TQ] for r in range(NSA_GROUP))
    pcs_near = sum(pc_near[r * TQ:(r + 1) * TQ] for r in range(NSA_GROUP))

    def hi_lo_dot(p, w):
        hi = p.astype(BF16)
        lo = (p - hi.astype(F32)).astype(BF16)
        return _dot(hi, w) + _dot(lo, w)

    imp = (hi_lo_dot(pcs_far, ov_ref[CMP_PAD:CMP_PAD + ncp, :])
           + hi_lo_dot(pcs_near, ov_ref[pl.ds(near0, CMP_NEAR), :]))
    imp_t = imp.T
    n_idx = lax.broadcasted_iota(jnp.int32, (LANES, TQ), 0)
    t_idx = TQ * i + lax.broadcasted_iota(jnp.int32, (LANES, TQ), 1)
    cur = t_idx // SEL_BLOCK
    eligible = n_idx <= cur
    forced = ((n_idx == 0) | (n_idx == cur) | (n_idx == cur - 1)) & eligible
    rest = jnp.where(forced, -3.0e38, jnp.where(eligible, imp_t, -1.0))
    sel_t = _topk_rows(rest, n_idx, SEL_TOPN - 3, -3.0e38)
    maskneg = jnp.where(forced | (sel_t > 0.5), 0.0, NEG).T
    qaug = jnp.concatenate(
        [jnp.concatenate([maskneg, qs[r], qtail], axis=1) for r in range(NSA_GROUP)],
        axis=0).astype(BF16)

    n_full = jnp.maximum(i - 2, 0) // GROUP
    tail = _tail_steps(win0, i - 5, GROUP * n_full, stack(ssel_ref, 0, 3 * TK), True)
    acc = _flash_chain(qaug, kaug, vaug, bufs, BAND_PAD, n_full, tail)
    o_sel = acc[:, :hd] / acc[:, hd:hd + 1]

    sig = jax.nn.sigmoid(gate_ref[...])
    outs = []
    for r in range(NSA_GROUP):
        def gcol(branch):
            c0 = branch * NSA_HEADS + r
            c1 = c0 + NSA_GROUP
            return jnp.where(g == 0, sig[:, c0:c0 + 1], sig[:, c1:c1 + 1])
        sl = slice(r * TQ, (r + 1) * TQ)
        outs.append(gcol(0) * o_cmp[sl] + gcol(1) * o_sel[sl] + gcol(2) * o_win[sl])
    o_ref[...] = jnp.concatenate(outs, axis=1).astype(o_ref.dtype)


def _nsa(main, aux, ckv, ssel, swin, scmp, ov, batch, seq):
    nq = seq // TQ
    ncp = seq // CMP_STRIDE
    gw = NSA_GROUP * HEAD_DIM
    once = pl.Buffered(1)
    kv_spec = lambda colblk: pl.BlockSpec((seq, LANES), lambda b, g, i: (b, colblk), pipeline_mode=once)
    strip_spec = lambda cols: pl.BlockSpec((NSA_GROUP, TQ, cols), lambda b, g, i: (g, 0, 0), pipeline_mode=once)
    return pl.pallas_call(
        functools.partial(_nsa_kernel, seq=seq),
        out_shape=jax.ShapeDtypeStruct((batch * seq, NSA_HEADS * HEAD_DIM), BF16),
        grid=(batch, NSA_KV_HEADS, nq),
        in_specs=[pl.BlockSpec((TQ, gw), lambda b, g, i: (b * nq + i, g)),
                  kv_spec(4), kv_spec(5), kv_spec(6), kv_spec(7),
                  pl.BlockSpec((1, 1, ncp, HEAD_DIM), lambda b, g, i: (0, b * NSA_KV_HEADS + g, 0, 0)),
                  pl.BlockSpec((1, 1, ncp, HEAD_DIM), lambda b, g, i: (1, b * NSA_KV_HEADS + g, 0, 0)),
                  pl.BlockSpec((TQ, LANES), lambda b, g, i: (b * nq + i, 0)),
                  strip_spec(3 * TK), strip_spec(3 * TK), strip_spec(CMP_NEAR),
                  pl.BlockSpec(ov.shape, lambda b, g, i: (0, 0))],
        out_specs=pl.BlockSpec((TQ, gw), lambda b, g, i: (b * nq + i, g)),
        scratch_shapes=[pltpu.VMEM((BAND_PAD + seq, 2 * LANES), BF16),
                        pltpu.VMEM((BAND_PAD + seq, LANES), BF16),
                        pltpu.VMEM((BAND_PAD + seq, LANES), BF16),
                        pltpu.VMEM((BAND_PAD + seq, LANES), BF16),
                        pltpu.VMEM((ov.shape[0], LANES), BF16),
                        pltpu.VMEM((ov.shape[0], LANES), BF16),
                        *_flash_buffers(NSA_GROUP * TQ, LANES)],
        compiler_params=_attention_params(),
        name="nsa_attention",
    )(main, main, main, main, main, ckv, ckv, aux, ssel, swin, scmp, ov)


def _ones_column():
    return jnp.where(lax.broadcasted_iota(jnp.int32, (TK, LANES), 1) == 0, 1.0, 0.0).astype(BF16)


def _causal_flash(q, k_sc, v_sc, bufs, strip, i):
    n_full = jnp.maximum(2 * i - 1, 0) // GROUP
    win0 = pl.multiple_of(i * TQD, SUB)
    tail = _tail_steps(win0, 2 * i - 4, GROUP * n_full, strip, i >= 1)
    return _flash_chain(q, k_sc, v_sc, bufs, FRONT, n_full, tail)


def _diff_kernel(q_ref, k_ref, v_ref, lam_ref, sub_ref, strip_ref, o_ref, k_sc, v_sc, *bufs, lam_init, seq):
    i = pl.program_id(2)
    hd = HEAD_DIM

    @pl.when(i == 0)
    def _():
        k_sc[0:FRONT, :] = jnp.zeros((FRONT, k_sc.shape[1]), BF16)
        v_sc[0:FRONT, :] = jnp.zeros((FRONT, v_sc.shape[1]), BF16)
        ones = _ones_column()

        def fill(c, _):
            r0 = pl.multiple_of(c * TK, TK)
            k_sc[pl.ds(FRONT + r0, TK), :] = k_ref[pl.ds(r0, TK), :]
            v_sc[pl.ds(FRONT + r0, TK), 0:LANES] = v_ref[pl.ds(r0, TK), :]
            v_sc[pl.ds(FRONT + r0, TK), LANES:2 * LANES] = ones
            return 0

        lax.fori_loop(0, seq // TK, fill, 0)

    q = q_ref[...].astype(F32) * QSCALE
    lane = lax.broadcasted_iota(jnp.int32, (TQD, 2 * hd), 1)
    qst = jnp.concatenate([jnp.where(lane < hd, q, 0.0), jnp.where(lane < hd, 0.0, q)],
                          axis=0).astype(BF16)
    strip = strip_ref[0]
    acc = _causal_flash(qst, k_sc, v_sc, bufs, jnp.concatenate([strip, strip], axis=0), i)
    o = acc[:, :2 * hd] / acc[:, 2 * hd:2 * hd + 1]
    lam_p = lam_ref[...]
    lam = (jnp.exp(jnp.sum(lam_p[0:1] * lam_p[1:2], axis=1, keepdims=True))
           - jnp.exp(jnp.sum(lam_p[2:3] * lam_p[3:4], axis=1, keepdims=True)) + lam_init)
    d = o[:TQD] - lam * o[TQD:]
    o_ref[...] = (_rms(d, sub_ref[...]) * (1.0 - lam_init)).astype(o_ref.dtype)


def _diff(main, lam_p, subln, strips, batch, seq, lam_init):
    nq = seq // TQD
    blk = 2 * HEAD_DIM
    q0, k0, v0 = 1024 // blk, 1536 // blk, 2048 // blk
    return pl.pallas_call(
        functools.partial(_diff_kernel, lam_init=lam_init, seq=seq),
        out_shape=jax.ShapeDtypeStruct((batch * seq, DIFF_HEADS * blk), BF16),
        grid=(batch, DIFF_HEADS, nq),
        in_specs=[pl.BlockSpec((TQD, blk), lambda b, h, i: (b * nq + i, q0 + h)),
                  pl.BlockSpec((seq, blk), lambda b, h, i: (b, k0 + h)),
                  pl.BlockSpec((seq, blk), lambda b, h, i: (b, v0 + h)),
                  pl.BlockSpec((4, HEAD_DIM), lambda b, h, i: (0, 0)),
                  pl.BlockSpec((1, blk), lambda b, h, i: (0, 0)),
                  pl.BlockSpec((1, TQD, 3 * TK), lambda b, h, i: (NSA_HEADS + h, 0, 0))],
        out_specs=pl.BlockSpec((TQD, blk), lambda b, h, i: (b * nq + i, h)),
        scratch_shapes=[pltpu.VMEM((FRONT + seq, LANES), BF16), pltpu.VMEM((FRONT + seq, 2 * LANES), BF16),
                        *_flash_buffers(2 * TQD, 2 * LANES)],
        compiler_params=_attention_params(),
        name="diff_attention",
    )(main, main, main, lam_p, subln.reshape(1, blk), strips)


def _moba_kernel(q_ref, k_ref, v_ref, strip_ref, o_ref, kaug, vaug, kmean, *bufs, seq):
    i = pl.program_id(2)
    hd = HEAD_DIM
    nb = seq // MOBA_BLOCK

    @pl.when(i == 0)
    def _():
        kmean[...] = jnp.zeros(kmean.shape, F32)
        kaug[0:FRONT, :] = jnp.zeros((FRONT, kaug.shape[1]), BF16)
        vaug[0:FRONT, :] = jnp.zeros((FRONT, vaug.shape[1]), BF16)
        col = lax.broadcasted_iota(jnp.int32, (MOBA_BLOCK, LANES), 1)
        ones = _ones_column()

        def fill(n, _):
            r0 = pl.multiple_of(n * MOBA_BLOCK, MOBA_BLOCK)
            kb = k_ref[pl.ds(r0, MOBA_BLOCK), :]
            kaug[pl.ds(FRONT + r0, MOBA_BLOCK), 0:LANES] = jnp.where(col == n, 1.0, 0.0).astype(BF16)
            kaug[pl.ds(FRONT + r0, MOBA_BLOCK), LANES:2 * LANES] = kb
            vaug[pl.ds(FRONT + r0, MOBA_BLOCK), 0:LANES] = v_ref[pl.ds(r0, MOBA_BLOCK), :]
            vaug[pl.ds(FRONT + r0, MOBA_BLOCK), LANES:2 * LANES] = ones
            kmean[pl.ds(n, 1), :] = jnp.mean(kb.astype(F32), axis=0, keepdims=True)
            return 0

        lax.fori_loop(0, nb, fill, 0)

    qblk = q_ref[...]
    qf = qblk.astype(F32) * QSCALE
    lane = lax.broadcasted_iota(jnp.int32, (TQD, LANES), 1)
    n_idx = lax.broadcasted_iota(jnp.int32, (nb, TQD), 0)
    own = (i * TQD + lax.broadcasted_iota(jnp.int32, (nb, TQD), 1)) // MOBA_BLOCK
    past = n_idx < own
    km = kmean[0:nb, :]
    lane_k = lax.broadcasted_iota(jnp.int32, km.shape, 1)
    no_block = jnp.full((LANES - nb, TQD), NEG, F32)
    qrows = []
    for hh in range(2):
        mine = (lane < hd) if hh == 0 else (lane >= hd)
        mine_k = (lane_k < hd) if hh == 0 else (lane_k >= hd)
        gate_t = _dot_nt(jnp.where(mine_k, km, 0.0).astype(BF16), qblk)
        sel_t = _topk_rows(jnp.where(past, gate_t, NEG), n_idx, MOBA_TOPK, -3.0e38)
        allowed = ((sel_t > 0.5) & past) | (n_idx == own)
        maskneg = jnp.concatenate([jnp.where(allowed, 0.0, NEG), no_block], axis=0).T
        qrows.append(jnp.concatenate([maskneg, jnp.where(mine, qf, 0.0)], axis=1))
    qaug = jnp.concatenate(qrows, axis=0).astype(BF16)
    acc = _causal_flash(qaug, kaug, vaug, bufs, jnp.concatenate([strip_ref[0], strip_ref[1]], axis=0), i)
    o = acc[:, :LANES] / acc[:, LANES:LANES + 1]
    o_ref[...] = jnp.where(lane < hd, o[:TQD], o[TQD:]).astype(o_ref.dtype)


def _moba(main, strips, batch, seq):
    nq = seq // TQD
    pairs = MOBA_HEADS // 2
    return pl.pallas_call(
        functools.partial(_moba_kernel, seq=seq),
        out_shape=jax.ShapeDtypeStruct((batch * seq, MOBA_HEADS * HEAD_DIM), BF16),
        grid=(batch, pairs, nq),
        in_specs=[pl.BlockSpec((TQD, LANES), lambda b, h, i: (b * nq + i, h)),
                  pl.BlockSpec((seq, LANES), lambda b, h, i: (b, pairs + h)),
                  pl.BlockSpec((seq, LANES), lambda b, h, i: (b, 2 * pairs + h)),
                  pl.BlockSpec((2, TQD, 3 * TK), lambda b, h, i: (h, 0, 0))],
        out_specs=pl.BlockSpec((TQD, LANES), lambda b, h, i: (b * nq + i, h)),
        scratch_shapes=[pltpu.VMEM((FRONT + seq, 2 * LANES), BF16), pltpu.VMEM((FRONT + seq, 2 * LANES), BF16),
                        pltpu.VMEM((LANES, LANES), F32), *_flash_buffers(2 * TQD, 2 * LANES)],
        compiler_params=_attention_params(),
        name="moba_attention",
    )(main, main, main, strips)


def _strip_geometry():
    r = np.arange(TQ)[:, None]
    c3 = np.arange(3 * TK)[None, :]
    rel_a = np.arange(TQD)[:, None] + TK - c3
    rel_b = r + 2 * TK - c3
    cc = np.arange(CMP_NEAR)[None, :]
    rel_c = r + CMP_STRIDE * CMP_PAD - (CMP_BLOCK - 1) - CMP_STRIDE * cc
    return (rel_a, rel_a >= 0), (rel_b, rel_b >= 0), (rel_b, (rel_b >= 0) & (rel_b < WINDOW)), (rel_c, rel_c >= 0)


def _overlap_padded(seq):
    nc = (seq - CMP_BLOCK) // CMP_STRIDE + 1
    ncp = seq // CMP_STRIDE
    tok = np.arange(nc)[:, None] * CMP_STRIDE + np.arange(CMP_BLOCK)[None, :]
    ov = np.zeros((CMP_PAD + ncp + 2 * CMP_PER_TQ, LANES), np.float32)
    np.add.at(ov, (CMP_PAD + np.repeat(np.arange(nc), CMP_BLOCK), (tok // SEL_BLOCK).reshape(-1)),
              1.0 / CMP_BLOCK)
    return jnp.asarray(ov, dtype=BF16)


def _strips(bias_table):
    geo_a, geo_sel, geo_win, geo_cmp = _strip_geometry()
    return (_bias_strips(bias_table, geo_a[0], geo_a[1], 0, MOBA_HEADS, "bias_strip_diag"),
            _bias_strips(bias_table, geo_sel[0], geo_sel[1], 0, NSA_HEADS, "bias_strip_sel"),
            _bias_strips(bias_table, geo_win[0], geo_win[1], 0, NSA_HEADS, "bias_strip_win"),
            _bias_strips(bias_table, geo_cmp[0], geo_cmp[1], 0, NSA_HEADS, "bias_strip_cmp"))


def _even_heads(xf, batch, seq, bias_table, g_mix, w_in, pos_k, w1_k, w2_k, pos_v, w1_v, w2_v,
                lam_q1, lam_k1, lam_q2, lam_k2, subln, strips=None):
    d = xf.shape[1]
    strip_a, strip_sel, strip_win, strip_cmp = strips if strips is not None else _strips(bias_table)
    w_main = jnp.concatenate([w_in[:, 0:512], w_in[:, 768:1280], w_in[:, 1304:2840]], axis=1).astype(BF16)
    w_aux = jnp.concatenate([w_in[:, 512:768], w_in[:, 1280:1304],
                             jnp.zeros((d, 3 * LANES - 280), F32)], axis=1).astype(BF16)
    main0, kc0, kc1, vc0, vc1, gates = _in_proj_even(xf, g_mix, w_main, w_aux, 1024, w_main.shape[1] // 2)

    nch = seq // CMP_STRIDE
    chunks = lambda a: a.reshape(batch, nch, CMP_STRIDE * HEAD_DIM)
    pos = jnp.stack([pos_k, pos_v]).reshape(2, 2, CMP_STRIDE * HEAD_DIM)
    ckv = _compress(chunks(kc0), chunks(kc1), chunks(vc0), chunks(vc1), pos,
                    jnp.stack([w1_k, w1_v]).astype(BF16), jnp.stack([w2_k, w2_v]).astype(BF16))

    o_a = _nsa(main0, gates, ckv, strip_sel, strip_win, strip_cmp, _overlap_padded(seq), batch, seq)
    lam_p = jnp.stack([lam_q1, lam_k1, lam_q2, lam_k2])
    lam_init = 0.8 - 0.6 * math.exp(-0.3 * 0)
    o_b = _diff(main0, lam_p, subln, strip_a, batch, seq, lam_init)
    return o_a, o_b, ckv


def _odd_heads(xf, batch, seq, bias_table, g_mix, w_in, strips=None):
    strip_a = (strips if strips is not None else _strips(bias_table))[0]
    main1 = _norm_matmul(xf, g_mix, w_in.astype(BF16), BF16, 1024, w_in.shape[1] // 2, "in_proj_odd")
    return _moba(main1, strip_a, batch, seq)


def kernel(x, bias_table, norm_mix, norm_mlp, norm_final, mlp_w1, mlp_w2, ev_w_in, ev_w_out,
           ev_cmp_pos_k, ev_cmp_w1_k, ev_cmp_w2_k, ev_cmp_pos_v, ev_cmp_w1_v, ev_cmp_w2_v,
           ev_lam_q1, ev_lam_k1, ev_lam_q2, ev_lam_k2, ev_subln, od_w_in, od_w_out):
    batch, seq, d = x.shape
    assert d == D_MODEL and (batch * seq) % 1024 == 0 and seq % TQD == 0
    assert SEL_TOPN * SEL_BLOCK <= seq <= LANES * SEL_BLOCK and seq >= MOBA_TOPK * MOBA_BLOCK
    xf = x.reshape(batch * seq, d)
    strips = _strips(bias_table)

    o_a, o_b, _ = _even_heads(xf, batch, seq, bias_table, norm_mix[0], ev_w_in[0],
                              ev_cmp_pos_k[0], ev_cmp_w1_k[0], ev_cmp_w2_k[0],
                              ev_cmp_pos_v[0], ev_cmp_w1_v[0], ev_cmp_w2_v[0],
                              ev_lam_q1[0], ev_lam_k1[0], ev_lam_q2[0], ev_lam_k2[0], ev_subln[0], strips)
    wo = ev_w_out[0].astype(BF16)
    x1 = _post(xf, o_a, o_b, 0, wo[:512], wo[512:], norm_mlp[0], mlp_w1[0].astype(BF16),
               mlp_w2[0].astype(BF16), norm_final, False, 1024, 1024, "post_even")

    o_m = _odd_heads(x1, batch, seq, bias_table, norm_mix[1], od_w_in[0], strips)
    wo = od_w_out[0].astype(BF16)
    out = _post(x1, o_m, o_m, 1, wo[:512], wo[512:], norm_mlp[1], mlp_w1[1].astype(BF16),
                mlp_w2[1].astype(BF16), norm_final, True, 1024, 1024, "post_odd")
    return out.reshape(batch, seq, d)
```

```python
import functools
import math

import numpy as np
import jax
import jax.numpy as jnp
from jax import lax
from jax.experimental import pallas as pl
from jax.experimental.pallas import tpu as pltpu

F32 = jnp.float32
BF16 = jnp.bfloat16

D_MODEL = 1024
HEAD_DIM = 64
EPS = 1e-6
NEG = -1e30
NUM_BUCKETS = 32
MAX_DISTANCE = 128
NSA_HEADS = 8
NSA_KV_HEADS = 2
NSA_GROUP = 4
CMP_BLOCK = 32
CMP_STRIDE = 16
SEL_BLOCK = 64
SEL_TOPN = 16
WINDOW = 512
FORCE_BONUS = 1e3
DIFF_HEADS = 4
MOBA_HEADS = 16
MOBA_BLOCK = 256
MOBA_TOPK = 3
D_FF = 4 * D_MODEL
SCALE = HEAD_DIM ** -0.5

LANES = 128
TQ = 256
TQD = 512
TK = 256
FAR_REL = 113
CMP_PER_TQ = TQ // CMP_STRIDE
CMP_NEAR = 128
CMP_PAD = CMP_NEAR - 2 * CMP_PER_TQ
GROUP = 4
SUB = 2 * TK
BAND_PAD = (GROUP + 1) * TK
FRONT = GROUP * TK
LOG2E = math.log2(math.e)
QSCALE = SCALE * LOG2E
VMEM_LIMIT = 56 * 1024 * 1024


def _attention_params():
    return pltpu.CompilerParams(dimension_semantics=("arbitrary", "arbitrary", "arbitrary"),
                                vmem_limit_bytes=VMEM_LIMIT)


def _dot_nt(a, b):
    return lax.dot_general(a, b, (((1,), (1,)), ((), ())), preferred_element_type=F32)


def _dot(a, b):
    return jnp.dot(a, b, preferred_element_type=F32)


def _rms(x, g):
    return x * lax.rsqrt(jnp.mean(x * x, axis=-1, keepdims=True) + EPS) * g


def _t5_bucket_np(rel):
    n = np.maximum(rel, 0)
    max_exact = NUM_BUCKETS // 2
    large = max_exact + (np.log(np.maximum(n, 1).astype(np.float32) / np.float32(max_exact))
                         / np.float32(math.log(MAX_DISTANCE / max_exact))
                         * np.float32(NUM_BUCKETS - max_exact)).astype(np.int32)
    large = np.minimum(large, NUM_BUCKETS - 1)
    return np.where(n < max_exact, n, large).astype(np.int32)


def _strip_kernel(tbl_ref, bucket_ref, valid_ref, o_ref, *, head0):
    h = pl.program_id(0) + head0
    bucket = bucket_ref[...]
    last = tbl_ref[NUM_BUCKETS - 1, h]
    acc = jnp.zeros(bucket.shape, F32)
    for b in range(NUM_BUCKETS - 1):
        acc = jnp.where(bucket == b, tbl_ref[b, h] - last, acc)
    o_ref[0] = jnp.where(valid_ref[...] != 0, acc * LOG2E, NEG)


def _toeplitz_kernel(tbl_ref, bucket_ref, valid_ref, o_ref, *, rows, cols):
    h = pl.program_id(0)
    bucket = bucket_ref[...]
    last = tbl_ref[NUM_BUCKETS - 1, h]
    acc = jnp.zeros(bucket.shape, F32)
    for b in range(NUM_BUCKETS - 1):
        acc = jnp.where(bucket == b, tbl_ref[b, h] - last, acc)
    line = jnp.where(valid_ref[...] != 0, acc * LOG2E, NEG)
    wide = jnp.broadcast_to(line[0:1], (rows, line.shape[1]))
    o_ref[0] = pltpu.roll(wide, 0, 1, stride=1, stride_axis=0)[:, :cols]


def _toeplitz_strips(table, rows, cols, off, lo, hi, nheads, name):
    width = -(-(rows + cols) // LANES) * LANES
    j = np.arange(width)
    rel = off - np.where(j < cols, j, j - width)
    valid = (rel >= lo) & (rel < hi)
    tile8 = lambda a: jnp.asarray(np.broadcast_to(a[None, :], (8, width)).astype(np.int32))
    return pl.pallas_call(
        functools.partial(_toeplitz_kernel, rows=rows, cols=cols),
        out_shape=jax.ShapeDtypeStruct((nheads, rows, cols), F32),
        grid=(nheads,),
        in_specs=[pl.BlockSpec(memory_space=pltpu.SMEM),
                  pl.BlockSpec((8, width), lambda h: (0, 0)),
                  pl.BlockSpec((8, width), lambda h: (0, 0))],
        out_specs=pl.BlockSpec((1, rows, cols), lambda h: (h, 0, 0)),
        name=name,
    )(table, tile8(_t5_bucket_np(rel)), tile8(valid))


def _bias_strips(table, rel, valid, head0, nheads, name):
    rows, cols = rel.shape
    bucket = jnp.asarray(_t5_bucket_np(rel))
    validi = jnp.asarray(valid.astype(np.int32))
    return pl.pallas_call(
        functools.partial(_strip_kernel, head0=head0),
        out_shape=jax.ShapeDtypeStruct((nheads, rows, cols), F32),
        grid=(nheads,),
        in_specs=[pl.BlockSpec(memory_space=pltpu.SMEM),
                  pl.BlockSpec((rows, cols), lambda h: (0, 0)),
                  pl.BlockSpec((rows, cols), lambda h: (0, 0))],
        out_specs=pl.BlockSpec((1, rows, cols), lambda h: (h, 0, 0)),
        name=name,
    )(table, bucket, validi)


def _norm_matmul_kernel(x_ref, g_ref, w_ref, o_ref, xn_ref):
    @pl.when(pl.program_id(1) == 0)
    def _():
        xn_ref[...] = _rms(x_ref[...], g_ref[...]).astype(BF16)

    o_ref[...] = _dot(xn_ref[...], w_ref[...]).astype(o_ref.dtype)


def _norm_matmul(x, g, w, out_dtype, tm, tn, name):
    m, d = x.shape
    n = w.shape[1]
    return pl.pallas_call(
        _norm_matmul_kernel,
        out_shape=jax.ShapeDtypeStruct((m, n), out_dtype),
        grid=(m // tm, n // tn),
        in_specs=[pl.BlockSpec((tm, d), lambda i, j: (i, 0)),
                  pl.BlockSpec((1, d), lambda i, j: (0, 0)),
                  pl.BlockSpec((d, tn), lambda i, j: (0, j))],
        out_specs=pl.BlockSpec((tm, tn), lambda i, j: (i, j)),
        scratch_shapes=[pltpu.VMEM((tm, d), BF16)],
        compiler_params=pltpu.CompilerParams(
            dimension_semantics=("arbitrary", "arbitrary"), vmem_limit_bytes=VMEM_LIMIT),
        name=name,
    )(x, g.reshape(1, d), w)


def _in_proj_even_kernel(x_ref, g_ref, w_ref, wa_ref, o_ref, kc0_ref, kc1_ref, vc0_ref, vc1_ref, gate_ref,
                         xn_ref):
    hd = HEAD_DIM

    @pl.when(pl.program_id(1) == 0)
    def _():
        xn = _rms(x_ref[...], g_ref[...]).astype(BF16)
        xn_ref[...] = xn
        aux = _dot(xn, wa_ref[...])
        kc0_ref[...] = aux[:, 0:hd]
        kc1_ref[...] = aux[:, hd:2 * hd]
        vc0_ref[...] = aux[:, 2 * hd:3 * hd]
        vc1_ref[...] = aux[:, 3 * hd:4 * hd]
        gate_ref[...] = aux[:, 4 * hd:]

    o_ref[...] = _dot(xn_ref[...], w_ref[...]).astype(o_ref.dtype)


def _in_proj_even(x, g, w_main, w_aux, tm, tn):
    m, d = x.shape
    n = w_main.shape[1]
    na = w_aux.shape[1]
    row_spec = lambda width: pl.BlockSpec((tm, width), lambda i, j: (i, 0))
    cmp_shape = jax.ShapeDtypeStruct((m, HEAD_DIM), F32)
    return pl.pallas_call(
        _in_proj_even_kernel,
        out_shape=(jax.ShapeDtypeStruct((m, n), BF16), cmp_shape, cmp_shape, cmp_shape, cmp_shape,
                   jax.ShapeDtypeStruct((m, na - 4 * HEAD_DIM), F32)),
        grid=(m // tm, n // tn),
        in_specs=[row_spec(d),
                  pl.BlockSpec((1, d), lambda i, j: (0, 0)),
                  pl.BlockSpec((d, tn), lambda i, j: (0, j)),
                  pl.BlockSpec((d, na), lambda i, j: (0, 0))],
        out_specs=(pl.BlockSpec((tm, tn), lambda i, j: (i, j)), row_spec(HEAD_DIM), row_spec(HEAD_DIM),
                   row_spec(HEAD_DIM), row_spec(HEAD_DIM), row_spec(na - 4 * HEAD_DIM)),
        scratch_shapes=[pltpu.VMEM((tm, d), BF16)],
        compiler_params=pltpu.CompilerParams(
            dimension_semantics=("arbitrary", "arbitrary"), vmem_limit_bytes=VMEM_LIMIT),
        name="in_proj_even",
    )(x, g.reshape(1, d), w_main, w_aux)


def _post_kernel(x_ref, a0_ref, a1_ref, wo0_ref, wo1_ref, g_ref, w1_ref, w2_ref, gf_ref,
                 o_ref, acc_ref, xn_ref, *, final_norm):
    f = pl.program_id(1)

    @pl.when(f == 0)
    def _():
        x1 = x_ref[...] + _dot(a0_ref[...], wo0_ref[...]) + _dot(a1_ref[...], wo1_ref[...])
        acc_ref[...] = x1
        xn_ref[...] = _rms(x1, g_ref[...]).astype(BF16)

    h = jnp.square(jnp.maximum(_dot(xn_ref[...], w1_ref[...]), 0.0))
    acc_ref[...] += _dot(h.astype(BF16), w2_ref[...])

    @pl.when(f == pl.num_programs(1) - 1)
    def _():
        y = acc_ref[...]
        if final_norm:
            y = _rms(y, gf_ref[...])
        o_ref[...] = y


def _post(x, a0, a1, a1_colblk, wo0, wo1, g, w1, w2, gf, final_norm, tm, tf, name):
    m, d = x.shape
    k0 = wo0.shape[0]
    k1 = wo1.shape[0]
    ff = w1.shape[1]
    return pl.pallas_call(
        functools.partial(_post_kernel, final_norm=final_norm),
        out_shape=jax.ShapeDtypeStruct((m, d), F32),
        grid=(m // tm, ff // tf),
        in_specs=[pl.BlockSpec((tm, d), lambda i, f: (i, 0)),
                  pl.BlockSpec((tm, k0), lambda i, f: (i, 0)),
                  pl.BlockSpec((tm, k1), lambda i, f: (i, a1_colblk)),
                  pl.BlockSpec((k0, d), lambda i, f: (0, 0)),
                  pl.BlockSpec((k1, d), lambda i, f: (0, 0)),
                  pl.BlockSpec((1, d), lambda i, f: (0, 0)),
                  pl.BlockSpec((d, tf), lambda i, f: (0, f)),
                  pl.BlockSpec((tf, d), lambda i, f: (f, 0)),
                  pl.BlockSpec((1, d), lambda i, f: (0, 0))],
        out_specs=pl.BlockSpec((tm, d), lambda i, f: (i, 0)),
        scratch_shapes=[pltpu.VMEM((tm, d), F32), pltpu.VMEM((tm, d), BF16)],
        compiler_params=pltpu.CompilerParams(
            dimension_semantics=("arbitrary", "arbitrary"), vmem_limit_bytes=VMEM_LIMIT),
        name=name,
    )(x, a0, a1, wo0, wo1, g.reshape(1, d), w1, w2, gf.reshape(1, d))


def _compress_kernel(kc0_ref, kc1_ref, vc0_ref, vc1_ref, pos_ref, w1_ref, w2_ref, o_ref):
    half = CMP_STRIDE * HEAD_DIM
    for kv, refs in enumerate(((kc0_ref, kc1_ref), (vc0_ref, vc1_ref))):
        w1a = w1_ref[kv, :half, :]
        w1b = w1_ref[kv, half:, :]
        pos = pos_ref[kv].astype(BF16)
        pa = jnp.broadcast_to(pos[0:1], (8, half))
        pb = jnp.broadcast_to(pos[1:2], (8, half))
        posterm = (_dot(pa, w1a) + _dot(pb, w1b))[0:1]
        for grp, x_ref in enumerate(refs):
            x = x_ref[0].astype(BF16)
            first = _dot(x, w1a)
            second = _dot(x, w1b)
            second = pltpu.roll(second, second.shape[0] - 1, 0)
            hid = jax.nn.gelu(first + second + posterm)
            o_ref[kv, grp] = _dot(hid.astype(BF16), w2_ref[kv])


def _compress(kc0, kc1, vc0, vc1, pos, w1, w2):
    batch, nch, wide = kc0.shape
    x_spec = pl.BlockSpec((1, nch, wide), lambda b: (b, 0, 0))
    full = lambda a: pl.BlockSpec(a.shape, lambda b: (0,) * a.ndim)
    return pl.pallas_call(
        _compress_kernel,
        out_shape=jax.ShapeDtypeStruct((2, batch * NSA_KV_HEADS, nch, HEAD_DIM), F32),
        grid=(batch,),
        in_specs=[x_spec, x_spec, x_spec, x_spec, full(pos), full(w1), full(w2)],
        out_specs=pl.BlockSpec((2, NSA_KV_HEADS, nch, HEAD_DIM), lambda b: (0, b, 0, 0)),
        compiler_params=pltpu.CompilerParams(
            dimension_semantics=("arbitrary",), vmem_limit_bytes=VMEM_LIMIT),
        name="nsa_compress",
    )(kc0, kc1, vc0, vc1, pos, w1, w2)


def _softmax_part(m, s):
    m_new = jnp.maximum(m, jnp.max(s, axis=1, keepdims=True))
    return m_new, jnp.exp2(m - m_new), jnp.exp2(s - m_new).astype(BF16)


def _flash_chain(q, k_sc, v_sc, bufs, far0, n_full, tail, side_work=None):
    s_sc, p_sc, acc_sc = bufs
    rows = q.shape[0]
    assert GROUP * TK == 2 * SUB and tail[0][1] == SUB
    tail0 = tail[0][0]

    def qk(r0):
        return _dot_nt(q, k_sc[pl.ds(pl.multiple_of(r0, TK), SUB), :])

    s_sc[...] = qk(jnp.where(n_full > 0, far0, tail0))
    p_sc[...] = jnp.zeros(p_sc.shape, BF16)
    acc_sc[...] = jnp.zeros(acc_sc.shape, F32)

    def far_body(j, m):
        base = pl.multiple_of(far0 + j * (GROUP * TK), SUB)
        acc = acc_sc[...] + _dot(p_sc[...], v_sc[pl.ds(base - SUB, SUB), :])
        m, alpha, p = _softmax_part(m, s_sc[...])
        s_next = qk(base + SUB)
        acc = alpha * acc + _dot(p, v_sc[pl.ds(base, SUB), :])
        m, alpha, p = _softmax_part(m, s_next)
        s_sc[...] = qk(jnp.where(j + 1 < n_full, base + GROUP * TK, tail0))
        p_sc[...] = p
        acc_sc[...] = alpha * acc
        return m

    m = lax.fori_loop(0, n_full, far_body, jnp.full((rows, 1), NEG, F32))
    if side_work is not None:
        side_work()
    base = pl.multiple_of(far0 + n_full * (GROUP * TK), SUB)
    acc = acc_sc[...] + _dot(p_sc[...], v_sc[pl.ds(base - SUB, SUB), :])
    m, alpha, p = _softmax_part(m, s_sc[...] + tail[0][2]())
    acc = alpha * acc + _dot(p, v_sc[pl.ds(tail0, SUB), :])
    return _flash_tail((m, acc), q, k_sc, v_sc, tail[1:])


def _flash_buffers(rows, ncols):
    return [pltpu.VMEM((rows, SUB), F32), pltpu.VMEM((rows, SUB), BF16), pltpu.VMEM((rows, ncols), F32)]


def _flash_tail(carry, q, k_sc, v_sc, tail):
    m, acc = carry
    for r0, width, extra in tail:
        m, alpha, p = _softmax_part(m, _dot_nt(q, k_sc[pl.ds(r0, width), :]) + extra())
        acc = alpha * acc + _dot(p, v_sc[pl.ds(r0, width), :])
    return acc


def _tile_gate(cond):
    return jnp.where(cond, 0.0, NEG)


def _gate_pair(cond_a, cond_b):
    lane = lax.broadcasted_iota(jnp.int32, (1, 2 * TK), 1)
    return jnp.where(lane < TK, _tile_gate(cond_a), _tile_gate(cond_b))


def _tail_steps(win0, t0, done, rows, strip, near_ok):
    def mid():
        return jnp.concatenate([jnp.broadcast_to(_tile_gate(t0 + 2 >= done), (rows, TK)),
                                strip(0, TK) + _tile_gate(near_ok)], axis=1)

    return [(win0, SUB, lambda: _gate_pair(t0 >= done, t0 + 1 >= done)),
            (win0 + SUB, SUB, mid),
            (win0 + 2 * SUB, SUB, lambda: strip(TK, 3 * TK))]


def _topk_rows(score, idx, k, floor):
    sel = jnp.zeros(score.shape, F32)
    nrow = score.shape[0]
    for _ in range(k):
        mx = jnp.max(score, axis=0, keepdims=True)
        first = jnp.min(jnp.where(score == mx, idx, nrow), axis=0, keepdims=True)
        pick = idx == first
        sel = jnp.where(pick, 1.0, sel)
        score = jnp.where(pick, floor, score)
    return sel


def _nsa_kernel(q_ref, qn_ref, ks_ref, vs_ref, kw_ref, vw_ref, kc_ref, vc_ref, gate_ref,
                ssel_ref, swin_ref, scmp_ref, ov_ref, o_ref,
                kaug, vaug, kwp, vwp, kcp, vcp, qaug_sc, ocw_sc, *bufs, seq):
    g = pl.program_id(1)
    i = pl.program_id(2)
    ncp = seq // CMP_STRIDE
    hd = HEAD_DIM
    rows = NSA_GROUP * TQ

    @pl.when(i == 0)
    def _():
        lane = lax.broadcasted_iota(jnp.int32, (BAND_PAD, LANES), 1)
        padflag = jnp.where(lane == hd, 1.0, 0.0).astype(BF16)
        kaug[0:BAND_PAD, 0:LANES] = jnp.zeros((BAND_PAD, LANES), BF16)
        kaug[0:BAND_PAD, LANES:2 * LANES] = padflag
        kwp[0:BAND_PAD, :] = padflag
        vaug[0:BAND_PAD, :] = jnp.zeros((BAND_PAD, LANES), BF16)
        vwp[0:BAND_PAD, :] = jnp.zeros((BAND_PAD, LANES), BF16)
        onescol = jnp.where(lax.broadcasted_iota(jnp.int32, (TK, LANES), 1) == hd, 1.0, 0.0)
        zero_hd = jnp.zeros((TK, hd), F32)
        col = lax.broadcasted_iota(jnp.int32, (TK, LANES), 1)
        row = lax.broadcasted_iota(jnp.int32, (TK, LANES), 0)

        def fill(c, _):
            src = pl.multiple_of(c * TK, TK)
            dst = pl.multiple_of(BAND_PAD + c * TK, TK)

            def pick(ref):
                both = ref[pl.ds(src, TK), :].astype(F32)
                return jnp.concatenate([jnp.where(g == 0, both[:, :hd], both[:, hd:]), zero_hd], axis=1)

            kaug[pl.ds(dst, TK), 0:LANES] = jnp.where((src + row) // SEL_BLOCK == col, 1.0, 0.0).astype(BF16)
            kaug[pl.ds(dst, TK), LANES:2 * LANES] = pick(ks_ref).astype(BF16)
            kwp[pl.ds(dst, TK), :] = pick(kw_ref).astype(BF16)
            vaug[pl.ds(dst, TK), :] = (pick(vs_ref) + onescol).astype(BF16)
            vwp[pl.ds(dst, TK), :] = (pick(vw_ref) + onescol).astype(BF16)
            return 0

        lax.fori_loop(0, seq // TK, fill, 0)
        lane_c = lax.broadcasted_iota(jnp.int32, (CMP_PAD, LANES), 1)
        kcp[0:CMP_PAD, :] = jnp.where(lane_c == hd, 1.0, 0.0).astype(BF16)
        vcp[0:CMP_PAD, :] = jnp.zeros((CMP_PAD, LANES), BF16)
        zc = jnp.zeros((ncp, hd), F32)
        onesc = jnp.where(lax.broadcasted_iota(jnp.int32, (ncp, LANES), 1) == hd, 1.0, 0.0)
        kcp[CMP_PAD:CMP_PAD + ncp, :] = jnp.concatenate([kc_ref[0, 0], zc], axis=1).astype(BF16)
        vcp[CMP_PAD:CMP_PAD + ncp, :] = (jnp.concatenate([vc_ref[0, 0], zc], axis=1) + onesc).astype(BF16)
        tail = kcp.shape[0] - CMP_PAD - ncp
        kcp[CMP_PAD + ncp:, :] = jnp.zeros((tail, LANES), BF16)
        vcp[CMP_PAD + ncp:, :] = jnp.zeros((tail, LANES), BF16)

    def stack(ref, lo, hi):
        return jnp.concatenate([ref[r, :, lo:hi] for r in range(NSA_GROUP)], axis=0)

    def prepare(qsrc_ref, t, slot):
        qblk = qsrc_ref[...].astype(F32) * QSCALE
        lane_t = lax.broadcasted_iota(jnp.int32, (TQ, hd), 1)
        qtail = jnp.where(lane_t == 0, NEG, 0.0)
        qs = [qblk[:, r * hd:(r + 1) * hd] for r in range(NSA_GROUP)]
        q128 = jnp.concatenate([jnp.concatenate([qs[r], qtail], axis=1) for r in range(NSA_GROUP)],
                               axis=0).astype(BF16)

        s_far = _dot_nt(q128, kcp[CMP_PAD:CMP_PAD + ncp, :])
        c_idx = lax.broadcasted_iota(jnp.int32, (1, ncp), 1)
        far_ok = c_idx < CMP_PER_TQ * t - CMP_PAD
        near0 = pl.multiple_of(CMP_PER_TQ * t, CMP_PER_TQ)
        s_near = _dot_nt(q128, kcp[pl.ds(near0, CMP_NEAR), :]) + stack(scmp_ref, 0, CMP_NEAR)
        near_ok = s_near > 0.5 * NEG
        s_far = jnp.where(far_ok, s_far, NEG)
        m_c = jnp.maximum(jnp.max(s_far, axis=1, keepdims=True), jnp.max(s_near, axis=1, keepdims=True))
        p_far = jnp.where(far_ok, jnp.exp2(s_far - m_c), 0.0)
        p_near = jnp.where(near_ok, jnp.exp2(s_near - m_c), 0.0)
        den = jnp.maximum(jnp.sum(p_far, axis=1, keepdims=True) + jnp.sum(p_near, axis=1, keepdims=True), 1e-30)
        pc_far = p_far / den
        pc_near = p_near / den
        ocw_sc[slot, 0] = (_dot(pc_far.astype(BF16), vcp[CMP_PAD:CMP_PAD + ncp, :])
                           + _dot(pc_near.astype(BF16), vcp[pl.ds(near0, CMP_NEAR), :]))

        band0 = pl.multiple_of(TQ * t, TK) + 3 * TK
        steps = [(band0, 2 * TK, lambda: stack(swin_ref, 0, 2 * TK)),
                 (band0 + 2 * TK, TK, lambda: stack(swin_ref, 2 * TK, 3 * TK))]
        acc = _flash_tail((jnp.full((rows, 1), NEG, F32), jnp.zeros((rows, LANES), F32)), q128, kwp, vwp, steps)
        ocw_sc[slot, 1] = acc / acc[:, hd:hd + 1]

        pcs_far = sum(pc_far[r * TQ:(r + 1) * TQ] for r in range(NSA_GROUP))
        pcs_near = sum(pc_near[r * TQ:(r + 1) * TQ] for r in range(NSA_GROUP))

        def hi_lo_dot(p, w):
            hi = p.astype(BF16)
            lo = (p - hi.astype(F32)).astype(BF16)
            return _dot(hi, w) + _dot(lo, w)

        imp = (hi_lo_dot(pcs_far, ov_ref[CMP_PAD:CMP_PAD + ncp, :])
               + hi_lo_dot(pcs_near, ov_ref[pl.ds(near0, CMP_NEAR), :]))
        imp_t = imp.T
        n_idx = lax.broadcasted_iota(jnp.int32, (LANES, TQ), 0)
        cur = (TQ * t + lax.broadcasted_iota(jnp.int32, (LANES, TQ), 1)) // SEL_BLOCK
        eligible = n_idx <= cur
        forced = ((n_idx == 0) | (n_idx == cur) | (n_idx == cur - 1)) & eligible
        rest = jnp.where(forced, -3.0e38, jnp.where(eligible, imp_t, -1.0))
        sel_t = _topk_rows(rest, n_idx, SEL_TOPN - 3, -3.0e38)
        maskneg = jnp.where(forced | (sel_t > 0.5), 0.0, NEG).T
        qaug_sc[slot] = jnp.concatenate(
            [jnp.concatenate([maskneg, qs[r], qtail], axis=1) for r in range(NSA_GROUP)],
            axis=0).astype(BF16)

    @pl.when(i == 0)
    def _():
        prepare(q_ref, 0, 0)

    slot = i % 2
    n_full = jnp.maximum(i - 2, 0) // GROUP
    win0 = pl.multiple_of(TQ * i, TK)
    tail = _tail_steps(win0, i - 5, GROUP * n_full, rows, functools.partial(stack, ssel_ref), True)
    prepare_next = functools.partial(prepare, qn_ref, jnp.minimum(i + 1, pl.num_programs(2) - 1), 1 - slot)
    acc = _flash_chain(qaug_sc[slot], kaug, vaug, bufs, BAND_PAD, n_full, tail, prepare_next)
    o_sel = acc[:, :hd] / acc[:, hd:hd + 1]

    o_cmp = ocw_sc[slot, 0]
    o_win = ocw_sc[slot, 1]
    sig = jax.nn.sigmoid(gate_ref[...])
    outs = []
    for r in range(NSA_GROUP):
        def gcol(branch):
            c0 = branch * NSA_HEADS + r
            c1 = c0 + NSA_GROUP
            return jnp.where(g == 0, sig[:, c0:c0 + 1], sig[:, c1:c1 + 1])
        sl = slice(r * TQ, (r + 1) * TQ)
        outs.append(gcol(0) * o_cmp[sl, :hd] + gcol(1) * o_sel[sl] + gcol(2) * o_win[sl, :hd])
    o_ref[...] = jnp.concatenate(outs, axis=1).astype(o_ref.dtype)


def _nsa(main, aux, ckv, ssel, swin, scmp, ov, batch, seq):
    nq = seq // TQ
    ncp = seq // CMP_STRIDE
    gw = NSA_GROUP * HEAD_DIM
    once = pl.Buffered(1)
    kv_spec = lambda colblk: pl.BlockSpec((seq, LANES), lambda b, g, i: (b, colblk), pipeline_mode=once)
    strip_spec = lambda cols: pl.BlockSpec((NSA_GROUP, TQ, cols), lambda b, g, i: (g, 0, 0), pipeline_mode=once)
    return pl.pallas_call(
        functools.partial(_nsa_kernel, seq=seq),
        out_shape=jax.ShapeDtypeStruct((batch * seq, NSA_HEADS * HEAD_DIM), BF16),
        grid=(batch, NSA_KV_HEADS, nq),
        in_specs=[pl.BlockSpec((TQ, gw), lambda b, g, i: (b * nq + i, g)),
                  pl.BlockSpec((TQ, gw), lambda b, g, i: (b * nq + jnp.minimum(i + 1, nq - 1), g)),
                  kv_spec(4), kv_spec(5), kv_spec(6), kv_spec(7),
                  pl.BlockSpec((1, 1, ncp, HEAD_DIM), lambda b, g, i: (0, b * NSA_KV_HEADS + g, 0, 0)),
                  pl.BlockSpec((1, 1, ncp, HEAD_DIM), lambda b, g, i: (1, b * NSA_KV_HEADS + g, 0, 0)),
                  pl.BlockSpec((TQ, LANES), lambda b, g, i: (b * nq + i, 0)),
                  strip_spec(3 * TK), strip_spec(3 * TK), strip_spec(CMP_NEAR),
                  pl.BlockSpec(ov.shape, lambda b, g, i: (0, 0))],
        out_specs=pl.BlockSpec((TQ, gw), lambda b, g, i: (b * nq + i, g)),
        scratch_shapes=[pltpu.VMEM((BAND_PAD + seq, 2 * LANES), BF16),
                        pltpu.VMEM((BAND_PAD + seq, LANES), BF16),
                        pltpu.VMEM((BAND_PAD + seq, LANES), BF16),
                        pltpu.VMEM((BAND_PAD + seq, LANES), BF16),
                        pltpu.VMEM((ov.shape[0], LANES), BF16),
                        pltpu.VMEM((ov.shape[0], LANES), BF16),
                        pltpu.VMEM((2, NSA_GROUP * TQ, 2 * LANES), BF16),
                        pltpu.VMEM((2, 2, NSA_GROUP * TQ, LANES), F32),
                        *_flash_buffers(NSA_GROUP * TQ, LANES)],
        compiler_params=_attention_params(),
        name="nsa_attention",
    )(main, main, main, main, main, main, ckv, ckv, aux, ssel, swin, scmp, ov)


def _ones_column():
    return jnp.where(lax.broadcasted_iota(jnp.int32, (TK, LANES), 1) == 0, 1.0, 0.0).astype(BF16)


def _causal_flash(q, k_sc, v_sc, bufs, strip, i):
    n_full = jnp.maximum(2 * i - 1, 0) // GROUP
    win0 = pl.multiple_of(i * TQD, SUB)
    tail = _tail_steps(win0, 2 * i - 4, GROUP * n_full, q.shape[0], strip, i >= 1)
    return _flash_chain(q, k_sc, v_sc, bufs, FRONT, n_full, tail)


def _diff_kernel(q_ref, k_ref, v_ref, lam_ref, sub_ref, strip_ref, o_ref, k_sc, v_sc, *bufs, lam_init, seq):
    i = pl.program_id(2)
    hd = HEAD_DIM

    @pl.when(i == 0)
    def _():
        k_sc[0:FRONT, :] = jnp.zeros((FRONT, k_sc.shape[1]), BF16)
        v_sc[0:FRONT, :] = jnp.zeros((FRONT, v_sc.shape[1]), BF16)
        ones = _ones_column()

        def fill(c, _):
            r0 = pl.multiple_of(c * TK, TK)
            k_sc[pl.ds(FRONT + r0, TK), :] = k_ref[pl.ds(r0, TK), :]
            v_sc[pl.ds(FRONT + r0, TK), 0:LANES] = v_ref[pl.ds(r0, TK), :]
            v_sc[pl.ds(FRONT + r0, TK), LANES:2 * LANES] = ones
            return 0

        lax.fori_loop(0, seq // TK, fill, 0)

    q = q_ref[...].astype(F32) * QSCALE
    lane = lax.broadcasted_iota(jnp.int32, (TQD, 2 * hd), 1)
    qst = jnp.concatenate([jnp.where(lane < hd, q, 0.0), jnp.where(lane < hd, 0.0, q)],
                          axis=0).astype(BF16)
    strip = lambda lo, hi: jnp.concatenate([strip_ref[0, :, lo:hi]] * 2, axis=0)
    acc = _causal_flash(qst, k_sc, v_sc, bufs, strip, i)
    o = acc[:, :2 * hd] / acc[:, 2 * hd:2 * hd + 1]
    lam_p = lam_ref[...]
    lam = (jnp.exp(jnp.sum(lam_p[0:1] * lam_p[1:2], axis=1, keepdims=True))
           - jnp.exp(jnp.sum(lam_p[2:3] * lam_p[3:4], axis=1, keepdims=True)) + lam_init)
    d = o[:TQD] - lam * o[TQD:]
    o_ref[...] = (_rms(d, sub_ref[...]) * (1.0 - lam_init)).astype(o_ref.dtype)


def _diff(main, lam_p, subln, strips, batch, seq, lam_init):
    nq = seq // TQD
    blk = 2 * HEAD_DIM
    q0, k0, v0 = 1024 // blk, 1536 // blk, 2048 // blk
    return pl.pallas_call(
        functools.partial(_diff_kernel, lam_init=lam_init, seq=seq),
        out_shape=jax.ShapeDtypeStruct((batch * seq, DIFF_HEADS * blk), BF16),
        grid=(batch, DIFF_HEADS, nq),
        in_specs=[pl.BlockSpec((TQD, blk), lambda b, h, i: (b * nq + i, q0 + h)),
                  pl.BlockSpec((seq, blk), lambda b, h, i: (b, k0 + h)),
                  pl.BlockSpec((seq, blk), lambda b, h, i: (b, v0 + h)),
                  pl.BlockSpec((4, HEAD_DIM), lambda b, h, i: (0, 0)),
                  pl.BlockSpec((1, blk), lambda b, h, i: (0, 0)),
                  pl.BlockSpec((1, TQD, 3 * TK), lambda b, h, i: (NSA_HEADS + h, 0, 0))],
        out_specs=pl.BlockSpec((TQD, blk), lambda b, h, i: (b * nq + i, h)),
        scratch_shapes=[pltpu.VMEM((FRONT + seq, LANES), BF16), pltpu.VMEM((FRONT + seq, 2 * LANES), BF16),
                        *_flash_buffers(2 * TQD, 2 * LANES)],
        compiler_params=_attention_params(),
        name="diff_attention",
    )(main, main, main, lam_p, subln.reshape(1, blk), strips)


def _moba_kernel(q_ref, k_ref, v_ref, strip_ref, o_ref, kaug, vaug, kmean, *bufs, seq):
    i = pl.program_id(2)
    hd = HEAD_DIM
    nb = seq // MOBA_BLOCK

    @pl.when(i == 0)
    def _():
        kmean[...] = jnp.zeros(kmean.shape, F32)
        kaug[0:FRONT, :] = jnp.zeros((FRONT, kaug.shape[1]), BF16)
        vaug[0:FRONT, :] = jnp.zeros((FRONT, vaug.shape[1]), BF16)
        col = lax.broadcasted_iota(jnp.int32, (MOBA_BLOCK, LANES), 1)
        ones = _ones_column()

        def fill(n, _):
            r0 = pl.multiple_of(n * MOBA_BLOCK, MOBA_BLOCK)
            kb = k_ref[pl.ds(r0, MOBA_BLOCK), :]
            kaug[pl.ds(FRONT + r0, MOBA_BLOCK), 0:LANES] = jnp.where(col == n, 1.0, 0.0).astype(BF16)
            kaug[pl.ds(FRONT + r0, MOBA_BLOCK), LANES:2 * LANES] = kb
            vaug[pl.ds(FRONT + r0, MOBA_BLOCK), 0:LANES] = v_ref[pl.ds(r0, MOBA_BLOCK), :]
            vaug[pl.ds(FRONT + r0, MOBA_BLOCK), LANES:2 * LANES] = ones
            kmean[pl.ds(n, 1), :] = jnp.mean(kb.astype(F32), axis=0, keepdims=True)
            return 0

        lax.fori_loop(0, nb, fill, 0)

    qblk = q_ref[...]
    qf = qblk.astype(F32) * QSCALE
    lane = lax.broadcasted_iota(jnp.int32, (TQD, LANES), 1)
    n_idx = lax.broadcasted_iota(jnp.int32, (nb, TQD), 0)
    own = (i * TQD + lax.broadcasted_iota(jnp.int32, (nb, TQD), 1)) // MOBA_BLOCK
    past = n_idx < own
    km = kmean[0:nb, :]
    lane_k = lax.broadcasted_iota(jnp.int32, km.shape, 1)
    no_block = jnp.full((LANES - nb, TQD), NEG, F32)
    qrows = []
    for hh in range(2):
        mine = (lane < hd) if hh == 0 else (lane >= hd)
        mine_k = (lane_k < hd) if hh == 0 else (lane_k >= hd)
        gate_t = _dot_nt(jnp.where(mine_k, km, 0.0).astype(BF16), qblk)
        sel_t = _topk_rows(jnp.where(past, gate_t, NEG), n_idx, MOBA_TOPK, -3.0e38)
        allowed = ((sel_t > 0.5) & past) | (n_idx == own)
        maskneg = jnp.concatenate([jnp.where(allowed, 0.0, NEG), no_block], axis=0).T
        qrows.append(jnp.concatenate([maskneg, jnp.where(mine, qf, 0.0)], axis=1))
    qaug = jnp.concatenate(qrows, axis=0).astype(BF16)
    strip = lambda lo, hi: jnp.concatenate([strip_ref[0, :, lo:hi], strip_ref[1, :, lo:hi]], axis=0)
    acc = _causal_flash(qaug, kaug, vaug, bufs, strip, i)
    o = acc[:, :LANES] / acc[:, LANES:LANES + 1]
    o_ref[...] = jnp.where(lane < hd, o[:TQD], o[TQD:]).astype(o_ref.dtype)


def _moba(main, strips, batch, seq):
    nq = seq // TQD
    pairs = MOBA_HEADS // 2
    return pl.pallas_call(
        functools.partial(_moba_kernel, seq=seq),
        out_shape=jax.ShapeDtypeStruct((batch * seq, MOBA_HEADS * HEAD_DIM), BF16),
        grid=(batch, pairs, nq),
        in_specs=[pl.BlockSpec((TQD, LANES), lambda b, h, i: (b * nq + i, h)),
                  pl.BlockSpec((seq, LANES), lambda b, h, i: (b, pairs + h)),
                  pl.BlockSpec((seq, LANES), lambda b, h, i: (b, 2 * pairs + h)),
                  pl.BlockSpec((2, TQD, 3 * TK), lambda b, h, i: (h, 0, 0))],
        out_specs=pl.BlockSpec((TQD, LANES), lambda b, h, i: (b * nq + i, h)),
        scratch_shapes=[pltpu.VMEM((FRONT + seq, 2 * LANES), BF16), pltpu.VMEM((FRONT + seq, 2 * LANES), BF16),
                        pltpu.VMEM((LANES, LANES), F32), *_flash_buffers(2 * TQD, 2 * LANES)],
        compiler_params=_attention_params(),
        name="moba_attention",
    )(main, main, main, strips)


def _cmp_strip_geometry():
    r = np.arange(TQ)[:, None]
    cc = np.arange(CMP_NEAR)[None, :]
    rel_c = r + CMP_STRIDE * CMP_PAD - (CMP_BLOCK - 1) - CMP_STRIDE * cc
    return rel_c, rel_c >= 0


def _overlap_padded(seq):
    nc = (seq - CMP_BLOCK) // CMP_STRIDE + 1
    ncp = seq // CMP_STRIDE
    tok = np.arange(nc)[:, None] * CMP_STRIDE + np.arange(CMP_BLOCK)[None, :]
    ov = np.zeros((CMP_PAD + ncp + 2 * CMP_PER_TQ, LANES), np.float32)
    np.add.at(ov, (CMP_PAD + np.repeat(np.arange(nc), CMP_BLOCK), (tok // SEL_BLOCK).reshape(-1)),
              1.0 / CMP_BLOCK)
    return jnp.asarray(ov, dtype=BF16)


def _strips(bias_table):
    rel_c, valid_c = _cmp_strip_geometry()
    far = 1 << 30
    return (_toeplitz_strips(bias_table, TQD, 3 * TK, TK, 0, far, MOBA_HEADS, "bias_strip_diag"),
            _toeplitz_strips(bias_table, TQ, 3 * TK, 2 * TK, 0, far, NSA_HEADS, "bias_strip_sel"),
            _toeplitz_strips(bias_table, TQ, 3 * TK, 2 * TK, 0, WINDOW, NSA_HEADS, "bias_strip_win"),
            _bias_strips(bias_table, rel_c, valid_c, 0, NSA_HEADS, "bias_strip_cmp"))


def _even_heads(xf, batch, seq, bias_table, g_mix, w_in, pos_k, w1_k, w2_k, pos_v, w1_v, w2_v,
                lam_q1, lam_k1, lam_q2, lam_k2, subln, strips=None):
    d = xf.shape[1]
    strip_a, strip_sel, strip_win, strip_cmp = strips if strips is not None else _strips(bias_table)
    w_main = jnp.concatenate([w_in[:, 0:512], w_in[:, 768:1280], w_in[:, 1304:2840]], axis=1).astype(BF16)
    w_aux = jnp.concatenate([w_in[:, 512:768], w_in[:, 1280:1304],
                             jnp.zeros((d, 3 * LANES - 280), F32)], axis=1).astype(BF16)
    main0, kc0, kc1, vc0, vc1, gates = _in_proj_even(xf, g_mix, w_main, w_aux, 1024, w_main.shape[1] // 2)

    nch = seq // CMP_STRIDE
    chunks = lambda a: a.reshape(batch, nch, CMP_STRIDE * HEAD_DIM)
    pos = jnp.stack([pos_k, pos_v]).reshape(2, 2, CMP_STRIDE * HEAD_DIM)
    ckv = _compress(chunks(kc0), chunks(kc1), chunks(vc0), chunks(vc1), pos,
                    jnp.stack([w1_k, w1_v]).astype(BF16), jnp.stack([w2_k, w2_v]).astype(BF16))

    o_a = _nsa(main0, gates, ckv, strip_sel, strip_win, strip_cmp, _overlap_padded(seq), batch, seq)
    lam_p = jnp.stack([lam_q1, lam_k1, lam_q2, lam_k2])
    lam_init = 0.8 - 0.6 * math.exp(-0.3 * 0)
    o_b = _diff(main0, lam_p, subln, strip_a, batch, seq, lam_init)
    return o_a, o_b, ckv


def _odd_heads(xf, batch, seq, bias_table, g_mix, w_in, strips=None):
    strip_a = (strips if strips is not None else _strips(bias_table))[0]
    main1 = _norm_matmul(xf, g_mix, w_in.astype(BF16), BF16, 1024, w_in.shape[1] // 2, "in_proj_odd")
    return _moba(main1, strip_a, batch, seq)


def kernel(x, bias_table, norm_mix, norm_mlp, norm_final, mlp_w1, mlp_w2, ev_w_in, ev_w_out,
           ev_cmp_pos_k, ev_cmp_w1_k, ev_cmp_w2_k, ev_cmp_pos_v, ev_cmp_w1_v, ev_cmp_w2_v,
           ev_lam_q1, ev_lam_k1, ev_lam_q2, ev_lam_k2, ev_subln, od_w_in, od_w_out):
    batch, seq, d = x.shape
    assert d == D_MODEL and (batch * seq) % 1024 == 0 and seq % TQD == 0
    assert SEL_TOPN * SEL_BLOCK <= seq <= LANES * SEL_BLOCK and seq >= MOBA_TOPK * MOBA_BLOCK
    xf = x.reshape(batch * seq, d)
    strips = _strips(bias_table)

    o_a, o_b, _ = _even_heads(xf, batch, seq, bias_table, norm_mix[0], ev_w_in[0],
                              ev_cmp_pos_k[0], ev_cmp_w1_k[0], ev_cmp_w2_k[0],
                              ev_cmp_pos_v[0], ev_cmp_w1_v[0], ev_cmp_w2_v[0],
                              ev_lam_q1[0], ev_lam_k1[0], ev_lam_q2[0], ev_lam_k2[0], ev_subln[0], strips)
    wo = ev_w_out[0].astype(BF16)
    x1 = _post(xf, o_a, o_b, 0, wo[:512], wo[512:], norm_mlp[0], mlp_w1[0].astype(BF16),
               mlp_w2[0].astype(BF16), norm_final, False, 1024, 1024, "post_even")

    o_m = _odd_heads(x1, batch, seq, bias_table, norm_mix[1], od_w_in[0], strips)
    wo = od_w_out[0].astype(BF16)
    out = _post(x1, o_m, o_m, 1, wo[:512], wo[512:], norm_mlp[1], mlp_w1[1].astype(BF16),
                mlp_w2[1].astype(BF16), norm_final, True, 1024, 1024, "post_odd")
    return out.reshape(batch, seq, d)
```

```python
import functools
import math

import numpy as np
import jax
import jax.numpy as jnp
from jax import lax
from jax.experimental import pallas as pl
from jax.experimental.pallas import tpu as pltpu

F32 = jnp.float32
BF16 = jnp.bfloat16

D_MODEL = 1024
HEAD_DIM = 64
EPS = 1e-6
NEG = -1e30
NUM_BUCKETS = 32
MAX_DISTANCE = 128
NSA_HEADS = 8
NSA_KV_HEADS = 2
NSA_GROUP = 4
CMP_BLOCK = 32
CMP_STRIDE = 16
SEL_BLOCK = 64
SEL_TOPN = 16
WINDOW = 512
FORCE_BONUS = 1e3
DIFF_HEADS = 4
MOBA_HEADS = 16
MOBA_BLOCK = 256
MOBA_TOPK = 3
D_FF = 4 * D_MODEL
SCALE = HEAD_DIM ** -0.5

LANES = 128
TQ = 256
TQD = 512
TK = 256
FAR_REL = 113
CMP_PER_TQ = TQ // CMP_STRIDE
CMP_NEAR = 128
CMP_PAD = CMP_NEAR - 2 * CMP_PER_TQ
GROUP = 4
SUB = 2 * TK
BAND_PAD = (GROUP + 1) * TK
FRONT = GROUP * TK
LOG2E = math.log2(math.e)
QSCALE = SCALE * LOG2E
VMEM_LIMIT = 56 * 1024 * 1024


def _attention_params():
    return pltpu.CompilerParams(dimension_semantics=("arbitrary", "arbitrary", "arbitrary"),
                                vmem_limit_bytes=VMEM_LIMIT)


def _dot_nt(a, b):
    return lax.dot_general(a, b, (((1,), (1,)), ((), ())), preferred_element_type=F32)


def _dot(a, b):
    return jnp.dot(a, b, preferred_element_type=F32)


def _rms(x, g):
    return x * lax.rsqrt(jnp.mean(x * x, axis=-1, keepdims=True) + EPS) * g


def _t5_bucket_np(rel):
    n = np.maximum(rel, 0)
    max_exact = NUM_BUCKETS // 2
    large = max_exact + (np.log(np.maximum(n, 1).astype(np.float32) / np.float32(max_exact))
                         / np.float32(math.log(MAX_DISTANCE / max_exact))
                         * np.float32(NUM_BUCKETS - max_exact)).astype(np.int32)
    large = np.minimum(large, NUM_BUCKETS - 1)
    return np.where(n < max_exact, n, large).astype(np.int32)


def _strip_kernel(tbl_ref, bucket_ref, valid_ref, o_ref, *, head0):
    h = pl.program_id(0) + head0
    bucket = bucket_ref[...]
    last = tbl_ref[NUM_BUCKETS - 1, h]
    acc = jnp.zeros(bucket.shape, F32)
    for b in range(NUM_BUCKETS - 1):
        acc = jnp.where(bucket == b, tbl_ref[b, h] - last, acc)
    o_ref[0] = jnp.where(valid_ref[...] != 0, acc * LOG2E, NEG)


def _toeplitz_kernel(tbl_ref, bucket_ref, valid_ref, o_ref, *, rows, cols):
    h = pl.program_id(0)
    bucket = bucket_ref[...]
    last = tbl_ref[NUM_BUCKETS - 1, h]
    acc = jnp.zeros(bucket.shape, F32)
    for b in range(NUM_BUCKETS - 1):
        acc = jnp.where(bucket == b, tbl_ref[b, h] - last, acc)
    line = jnp.where(valid_ref[...] != 0, acc * LOG2E, NEG)
    wide = jnp.broadcast_to(line[0:1], (rows, line.shape[1]))
    o_ref[0] = pltpu.roll(wide, 0, 1, stride=1, stride_axis=0)[:, :cols]


def _toeplitz_strips(table, rows, cols, off, lo, hi, nheads, name):
    width = -(-(rows + cols) // LANES) * LANES
    j = np.arange(width)
    rel = off - np.where(j < cols, j, j - width)
    valid = (rel >= lo) & (rel < hi)
    tile8 = lambda a: jnp.asarray(np.broadcast_to(a[None, :], (8, width)).astype(np.int32))
    return pl.pallas_call(
        functools.partial(_toeplitz_kernel, rows=rows, cols=cols),
        out_shape=jax.ShapeDtypeStruct((nheads, rows, cols), F32),
        grid=(nheads,),
        in_specs=[pl.BlockSpec(memory_space=pltpu.SMEM),
                  pl.BlockSpec((8, width), lambda h: (0, 0)),
                  pl.BlockSpec((8, width), lambda h: (0, 0))],
        out_specs=pl.BlockSpec((1, rows, cols), lambda h: (h, 0, 0)),
        name=name,
    )(table, tile8(_t5_bucket_np(rel)), tile8(valid))


def _bias_strips(table, rel, valid, head0, nheads, name):
    rows, cols = rel.shape
    bucket = jnp.asarray(_t5_bucket_np(rel))
    validi = jnp.asarray(valid.astype(np.int32))
    return pl.pallas_call(
        functools.partial(_strip_kernel, head0=head0),
        out_shape=jax.ShapeDtypeStruct((nheads, rows, cols), F32),
        grid=(nheads,),
        in_specs=[pl.BlockSpec(memory_space=pltpu.SMEM),
                  pl.BlockSpec((rows, cols), lambda h: (0, 0)),
                  pl.BlockSpec((rows, cols), lambda h: (0, 0))],
        out_specs=pl.BlockSpec((1, rows, cols), lambda h: (h, 0, 0)),
        name=name,
    )(table, bucket, validi)


def _norm_matmul_kernel(x_ref, g_ref, w_ref, o_ref, xn_ref):
    @pl.when(pl.program_id(1) == 0)
    def _():
        xn_ref[...] = _rms(x_ref[...], g_ref[...]).astype(BF16)

    o_ref[...] = _dot(xn_ref[...], w_ref[...]).astype(o_ref.dtype)


def _norm_matmul(x, g, w, out_dtype, tm, tn, name):
    m, d = x.shape
    n = w.shape[1]
    return pl.pallas_call(
        _norm_matmul_kernel,
        out_shape=jax.ShapeDtypeStruct((m, n), out_dtype),
        grid=(m // tm, n // tn),
        in_specs=[pl.BlockSpec((tm, d), lambda i, j: (i, 0)),
                  pl.BlockSpec((1, d), lambda i, j: (0, 0)),
                  pl.BlockSpec((d, tn), lambda i, j: (0, j))],
        out_specs=pl.BlockSpec((tm, tn), lambda i, j: (i, j)),
        scratch_shapes=[pltpu.VMEM((tm, d), BF16)],
        compiler_params=pltpu.CompilerParams(
            dimension_semantics=("arbitrary", "arbitrary"), vmem_limit_bytes=VMEM_LIMIT),
        name=name,
    )(x, g.reshape(1, d), w)


def _in_proj_even_kernel(x_ref, g_ref, w_ref, wa_ref, o_ref, kc0_ref, kc1_ref, vc0_ref, vc1_ref, gate_ref,
                         xn_ref):
    hd = HEAD_DIM

    @pl.when(pl.program_id(1) == 0)
    def _():
        xn = _rms(x_ref[...], g_ref[...]).astype(BF16)
        xn_ref[...] = xn
        aux = _dot(xn, wa_ref[...])
        kc0_ref[...] = aux[:, 0:hd]
        kc1_ref[...] = aux[:, hd:2 * hd]
        vc0_ref[...] = aux[:, 2 * hd:3 * hd]
        vc1_ref[...] = aux[:, 3 * hd:4 * hd]
        gate_ref[...] = aux[:, 4 * hd:]

    o_ref[...] = _dot(xn_ref[...], w_ref[...]).astype(o_ref.dtype)


def _in_proj_even(x, g, w_main, w_aux, tm, tn):
    m, d = x.shape
    n = w_main.shape[1]
    na = w_aux.shape[1]
    row_spec = lambda width: pl.BlockSpec((tm, width), lambda i, j: (i, 0))
    cmp_shape = jax.ShapeDtypeStruct((m, HEAD_DIM), F32)
    return pl.pallas_call(
        _in_proj_even_kernel,
        out_shape=(jax.ShapeDtypeStruct((m, n), BF16), cmp_shape, cmp_shape, cmp_shape, cmp_shape,
                   jax.ShapeDtypeStruct((m, na - 4 * HEAD_DIM), F32)),
        grid=(m // tm, n // tn),
        in_specs=[row_spec(d),
                  pl.BlockSpec((1, d), lambda i, j: (0, 0)),
                  pl.BlockSpec((d, tn), lambda i, j: (0, j)),
                  pl.BlockSpec((d, na), lambda i, j: (0, 0))],
        out_specs=(pl.BlockSpec((tm, tn), lambda i, j: (i, j)), row_spec(HEAD_DIM), row_spec(HEAD_DIM),
                   row_spec(HEAD_DIM), row_spec(HEAD_DIM), row_spec(na - 4 * HEAD_DIM)),
        scratch_shapes=[pltpu.VMEM((tm, d), BF16)],
        compiler_params=pltpu.CompilerParams(
            dimension_semantics=("arbitrary", "arbitrary"), vmem_limit_bytes=VMEM_LIMIT),
        name="in_proj_even",
    )(x, g.reshape(1, d), w_main, w_aux)


def _post_kernel(x_ref, a0_ref, a1_ref, wo0_ref, wo1_ref, g_ref, w1_ref, w2_ref, gf_ref,
                 o_ref, acc_ref, xn_ref, *, final_norm):
    f = pl.program_id(1)

    @pl.when(f == 0)
    def _():
        x1 = x_ref[...] + _dot(a0_ref[...], wo0_ref[...]) + _dot(a1_ref[...], wo1_ref[...])
        acc_ref[...] = x1
        xn_ref[...] = _rms(x1, g_ref[...]).astype(BF16)

    h = jnp.square(jnp.maximum(_dot(xn_ref[...], w1_ref[...]), 0.0))
    acc_ref[...] += _dot(h.astype(BF16), w2_ref[...])

    @pl.when(f == pl.num_programs(1) - 1)
    def _():
        y = acc_ref[...]
        if final_norm:
            y = _rms(y, gf_ref[...])
        o_ref[...] = y


def _post(x, a0, a1, a1_colblk, wo0, wo1, g, w1, w2, gf, final_norm, tm, tf, name):
    m, d = x.shape
    k0 = wo0.shape[0]
    k1 = wo1.shape[0]
    ff = w1.shape[1]
    return pl.pallas_call(
        functools.partial(_post_kernel, final_norm=final_norm),
        out_shape=jax.ShapeDtypeStruct((m, d), F32),
        grid=(m // tm, ff // tf),
        in_specs=[pl.BlockSpec((tm, d), lambda i, f: (i, 0)),
                  pl.BlockSpec((tm, k0), lambda i, f: (i, 0)),
                  pl.BlockSpec((tm, k1), lambda i, f: (i, a1_colblk)),
                  pl.BlockSpec((k0, d), lambda i, f: (0, 0)),
                  pl.BlockSpec((k1, d), lambda i, f: (0, 0)),
                  pl.BlockSpec((1, d), lambda i, f: (0, 0)),
                  pl.BlockSpec((d, tf), lambda i, f: (0, f)),
                  pl.BlockSpec((tf, d), lambda i, f: (f, 0)),
                  pl.BlockSpec((1, d), lambda i, f: (0, 0))],
        out_specs=pl.BlockSpec((tm, d), lambda i, f: (i, 0)),
        scratch_shapes=[pltpu.VMEM((tm, d), F32), pltpu.VMEM((tm, d), BF16)],
        compiler_params=pltpu.CompilerParams(
            dimension_semantics=("arbitrary", "arbitrary"), vmem_limit_bytes=VMEM_LIMIT),
        name=name,
    )(x, a0, a1, wo0, wo1, g.reshape(1, d), w1, w2, gf.reshape(1, d))


def _compress_kernel(kc0_ref, kc1_ref, vc0_ref, vc1_ref, pos_ref, w1_ref, w2_ref, o_ref):
    half = CMP_STRIDE * HEAD_DIM
    for kv, refs in enumerate(((kc0_ref, kc1_ref), (vc0_ref, vc1_ref))):
        w1a = w1_ref[kv, :half, :]
        w1b = w1_ref[kv, half:, :]
        pos = pos_ref[kv].astype(BF16)
        pa = jnp.broadcast_to(pos[0:1], (8, half))
        pb = jnp.broadcast_to(pos[1:2], (8, half))
        posterm = (_dot(pa, w1a) + _dot(pb, w1b))[0:1]
        for grp, x_ref in enumerate(refs):
            x = x_ref[0].astype(BF16)
            first = _dot(x, w1a)
            second = _dot(x, w1b)
            second = pltpu.roll(second, second.shape[0] - 1, 0)
            hid = jax.nn.gelu(first + second + posterm)
            o_ref[kv, grp] = _dot(hid.astype(BF16), w2_ref[kv])


def _compress(kc0, kc1, vc0, vc1, pos, w1, w2):
    batch, nch, wide = kc0.shape
    x_spec = pl.BlockSpec((1, nch, wide), lambda b: (b, 0, 0))
    full = lambda a: pl.BlockSpec(a.shape, lambda b: (0,) * a.ndim)
    return pl.pallas_call(
        _compress_kernel,
        out_shape=jax.ShapeDtypeStruct((2, batch * NSA_KV_HEADS, nch, HEAD_DIM), F32),
        grid=(batch,),
        in_specs=[x_spec, x_spec, x_spec, x_spec, full(pos), full(w1), full(w2)],
        out_specs=pl.BlockSpec((2, NSA_KV_HEADS, nch, HEAD_DIM), lambda b: (0, b, 0, 0)),
        compiler_params=pltpu.CompilerParams(
            dimension_semantics=("arbitrary",), vmem_limit_bytes=VMEM_LIMIT),
        name="nsa_compress",
    )(kc0, kc1, vc0, vc1, pos, w1, w2)


def _softmax_part(m, s):
    m_new = jnp.maximum(m, jnp.max(s, axis=1, keepdims=True))
    return m_new, jnp.exp2(m - m_new), jnp.exp2(s - m_new).astype(BF16)


def _flash_chain(q, k_sc, v_sc, bufs, far0, n_full, tail, side_work=None):
    s_sc, p_sc, acc_sc = bufs
    rows = q.shape[0]
    assert GROUP * TK == 2 * SUB and tail[0][1] == SUB
    tail0 = tail[0][0]

    def qk(r0):
        return _dot_nt(q, k_sc[pl.ds(pl.multiple_of(r0, TK), SUB), :])

    s_sc[...] = qk(jnp.where(n_full > 0, far0, tail0))
    p_sc[...] = jnp.zeros(p_sc.shape, BF16)
    acc_sc[...] = jnp.zeros(acc_sc.shape, F32)

    n_long = n_full // 2
    n_short = n_full - 2 * n_long
    short0 = far0 + n_long * (2 * GROUP * TK)

    def trips(start, n_sub, n_trips, after):
        def body(j, m):
            base = pl.multiple_of(start + j * (n_sub * SUB), SUB)
            acc = acc_sc[...] + _dot(p_sc[...], v_sc[pl.ds(base - SUB, SUB), :])
            s_cur = s_sc[...]
            for u in range(n_sub):
                m, alpha, p = _softmax_part(m, s_cur)
                if u + 1 < n_sub:
                    s_cur = qk(base + (u + 1) * SUB)
                    acc = alpha * acc + _dot(p, v_sc[pl.ds(base + u * SUB, SUB), :])
                else:
                    s_sc[...] = qk(jnp.where(j + 1 < n_trips, base + n_sub * SUB, after))
                    p_sc[...] = p
                    acc_sc[...] = alpha * acc
            return m
        return body

    m = jnp.full((rows, 1), NEG, F32)
    m = lax.fori_loop(0, n_long, trips(far0, 4, n_long, jnp.where(n_short > 0, short0, tail0)), m)
    m = lax.fori_loop(0, n_short, trips(short0, 2, n_short, tail0), m)
    if side_work is not None:
        side_work()
    base = pl.multiple_of(far0 + n_full * (GROUP * TK), SUB)
    acc = acc_sc[...] + _dot(p_sc[...], v_sc[pl.ds(base - SUB, SUB), :])
    m, alpha, p = _softmax_part(m, s_sc[...] + tail[0][2]())
    acc = alpha * acc + _dot(p, v_sc[pl.ds(tail0, SUB), :])
    return _flash_tail((m, acc), q, k_sc, v_sc, tail[1:])


def _flash_buffers(rows, ncols):
    return [pltpu.VMEM((rows, SUB), F32), pltpu.VMEM((rows, SUB), BF16), pltpu.VMEM((rows, ncols), F32)]


def _flash_tail(carry, q, k_sc, v_sc, tail):
    m, acc = carry
    for r0, width, extra in tail:
        m, alpha, p = _softmax_part(m, _dot_nt(q, k_sc[pl.ds(r0, width), :]) + extra())
        acc = alpha * acc + _dot(p, v_sc[pl.ds(r0, width), :])
    return acc


def _tile_gate(cond):
    return jnp.where(cond, 0.0, NEG)


def _gate_pair(cond_a, cond_b):
    lane = lax.broadcasted_iota(jnp.int32, (1, 2 * TK), 1)
    return jnp.where(lane < TK, _tile_gate(cond_a), _tile_gate(cond_b))


def _tail_steps(win0, t0, done, rows, strip, near_ok):
    def mid():
        return jnp.concatenate([jnp.broadcast_to(_tile_gate(t0 + 2 >= done), (rows, TK)),
                                strip(0, TK) + _tile_gate(near_ok)], axis=1)

    return [(win0, SUB, lambda: _gate_pair(t0 >= done, t0 + 1 >= done)),
            (win0 + SUB, SUB, mid),
            (win0 + 2 * SUB, SUB, lambda: strip(TK, 3 * TK))]


def _topk_rows(score, idx, k, floor):
    sel = jnp.zeros(score.shape, F32)
    nrow = score.shape[0]
    for _ in range(k):
        mx = jnp.max(score, axis=0, keepdims=True)
        first = jnp.min(jnp.where(score == mx, idx, nrow), axis=0, keepdims=True)
        pick = idx == first
        sel = jnp.where(pick, 1.0, sel)
        score = jnp.where(pick, floor, score)
    return sel


def _nsa_kernel(q_ref, qn_ref, ks_ref, vs_ref, kw_ref, vw_ref, kc_ref, vc_ref, gate_ref,
                ssel_ref, swin_ref, scmp_ref, ov_ref, o_ref,
                kaug, vaug, kwp, vwp, kcp, vcp, qaug_sc, ocw_sc, *bufs, seq):
    g = pl.program_id(1)
    i = pl.program_id(2)
    ncp = seq // CMP_STRIDE
    hd = HEAD_DIM
    rows = NSA_GROUP * TQ

    @pl.when(i == 0)
    def _():
        lane = lax.broadcasted_iota(jnp.int32, (BAND_PAD, LANES), 1)
        padflag = jnp.where(lane == hd, 1.0, 0.0).astype(BF16)
        kaug[0:BAND_PAD, 0:LANES] = jnp.zeros((BAND_PAD, LANES), BF16)
        kaug[0:BAND_PAD, LANES:2 * LANES] = padflag
        kwp[0:BAND_PAD, :] = padflag
        vaug[0:BAND_PAD, :] = jnp.zeros((BAND_PAD, LANES), BF16)
        vwp[0:BAND_PAD, :] = jnp.zeros((BAND_PAD, LANES), BF16)
        onescol = jnp.where(lax.broadcasted_iota(jnp.int32, (TK, LANES), 1) == hd, 1.0, 0.0)
        zero_hd = jnp.zeros((TK, hd), F32)
        col = lax.broadcasted_iota(jnp.int32, (TK, LANES), 1)
        row = lax.broadcasted_iota(jnp.int32, (TK, LANES), 0)

        def fill(c, _):
            src = pl.multiple_of(c * TK, TK)
            dst = pl.multiple_of(BAND_PAD + c * TK, TK)

            def pick(ref):
                both = ref[pl.ds(src, TK), :].astype(F32)
                return jnp.concatenate([jnp.where(g == 0, both[:, :hd], both[:, hd:]), zero_hd], axis=1)

            kaug[pl.ds(dst, TK), 0:LANES] = jnp.where((src + row) // SEL_BLOCK == col, 1.0, 0.0).astype(BF16)
            kaug[pl.ds(dst, TK), LANES:2 * LANES] = pick(ks_ref).astype(BF16)
            kwp[pl.ds(dst, TK), :] = pick(kw_ref).astype(BF16)
            vaug[pl.ds(dst, TK), :] = (pick(vs_ref) + onescol).astype(BF16)
            vwp[pl.ds(dst, TK), :] = (pick(vw_ref) + onescol).astype(BF16)
            return 0

        lax.fori_loop(0, seq // TK, fill, 0)
        lane_c = lax.broadcasted_iota(jnp.int32, (CMP_PAD, LANES), 1)
        kcp[0:CMP_PAD, :] = jnp.where(lane_c == hd, 1.0, 0.0).astype(BF16)
        vcp[0:CMP_PAD, :] = jnp.zeros((CMP_PAD, LANES), BF16)
        zc = jnp.zeros((ncp, hd), F32)
        onesc = jnp.where(lax.broadcasted_iota(jnp.int32, (ncp, LANES), 1) == hd, 1.0, 0.0)
        kcp[CMP_PAD:CMP_PAD + ncp, :] = jnp.concatenate([kc_ref[0, 0], zc], axis=1).astype(BF16)
        vcp[CMP_PAD:CMP_PAD + ncp, :] = (jnp.concatenate([vc_ref[0, 0], zc], axis=1) + onesc).astype(BF16)
        tail = kcp.shape[0] - CMP_PAD - ncp
        kcp[CMP_PAD + ncp:, :] = jnp.zeros((tail, LANES), BF16)
        vcp[CMP_PAD + ncp:, :] = jnp.zeros((tail, LANES), BF16)

    def stack(ref, lo, hi):
        return jnp.concatenate([ref[r, :, lo:hi] for r in range(NSA_GROUP)], axis=0)

    def prepare(qsrc_ref, t, slot):
        qblk = qsrc_ref[...].astype(F32) * QSCALE
        lane_t = lax.broadcasted_iota(jnp.int32, (TQ, hd), 1)
        qtail = jnp.where(lane_t == 0, NEG, 0.0)
        qs = [qblk[:, r * hd:(r + 1) * hd] for r in range(NSA_GROUP)]
        q128 = jnp.concatenate([jnp.concatenate([qs[r], qtail], axis=1) for r in range(NSA_GROUP)],
                               axis=0).astype(BF16)

        s_far = _dot_nt(q128, kcp[CMP_PAD:CMP_PAD + ncp, :])
        c_idx = lax.broadcasted_iota(jnp.int32, (1, ncp), 1)
        far_ok = c_idx < CMP_PER_TQ * t - CMP_PAD
        near0 = pl.multiple_of(CMP_PER_TQ * t, CMP_PER_TQ)
        s_near = _dot_nt(q128, kcp[pl.ds(near0, CMP_NEAR), :]) + stack(scmp_ref, 0, CMP_NEAR)
        near_ok = s_near > 0.5 * NEG
        s_far = jnp.where(far_ok, s_far, NEG)
        m_c = jnp.maximum(jnp.max(s_far, axis=1, keepdims=True), jnp.max(s_near, axis=1, keepdims=True))
        p_far = jnp.where(far_ok, jnp.exp2(s_far - m_c), 0.0)
        p_near = jnp.where(near_ok, jnp.exp2(s_near - m_c), 0.0)
        den = jnp.maximum(jnp.sum(p_far, axis=1, keepdims=True) + jnp.sum(p_near, axis=1, keepdims=True), 1e-30)
        pc_far = p_far / den
        pc_near = p_near / den
        ocw_sc[slot, 0] = (_dot(pc_far.astype(BF16), vcp[CMP_PAD:CMP_PAD + ncp, :])
                           + _dot(pc_near.astype(BF16), vcp[pl.ds(near0, CMP_NEAR), :]))

        band0 = pl.multiple_of(TQ * t, TK) + 3 * TK
        steps = [(band0, 2 * TK, lambda: stack(swin_ref, 0, 2 * TK)),
                 (band0 + 2 * TK, TK, lambda: stack(swin_ref, 2 * TK, 3 * TK))]
        acc = _flash_tail((jnp.full((rows, 1), NEG, F32), jnp.zeros((rows, LANES), F32)), q128, kwp, vwp, steps)
        ocw_sc[slot, 1] = acc / acc[:, hd:hd + 1]

        pcs_far = sum(pc_far[r * TQ:(r + 1) * TQ] for r in range(NSA_GROUP))
        pcs_near = sum(pc_near[r * TQ:(r + 1) * TQ] for r in range(NSA_GROUP))

        def hi_lo_dot(p, w):
            hi = p.astype(BF16)
            lo = (p - hi.astype(F32)).astype(BF16)
            return _dot(hi, w) + _dot(lo, w)

        imp = (hi_lo_dot(pcs_far, ov_ref[CMP_PAD:CMP_PAD + ncp, :])
               + hi_lo_dot(pcs_near, ov_ref[pl.ds(near0, CMP_NEAR), :]))
        imp_t = imp.T
        n_idx = lax.broadcasted_iota(jnp.int32, (LANES, TQ), 0)
        cur = (TQ * t + lax.broadcasted_iota(jnp.int32, (LANES, TQ), 1)) // SEL_BLOCK
        eligible = n_idx <= cur
        forced = ((n_idx == 0) | (n_idx == cur) | (n_idx == cur - 1)) & eligible
        rest = jnp.where(forced, -3.0e38, jnp.where(eligible, imp_t, -1.0))
        sel_t = _topk_rows(rest, n_idx, SEL_TOPN - 3, -3.0e38)
        maskneg = jnp.where(forced | (sel_t > 0.5), 0.0, NEG).T
        qaug_sc[slot] = jnp.concatenate(
            [jnp.concatenate([maskneg, qs[r], qtail], axis=1) for r in range(NSA_GROUP)],
            axis=0).astype(BF16)

    @pl.when(i == 0)
    def _():
        prepare(q_ref, 0, 0)

    slot = i % 2
    n_full = jnp.maximum(i - 2, 0) // GROUP
    win0 = pl.multiple_of(TQ * i, TK)
    tail = _tail_steps(win0, i - 5, GROUP * n_full, rows, functools.partial(stack, ssel_ref), True)
    prepare_next = functools.partial(prepare, qn_ref, jnp.minimum(i + 1, pl.num_programs(2) - 1), 1 - slot)
    acc = _flash_chain(qaug_sc[slot], kaug, vaug, bufs, BAND_PAD, n_full, tail, prepare_next)
    o_sel = acc[:, :hd] / acc[:, hd:hd + 1]

    o_cmp = ocw_sc[slot, 0]
    o_win = ocw_sc[slot, 1]
    sig = jax.nn.sigmoid(gate_ref[...])
    outs = []
    for r in range(NSA_GROUP):
        def gcol(branch):
            c0 = branch * NSA_HEADS + r
            c1 = c0 + NSA_GROUP
            return jnp.where(g == 0, sig[:, c0:c0 + 1], sig[:, c1:c1 + 1])
        sl = slice(r * TQ, (r + 1) * TQ)
        outs.append(gcol(0) * o_cmp[sl, :hd] + gcol(1) * o_sel[sl] + gcol(2) * o_win[sl, :hd])
    o_ref[...] = jnp.concatenate(outs, axis=1).astype(o_ref.dtype)


def _nsa(main, aux, ckv, ssel, swin, scmp, ov, batch, seq):
    nq = seq // TQ
    ncp = seq // CMP_STRIDE
    gw = NSA_GROUP * HEAD_DIM
    once = pl.Buffered(1)
    kv_spec = lambda colblk: pl.BlockSpec((seq, LANES), lambda b, g, i: (b, colblk), pipeline_mode=once)
    strip_spec = lambda cols: pl.BlockSpec((NSA_GROUP, TQ, cols), lambda b, g, i: (g, 0, 0), pipeline_mode=once)
    return pl.pallas_call(
        functools.partial(_nsa_kernel, seq=seq),
        out_shape=jax.ShapeDtypeStruct((batch * seq, NSA_HEADS * HEAD_DIM), BF16),
        grid=(batch, NSA_KV_HEADS, nq),
        in_specs=[pl.BlockSpec((TQ, gw), lambda b, g, i: (b * nq + i, g)),
                  pl.BlockSpec((TQ, gw), lambda b, g, i: (b * nq + jnp.minimum(i + 1, nq - 1), g)),
                  kv_spec(4), kv_spec(5), kv_spec(6), kv_spec(7),
                  pl.BlockSpec((1, 1, ncp, HEAD_DIM), lambda b, g, i: (0, b * NSA_KV_HEADS + g, 0, 0)),
                  pl.BlockSpec((1, 1, ncp, HEAD_DIM), lambda b, g, i: (1, b * NSA_KV_HEADS + g, 0, 0)),
                  pl.BlockSpec((TQ, LANES), lambda b, g, i: (b * nq + i, 0)),
                  strip_spec(3 * TK), strip_spec(3 * TK), strip_spec(CMP_NEAR),
                  pl.BlockSpec(ov.shape, lambda b, g, i: (0, 0))],
        out_specs=pl.BlockSpec((TQ, gw), lambda b, g, i: (b * nq + i, g)),
        scratch_shapes=[pltpu.VMEM((BAND_PAD + seq, 2 * LANES), BF16),
                        pltpu.VMEM((BAND_PAD + seq, LANES), BF16),
                        pltpu.VMEM((BAND_PAD + seq, LANES), BF16),
                        pltpu.VMEM((BAND_PAD + seq, LANES), BF16),
                        pltpu.VMEM((ov.shape[0], LANES), BF16),
                        pltpu.VMEM((ov.shape[0], LANES), BF16),
                        pltpu.VMEM((2, NSA_GROUP * TQ, 2 * LANES), BF16),
                        pltpu.VMEM((2, 2, NSA_GROUP * TQ, LANES), F32),
                        *_flash_buffers(NSA_GROUP * TQ, LANES)],
        compiler_params=_attention_params(),
        name="nsa_attention",
    )(main, main, main, main, main, main, ckv, ckv, aux, ssel, swin, scmp, ov)


def _ones_column():
    return jnp.where(lax.broadcasted_iota(jnp.int32, (TK, LANES), 1) == 0, 1.0, 0.0).astype(BF16)


def _causal_flash(q, k_sc, v_sc, bufs, strip, i):
    n_full = jnp.maximum(2 * i - 1, 0) // GROUP
    win0 = pl.multiple_of(i * TQD, SUB)
    tail = _tail_steps(win0, 2 * i - 4, GROUP * n_full, q.shape[0], strip, i >= 1)
    return _flash_chain(q, k_sc, v_sc, bufs, FRONT, n_full, tail)


def _diff_kernel(q_ref, k_ref, v_ref, lam_ref, sub_ref, strip_ref, o_ref, k_sc, v_sc, *bufs, lam_init, seq):
    i = pl.program_id(2)
    hd = HEAD_DIM

    @pl.when(i == 0)
    def _():
        k_sc[0:FRONT, :] = jnp.zeros((FRONT, k_sc.shape[1]), BF16)
        v_sc[0:FRONT, :] = jnp.zeros((FRONT, v_sc.shape[1]), BF16)
        ones = _ones_column()

        def fill(c, _):
            r0 = pl.multiple_of(c * TK, TK)
            k_sc[pl.ds(FRONT + r0, TK), :] = k_ref[pl.ds(r0, TK), :]
            v_sc[pl.ds(FRONT + r0, TK), 0:LANES] = v_ref[pl.ds(r0, TK), :]
            v_sc[pl.ds(FRONT + r0, TK), LANES:2 * LANES] = ones
            return 0

        lax.fori_loop(0, seq // TK, fill, 0)

    q = q_ref[...].astype(F32) * QSCALE
    lane = lax.broadcasted_iota(jnp.int32, (TQD, 2 * hd), 1)
    qst = jnp.concatenate([jnp.where(lane < hd, q, 0.0), jnp.where(lane < hd, 0.0, q)],
                          axis=0).astype(BF16)
    strip = lambda lo, hi: jnp.concatenate([strip_ref[0, :, lo:hi]] * 2, axis=0)
    acc = _causal_flash(qst, k_sc, v_sc, bufs, strip, i)
    o = acc[:, :2 * hd] / acc[:, 2 * hd:2 * hd + 1]
    lam_p = lam_ref[...]
    lam = (jnp.exp(jnp.sum(lam_p[0:1] * lam_p[1:2], axis=1, keepdims=True))
           - jnp.exp(jnp.sum(lam_p[2:3] * lam_p[3:4], axis=1, keepdims=True)) + lam_init)
    d = o[:TQD] - lam * o[TQD:]
    o_ref[...] = (_rms(d, sub_ref[...]) * (1.0 - lam_init)).astype(o_ref.dtype)


def _diff(main, lam_p, subln, strips, batch, seq, lam_init):
    nq = seq // TQD
    blk = 2 * HEAD_DIM
    q0, k0, v0 = 1024 // blk, 1536 // blk, 2048 // blk
    return pl.pallas_call(
        functools.partial(_diff_kernel, lam_init=lam_init, seq=seq),
        out_shape=jax.ShapeDtypeStruct((batch * seq, DIFF_HEADS * blk), BF16),
        grid=(batch, DIFF_HEADS, nq),
        in_specs=[pl.BlockSpec((TQD, blk), lambda b, h, i: (b * nq + i, q0 + h)),
                  pl.BlockSpec((seq, blk), lambda b, h, i: (b, k0 + h)),
                  pl.BlockSpec((seq, blk), lambda b, h, i: (b, v0 + h)),
                  pl.BlockSpec((4, HEAD_DIM), lambda b, h, i: (0, 0)),
                  pl.BlockSpec((1, blk), lambda b, h, i: (0, 0)),
                  pl.BlockSpec((1, TQD, 3 * TK), lambda b, h, i: (NSA_HEADS + h, 0, 0))],
        out_specs=pl.BlockSpec((TQD, blk), lambda b, h, i: (b * nq + i, h)),
        scratch_shapes=[pltpu.VMEM((FRONT + seq, LANES), BF16), pltpu.VMEM((FRONT + seq, 2 * LANES), BF16),
                        *_flash_buffers(2 * TQD, 2 * LANES)],
        compiler_params=_attention_params(),
        name="diff_attention",
    )(main, main, main, lam_p, subln.reshape(1, blk), strips)


def _moba_kernel(q_ref, k_ref, v_ref, strip_ref, o_ref, kaug, vaug, kmean, *bufs, seq):
    i = pl.program_id(2)
    hd = HEAD_DIM
    nb = seq // MOBA_BLOCK

    @pl.when(i == 0)
    def _():
        kmean[...] = jnp.zeros(kmean.shape, F32)
        kaug[0:FRONT, :] = jnp.zeros((FRONT, kaug.shape[1]), BF16)
        vaug[0:FRONT, :] = jnp.zeros((FRONT, vaug.shape[1]), BF16)
        col = lax.broadcasted_iota(jnp.int32, (MOBA_BLOCK, LANES), 1)
        ones = _ones_column()

        def fill(n, _):
            r0 = pl.multiple_of(n * MOBA_BLOCK, MOBA_BLOCK)
            kb = k_ref[pl.ds(r0, MOBA_BLOCK), :]
            kaug[pl.ds(FRONT + r0, MOBA_BLOCK), 0:LANES] = jnp.where(col == n, 1.0, 0.0).astype(BF16)
            kaug[pl.ds(FRONT + r0, MOBA_BLOCK), LANES:2 * LANES] = kb
            vaug[pl.ds(FRONT + r0, MOBA_BLOCK), 0:LANES] = v_ref[pl.ds(r0, MOBA_BLOCK), :]
            vaug[pl.ds(FRONT + r0, MOBA_BLOCK), LANES:2 * LANES] = ones
            kmean[pl.ds(n, 1), :] = jnp.mean(kb.astype(F32), axis=0, keepdims=True)
            return 0

        lax.fori_loop(0, nb, fill, 0)

    qblk = q_ref[...]
    qf = qblk.astype(F32) * QSCALE
    lane = lax.broadcasted_iota(jnp.int32, (TQD, LANES), 1)
    n_idx = lax.broadcasted_iota(jnp.int32, (nb, TQD), 0)
    own = (i * TQD + lax.broadcasted_iota(jnp.int32, (nb, TQD), 1)) // MOBA_BLOCK
    past = n_idx < own
    km = kmean[0:nb, :]
    lane_k = lax.broadcasted_iota(jnp.int32, km.shape, 1)
    no_block = jnp.full((LANES - nb, TQD), NEG, F32)
    qrows = []
    for hh in range(2):
        mine = (lane < hd) if hh == 0 else (lane >= hd)
        mine_k = (lane_k < hd) if hh == 0 else (lane_k >= hd)
        gate_t = _dot_nt(jnp.where(mine_k, km, 0.0).astype(BF16), qblk)
        sel_t = _topk_rows(jnp.where(past, gate_t, NEG), n_idx, MOBA_TOPK, -3.0e38)
        allowed = ((sel_t > 0.5) & past) | (n_idx == own)
        maskneg = jnp.concatenate([jnp.where(allowed, 0.0, NEG), no_block], axis=0).T
        qrows.append(jnp.concatenate([maskneg, jnp.where(mine, qf, 0.0)], axis=1))
    qaug = jnp.concatenate(qrows, axis=0).astype(BF16)
    strip = lambda lo, hi: jnp.concatenate([strip_ref[0, :, lo:hi], strip_ref[1, :, lo:hi]], axis=0)
    acc = _causal_flash(qaug, kaug, vaug, bufs, strip, i)
    o = acc[:, :LANES] / acc[:, LANES:LANES + 1]
    o_ref[...] = jnp.where(lane < hd, o[:TQD], o[TQD:]).astype(o_ref.dtype)


def _moba(main, strips, batch, seq):
    nq = seq // TQD
    pairs = MOBA_HEADS // 2
    return pl.pallas_call(
        functools.partial(_moba_kernel, seq=seq),
        out_shape=jax.ShapeDtypeStruct((batch * seq, MOBA_HEADS * HEAD_DIM), BF16),
        grid=(batch, pairs, nq),
        in_specs=[pl.BlockSpec((TQD, LANES), lambda b, h, i: (b * nq + i, h)),
                  pl.BlockSpec((seq, LANES), lambda b, h, i: (b, pairs + h)),
                  pl.BlockSpec((seq, LANES), lambda b, h, i: (b, 2 * pairs + h)),
                  pl.BlockSpec((2, TQD, 3 * TK), lambda b, h, i: (h, 0, 0))],
        out_specs=pl.BlockSpec((TQD, LANES), lambda b, h, i: (b * nq + i, h)),
        scratch_shapes=[pltpu.VMEM((FRONT + seq, 2 * LANES), BF16), pltpu.VMEM((FRONT + seq, 2 * LANES), BF16),
                        pltpu.VMEM((LANES, LANES), F32), *_flash_buffers(2 * TQD, 2 * LANES)],
        compiler_params=_attention_params(),
        name="moba_attention",
    )(main, main, main, strips)


def _cmp_strip_geometry():
    r = np.arange(TQ)[:, None]
    cc = np.arange(CMP_NEAR)[None, :]
    rel_c = r + CMP_STRIDE * CMP_PAD - (CMP_BLOCK - 1) - CMP_STRIDE * cc
    return rel_c, rel_c >= 0


def _overlap_padded(seq):
    nc = (seq - CMP_BLOCK) // CMP_STRIDE + 1
    ncp = seq // CMP_STRIDE
    tok = np.arange(nc)[:, None] * CMP_STRIDE + np.arange(CMP_BLOCK)[None, :]
    ov = np.zeros((CMP_PAD + ncp + 2 * CMP_PER_TQ, LANES), np.float32)
    np.add.at(ov, (CMP_PAD + np.repeat(np.arange(nc), CMP_BLOCK), (tok // SEL_BLOCK).reshape(-1)),
              1.0 / CMP_BLOCK)
    return jnp.asarray(ov, dtype=BF16)


def _strips(bias_table):
    rel_c, valid_c = _cmp_strip_geometry()
    far = 1 << 30
    return (_toeplitz_strips(bias_table, TQD, 3 * TK, TK, 0, far, MOBA_HEADS, "bias_strip_diag"),
            _toeplitz_strips(bias_table, TQ, 3 * TK, 2 * TK, 0, far, NSA_HEADS, "bias_strip_sel"),
            _toeplitz_strips(bias_table, TQ, 3 * TK, 2 * TK, 0, WINDOW, NSA_HEADS, "bias_strip_win"),
            _bias_strips(bias_table, rel_c, valid_c, 0, NSA_HEADS, "bias_strip_cmp"))


def _even_heads(xf, batch, seq, bias_table, g_mix, w_in, pos_k, w1_k, w2_k, pos_v, w1_v, w2_v,
                lam_q1, lam_k1, lam_q2, lam_k2, subln, strips=None):
    d = xf.shape[1]
    strip_a, strip_sel, strip_win, strip_cmp = strips if strips is not None else _strips(bias_table)
    w_main = jnp.concatenate([w_in[:, 0:512], w_in[:, 768:1280], w_in[:, 1304:2840]], axis=1).astype(BF16)
    w_aux = jnp.concatenate([w_in[:, 512:768], w_in[:, 1280:1304],
                             jnp.zeros((d, 3 * LANES - 280), F32)], axis=1).astype(BF16)
    main0, kc0, kc1, vc0, vc1, gates = _in_proj_even(xf, g_mix, w_main, w_aux, 1024, w_main.shape[1] // 2)

    nch = seq // CMP_STRIDE
    chunks = lambda a: a.reshape(batch, nch, CMP_STRIDE * HEAD_DIM)
    pos = jnp.stack([pos_k, pos_v]).reshape(2, 2, CMP_STRIDE * HEAD_DIM)
    ckv = _compress(chunks(kc0), chunks(kc1), chunks(vc0), chunks(vc1), pos,
                    jnp.stack([w1_k, w1_v]).astype(BF16), jnp.stack([w2_k, w2_v]).astype(BF16))

    o_a = _nsa(main0, gates, ckv, strip_sel, strip_win, strip_cmp, _overlap_padded(seq), batch, seq)
    lam_p = jnp.stack([lam_q1, lam_k1, lam_q2, lam_k2])
    lam_init = 0.8 - 0.6 * math.exp(-0.3 * 0)
    o_b = _diff(main0, lam_p, subln, strip_a, batch, seq, lam_init)
    return o_a, o_b, ckv


def _odd_heads(xf, batch, seq, bias_table, g_mix, w_in, strips=None):
    strip_a = (strips if strips is not None else _strips(bias_table))[0]
    main1 = _norm_matmul(xf, g_mix, w_in.astype(BF16), BF16, 1024, w_in.shape[1] // 2, "in_proj_odd")
    return _moba(main1, strip_a, batch, seq)


def kernel(x, bias_table, norm_mix, norm_mlp, norm_final, mlp_w1, mlp_w2, ev_w_in, ev_w_out,
           ev_cmp_pos_k, ev_cmp_w1_k, ev_cmp_w2_k, ev_cmp_pos_v, ev_cmp_w1_v, ev_cmp_w2_v,
           ev_lam_q1, ev_lam_k1, ev_lam_q2, ev_lam_k2, ev_subln, od_w_in, od_w_out):
    batch, seq, d = x.shape
    assert d == D_MODEL and (batch * seq) % 1024 == 0 and seq % TQD == 0
    assert SEL_TOPN * SEL_BLOCK <= seq <= LANES * SEL_BLOCK and seq >= MOBA_TOPK * MOBA_BLOCK
    xf = x.reshape(batch * seq, d)
    strips = _strips(bias_table)

    o_a, o_b, _ = _even_heads(xf, batch, seq, bias_table, norm_mix[0], ev_w_in[0],
                              ev_cmp_pos_k[0], ev_cmp_w1_k[0], ev_cmp_w2_k[0],
                              ev_cmp_pos_v[0], ev_cmp_w1_v[0], ev_cmp_w2_v[0],
                              ev_lam_q1[0], ev_lam_k1[0], ev_lam_q2[0], ev_lam_k2[0], ev_subln[0], strips)
    wo = ev_w_out[0].astype(BF16)
    x1 = _post(xf, o_a, o_b, 0, wo[:512], wo[512:], norm_mlp[0], mlp_w1[0].astype(BF16),
               mlp_w2[0].astype(BF16), norm_final, False, 1024, 1024, "post_even")

    o_m = _odd_heads(x1, batch, seq, bias_table, norm_mix[1], od_w_in[0], strips)
    wo = od_w_out[0].astype(BF16)
    out = _post(x1, o_m, o_m, 1, wo[:512], wo[512:], norm_mlp[1], mlp_w1[1].astype(BF16),
                mlp_w2[1].astype(BF16), norm_final, True, 1024, 1024, "post_odd")
    return out.reshape(batch, seq, d)
```

```python
import functools
import math

import numpy as np
import jax
import jax.numpy as jnp
from jax import lax
from jax.experimental import pallas as pl
from jax.experimental.pallas import tpu as pltpu

F32 = jnp.float32
BF16 = jnp.bfloat16

D_MODEL = 1024
HEAD_DIM = 64
EPS = 1e-6
NEG = -1e30
NUM_BUCKETS = 32
MAX_DISTANCE = 128
NSA_HEADS = 8
NSA_KV_HEADS = 2
NSA_GROUP = 4
CMP_BLOCK = 32
CMP_STRIDE = 16
SEL_BLOCK = 64
SEL_TOPN = 16
WINDOW = 512
FORCE_BONUS = 1e3
DIFF_HEADS = 4
MOBA_HEADS = 16
MOBA_BLOCK = 256
MOBA_TOPK = 3
D_FF = 4 * D_MODEL
SCALE = HEAD_DIM ** -0.5

LANES = 128
TQ = 256
TQD = 512
TK = 256
FAR_REL = 113
CMP_PER_TQ = TQ // CMP_STRIDE
CMP_NEAR = 128
CMP_PAD = CMP_NEAR - 2 * CMP_PER_TQ
SUB = 2 * TK
BAND_PAD = 3 * TK
FRONT = 2 * TK
LOG2E = math.log2(math.e)
QSCALE = SCALE * LOG2E
VMEM_LIMIT = 56 * 1024 * 1024


def _attention_params():
    return pltpu.CompilerParams(dimension_semantics=("arbitrary", "arbitrary", "arbitrary"),
                                vmem_limit_bytes=VMEM_LIMIT)


def _dot_nt(a, b):
    return lax.dot_general(a, b, (((1,), (1,)), ((), ())), preferred_element_type=F32)


def _dot(a, b):
    return jnp.dot(a, b, preferred_element_type=F32)


def _rms(x, g):
    return x * lax.rsqrt(jnp.mean(x * x, axis=-1, keepdims=True) + EPS) * g


def _t5_bucket_np(rel):
    n = np.maximum(rel, 0)
    max_exact = NUM_BUCKETS // 2
    large = max_exact + (np.log(np.maximum(n, 1).astype(np.float32) / np.float32(max_exact))
                         / np.float32(math.log(MAX_DISTANCE / max_exact))
                         * np.float32(NUM_BUCKETS - max_exact)).astype(np.int32)
    large = np.minimum(large, NUM_BUCKETS - 1)
    return np.where(n < max_exact, n, large).astype(np.int32)


def _strip_kernel(tbl_ref, bucket_ref, valid_ref, o_ref, *, head0):
    h = pl.program_id(0) + head0
    bucket = bucket_ref[...]
    last = tbl_ref[NUM_BUCKETS - 1, h]
    acc = jnp.zeros(bucket.shape, F32)
    for b in range(NUM_BUCKETS - 1):
        acc = jnp.where(bucket == b, tbl_ref[b, h] - last, acc)
    o_ref[0] = jnp.where(valid_ref[...] != 0, acc * LOG2E, NEG)


def _toeplitz_kernel(tbl_ref, bucket_ref, valid_ref, o_ref, *, rows, cols):
    h = pl.program_id(0)
    bucket = bucket_ref[...]
    last = tbl_ref[NUM_BUCKETS - 1, h]
    acc = jnp.zeros(bucket.shape, F32)
    for b in range(NUM_BUCKETS - 1):
        acc = jnp.where(bucket == b, tbl_ref[b, h] - last, acc)
    line = jnp.where(valid_ref[...] != 0, acc * LOG2E, NEG)
    wide = jnp.broadcast_to(line[0:1], (rows, line.shape[1]))
    o_ref[0] = pltpu.roll(wide, 0, 1, stride=1, stride_axis=0)[:, :cols]


def _toeplitz_strips(table, rows, cols, off, lo, hi, nheads, name):
    width = -(-(rows + cols) // LANES) * LANES
    j = np.arange(width)
    rel = off - np.where(j < cols, j, j - width)
    valid = (rel >= lo) & (rel < hi)
    tile8 = lambda a: jnp.asarray(np.broadcast_to(a[None, :], (8, width)).astype(np.int32))
    return pl.pallas_call(
        functools.partial(_toeplitz_kernel, rows=rows, cols=cols),
        out_shape=jax.ShapeDtypeStruct((nheads, rows, cols), F32),
        grid=(nheads,),
        in_specs=[pl.BlockSpec(memory_space=pltpu.SMEM),
                  pl.BlockSpec((8, width), lambda h: (0, 0)),
                  pl.BlockSpec((8, width), lambda h: (0, 0))],
        out_specs=pl.BlockSpec((1, rows, cols), lambda h: (h, 0, 0)),
        name=name,
    )(table, tile8(_t5_bucket_np(rel)), tile8(valid))


def _bias_strips(table, rel, valid, head0, nheads, name):
    rows, cols = rel.shape
    bucket = jnp.asarray(_t5_bucket_np(rel))
    validi = jnp.asarray(valid.astype(np.int32))
    return pl.pallas_call(
        functools.partial(_strip_kernel, head0=head0),
        out_shape=jax.ShapeDtypeStruct((nheads, rows, cols), F32),
        grid=(nheads,),
        in_specs=[pl.BlockSpec(memory_space=pltpu.SMEM),
                  pl.BlockSpec((rows, cols), lambda h: (0, 0)),
                  pl.BlockSpec((rows, cols), lambda h: (0, 0))],
        out_specs=pl.BlockSpec((1, rows, cols), lambda h: (h, 0, 0)),
        name=name,
    )(table, bucket, validi)


def _norm_matmul_kernel(x_ref, g_ref, w_ref, o_ref, xn_ref):
    @pl.when(pl.program_id(1) == 0)
    def _():
        xn_ref[...] = _rms(x_ref[...], g_ref[...]).astype(BF16)

    o_ref[...] = _dot(xn_ref[...], w_ref[...]).astype(o_ref.dtype)


def _norm_matmul(x, g, w, out_dtype, tm, tn, name):
    m, d = x.shape
    n = w.shape[1]
    return pl.pallas_call(
        _norm_matmul_kernel,
        out_shape=jax.ShapeDtypeStruct((m, n), out_dtype),
        grid=(m // tm, n // tn),
        in_specs=[pl.BlockSpec((tm, d), lambda i, j: (i, 0)),
                  pl.BlockSpec((1, d), lambda i, j: (0, 0)),
                  pl.BlockSpec((d, tn), lambda i, j: (0, j))],
        out_specs=pl.BlockSpec((tm, tn), lambda i, j: (i, j)),
        scratch_shapes=[pltpu.VMEM((tm, d), BF16)],
        compiler_params=pltpu.CompilerParams(
            dimension_semantics=("arbitrary", "arbitrary"), vmem_limit_bytes=VMEM_LIMIT),
        name=name,
    )(x, g.reshape(1, d), w)


def _in_proj_even_kernel(x_ref, g_ref, w_ref, wa_ref, o_ref, kc0_ref, kc1_ref, vc0_ref, vc1_ref, gate_ref,
                         xn_ref):
    hd = HEAD_DIM

    @pl.when(pl.program_id(1) == 0)
    def _():
        xn = _rms(x_ref[...], g_ref[...]).astype(BF16)
        xn_ref[...] = xn
        aux = _dot(xn, wa_ref[...])
        kc0_ref[...] = aux[:, 0:hd]
        kc1_ref[...] = aux[:, hd:2 * hd]
        vc0_ref[...] = aux[:, 2 * hd:3 * hd]
        vc1_ref[...] = aux[:, 3 * hd:4 * hd]
        gate_ref[...] = aux[:, 4 * hd:]

    o_ref[...] = _dot(xn_ref[...], w_ref[...]).astype(o_ref.dtype)


def _in_proj_even(x, g, w_main, w_aux, tm, tn):
    m, d = x.shape
    n = w_main.shape[1]
    na = w_aux.shape[1]
    row_spec = lambda width: pl.BlockSpec((tm, width), lambda i, j: (i, 0))
    cmp_shape = jax.ShapeDtypeStruct((m, HEAD_DIM), F32)
    return pl.pallas_call(
        _in_proj_even_kernel,
        out_shape=(jax.ShapeDtypeStruct((m, n), BF16), cmp_shape, cmp_shape, cmp_shape, cmp_shape,
                   jax.ShapeDtypeStruct((m, na - 4 * HEAD_DIM), F32)),
        grid=(m // tm, n // tn),
        in_specs=[row_spec(d),
                  pl.BlockSpec((1, d), lambda i, j: (0, 0)),
                  pl.BlockSpec((d, tn), lambda i, j: (0, j)),
                  pl.BlockSpec((d, na), lambda i, j: (0, 0))],
        out_specs=(pl.BlockSpec((tm, tn), lambda i, j: (i, j)), row_spec(HEAD_DIM), row_spec(HEAD_DIM),
                   row_spec(HEAD_DIM), row_spec(HEAD_DIM), row_spec(na - 4 * HEAD_DIM)),
        scratch_shapes=[pltpu.VMEM((tm, d), BF16)],
        compiler_params=pltpu.CompilerParams(
            dimension_semantics=("arbitrary", "arbitrary"), vmem_limit_bytes=VMEM_LIMIT),
        name="in_proj_even",
    )(x, g.reshape(1, d), w_main, w_aux)


def _post_kernel(x_ref, a0_ref, a1_ref, wo0_ref, wo1_ref, g_ref, w1_ref, w2_ref, gf_ref,
                 o_ref, acc_ref, xn_ref, *, final_norm):
    f = pl.program_id(1)

    @pl.when(f == 0)
    def _():
        x1 = x_ref[...] + _dot(a0_ref[...], wo0_ref[...]) + _dot(a1_ref[...], wo1_ref[...])
        acc_ref[...] = x1
        xn_ref[...] = _rms(x1, g_ref[...]).astype(BF16)

    h = jnp.square(jnp.maximum(_dot(xn_ref[...], w1_ref[...]), 0.0))
    acc_ref[...] += _dot(h.astype(BF16), w2_ref[...])

    @pl.when(f == pl.num_programs(1) - 1)
    def _():
        y = acc_ref[...]
        if final_norm:
            y = _rms(y, gf_ref[...])
        o_ref[...] = y


def _post(x, a0, a1, a1_colblk, wo0, wo1, g, w1, w2, gf, final_norm, tm, tf, name):
    m, d = x.shape
    k0 = wo0.shape[0]
    k1 = wo1.shape[0]
    ff = w1.shape[1]
    return pl.pallas_call(
        functools.partial(_post_kernel, final_norm=final_norm),
        out_shape=jax.ShapeDtypeStruct((m, d), F32),
        grid=(m // tm, ff // tf),
        in_specs=[pl.BlockSpec((tm, d), lambda i, f: (i, 0)),
                  pl.BlockSpec((tm, k0), lambda i, f: (i, 0)),
                  pl.BlockSpec((tm, k1), lambda i, f: (i, a1_colblk)),
                  pl.BlockSpec((k0, d), lambda i, f: (0, 0)),
                  pl.BlockSpec((k1, d), lambda i, f: (0, 0)),
                  pl.BlockSpec((1, d), lambda i, f: (0, 0)),
                  pl.BlockSpec((d, tf), lambda i, f: (0, f)),
                  pl.BlockSpec((tf, d), lambda i, f: (f, 0)),
                  pl.BlockSpec((1, d), lambda i, f: (0, 0))],
        out_specs=pl.BlockSpec((tm, d), lambda i, f: (i, 0)),
        scratch_shapes=[pltpu.VMEM((tm, d), F32), pltpu.VMEM((tm, d), BF16)],
        compiler_params=pltpu.CompilerParams(
            dimension_semantics=("arbitrary", "arbitrary"), vmem_limit_bytes=VMEM_LIMIT),
        name=name,
    )(x, a0, a1, wo0, wo1, g.reshape(1, d), w1, w2, gf.reshape(1, d))


def _compress_kernel(kc0_ref, kc1_ref, vc0_ref, vc1_ref, pos_ref, w1_ref, w2_ref, o_ref):
    half = CMP_STRIDE * HEAD_DIM
    for kv, refs in enumerate(((kc0_ref, kc1_ref), (vc0_ref, vc1_ref))):
        w1a = w1_ref[kv, :half, :]
        w1b = w1_ref[kv, half:, :]
        pos = pos_ref[kv].astype(BF16)
        pa = jnp.broadcast_to(pos[0:1], (8, half))
        pb = jnp.broadcast_to(pos[1:2], (8, half))
        posterm = (_dot(pa, w1a) + _dot(pb, w1b))[0:1]
        for grp, x_ref in enumerate(refs):
            x = x_ref[0].astype(BF16)
            first = _dot(x, w1a)
            second = _dot(x, w1b)
            second = pltpu.roll(second, second.shape[0] - 1, 0)
            hid = jax.nn.gelu(first + second + posterm)
            o_ref[kv, grp] = _dot(hid.astype(BF16), w2_ref[kv])


def _compress(kc0, kc1, vc0, vc1, pos, w1, w2):
    batch, nch, wide = kc0.shape
    x_spec = pl.BlockSpec((1, nch, wide), lambda b: (b, 0, 0))
    full = lambda a: pl.BlockSpec(a.shape, lambda b: (0,) * a.ndim)
    return pl.pallas_call(
        _compress_kernel,
        out_shape=jax.ShapeDtypeStruct((2, batch * NSA_KV_HEADS, nch, HEAD_DIM), F32),
        grid=(batch,),
        in_specs=[x_spec, x_spec, x_spec, x_spec, full(pos), full(w1), full(w2)],
        out_specs=pl.BlockSpec((2, NSA_KV_HEADS, nch, HEAD_DIM), lambda b: (0, b, 0, 0)),
        compiler_params=pltpu.CompilerParams(
            dimension_semantics=("arbitrary",), vmem_limit_bytes=VMEM_LIMIT),
        name="nsa_compress",
    )(kc0, kc1, vc0, vc1, pos, w1, w2)


def _softmax_part(m, s):
    m_new = jnp.maximum(m, jnp.max(s, axis=1, keepdims=True))
    return m_new, jnp.exp2(m - m_new), jnp.exp2(s - m_new).astype(BF16)


def _flash_chain(q, k_sc, v_sc, bufs, far0, n_far, tail, side_work=None):
    s_sc, p_sc, acc_sc = bufs
    rows = q.shape[0]
    assert SUB == 2 * TK and tail[0][1] == SUB
    tail0 = tail[0][0]

    def qk(r0):
        return _dot_nt(q, k_sc[pl.ds(pl.multiple_of(r0, TK), SUB), :])

    n_steps = n_far // 2
    s_sc[...] = qk(jnp.where(n_steps > 0, far0, tail0))
    p_sc[...] = jnp.zeros(p_sc.shape, BF16)
    acc_sc[...] = jnp.zeros(acc_sc.shape, F32)

    def trips(start, n_sub, n_trips, after):
        def body(j, m):
            base = pl.multiple_of(start + j * (n_sub * SUB), SUB)
            acc = acc_sc[...] + _dot(p_sc[...], v_sc[pl.ds(base - SUB, SUB), :])
            s_cur = s_sc[...]
            for u in range(n_sub):
                m, alpha, p = _softmax_part(m, s_cur)
                if u + 1 < n_sub:
                    s_cur = qk(base + (u + 1) * SUB)
                    acc = alpha * acc + _dot(p, v_sc[pl.ds(base + u * SUB, SUB), :])
                else:
                    s_sc[...] = qk(jnp.where(j + 1 < n_trips, base + n_sub * SUB, after))
                    p_sc[...] = p
                    acc_sc[...] = alpha * acc
            return m
        return body

    m = jnp.full((rows, 1), NEG, F32)
    start, left = far0, n_steps
    for n_sub in (4, 2, 1):
        n_trips = left // n_sub
        left = left - n_trips * n_sub
        nxt = start + n_trips * (n_sub * SUB)
        after = jnp.where(left > 0, nxt, tail0)
        m = lax.fori_loop(0, n_trips, trips(start, n_sub, n_trips, after), m)
        start = nxt
    if side_work is not None:
        side_work()
    base = pl.multiple_of(far0 + n_steps * SUB, SUB)
    acc = acc_sc[...] + _dot(p_sc[...], v_sc[pl.ds(base - SUB, SUB), :])
    m, alpha, p = _softmax_part(m, s_sc[...] + tail[0][2]())
    acc = alpha * acc + _dot(p, v_sc[pl.ds(tail0, SUB), :])
    return _flash_tail((m, acc), q, k_sc, v_sc, tail[1:])


def _flash_buffers(rows, ncols):
    return [pltpu.VMEM((rows, SUB), F32), pltpu.VMEM((rows, SUB), BF16), pltpu.VMEM((rows, ncols), F32)]


def _flash_tail(carry, q, k_sc, v_sc, tail):
    m, acc = carry
    for r0, width, extra in tail:
        m, alpha, p = _softmax_part(m, _dot_nt(q, k_sc[pl.ds(r0, width), :]) + extra())
        acc = alpha * acc + _dot(p, v_sc[pl.ds(r0, width), :])
    return acc


def _tile_gate(cond):
    return jnp.where(cond, 0.0, NEG)


def _tail_steps(win, odd_ok, rows, strip, near_ok):
    def first():
        return jnp.concatenate([jnp.broadcast_to(_tile_gate(odd_ok), (rows, TK)),
                                strip(0, TK) + _tile_gate(near_ok)], axis=1)

    return [(win, SUB, first), (win + SUB, SUB, lambda: strip(TK, 3 * TK))]


def _topk_rows(score, idx, k, floor):
    sel = jnp.zeros(score.shape, F32)
    nrow = score.shape[0]
    for _ in range(k):
        mx = jnp.max(score, axis=0, keepdims=True)
        first = jnp.min(jnp.where(score == mx, idx, nrow), axis=0, keepdims=True)
        pick = idx == first
        sel = jnp.where(pick, 1.0, sel)
        score = jnp.where(pick, floor, score)
    return sel


def _nsa_kernel(q_ref, qn_ref, ks_ref, vs_ref, kw_ref, vw_ref, kc_ref, vc_ref, gate_ref,
                ssel_ref, swin_ref, scmp_ref, ov_ref, o_ref,
                kaug, vaug, kwp, vwp, kcp, vcp, qaug_sc, ocw_sc, *bufs, seq):
    g = pl.program_id(1)
    i = pl.program_id(2)
    ncp = seq // CMP_STRIDE
    hd = HEAD_DIM
    rows = NSA_GROUP * TQ

    @pl.when(i == 0)
    def _():
        lane = lax.broadcasted_iota(jnp.int32, (BAND_PAD, LANES), 1)
        padflag = jnp.where(lane == hd, 1.0, 0.0).astype(BF16)
        kaug[0:BAND_PAD, 0:LANES] = jnp.zeros((BAND_PAD, LANES), BF16)
        kaug[0:BAND_PAD, LANES:2 * LANES] = padflag
        kwp[0:BAND_PAD, :] = padflag
        vaug[0:BAND_PAD, :] = jnp.zeros((BAND_PAD, LANES), BF16)
        vwp[0:BAND_PAD, :] = jnp.zeros((BAND_PAD, LANES), BF16)
        onescol = jnp.where(lax.broadcasted_iota(jnp.int32, (TK, LANES), 1) == hd, 1.0, 0.0)
        zero_hd = jnp.zeros((TK, hd), F32)
        col = lax.broadcasted_iota(jnp.int32, (TK, LANES), 1)
        row = lax.broadcasted_iota(jnp.int32, (TK, LANES), 0)

        def fill(c, _):
            src = pl.multiple_of(c * TK, TK)
            dst = pl.multiple_of(BAND_PAD + c * TK, TK)

            def pick(ref):
                both = ref[pl.ds(src, TK), :].astype(F32)
                return jnp.concatenate([jnp.where(g == 0, both[:, :hd], both[:, hd:]), zero_hd], axis=1)

            kaug[pl.ds(dst, TK), 0:LANES] = jnp.where((src + row) // SEL_BLOCK == col, 1.0, 0.0).astype(BF16)
            kaug[pl.ds(dst, TK), LANES:2 * LANES] = pick(ks_ref).astype(BF16)
            kwp[pl.ds(dst, TK), :] = pick(kw_ref).astype(BF16)
            vaug[pl.ds(dst, TK), :] = (pick(vs_ref) + onescol).astype(BF16)
            vwp[pl.ds(dst, TK), :] = (pick(vw_ref) + onescol).astype(BF16)
            return 0

        lax.fori_loop(0, seq // TK, fill, 0)
        lane_c = lax.broadcasted_iota(jnp.int32, (CMP_PAD, LANES), 1)
        kcp[0:CMP_PAD, :] = jnp.where(lane_c == hd, 1.0, 0.0).astype(BF16)
        vcp[0:CMP_PAD, :] = jnp.zeros((CMP_PAD, LANES), BF16)
        zc = jnp.zeros((ncp, hd), F32)
        onesc = jnp.where(lax.broadcasted_iota(jnp.int32, (ncp, LANES), 1) == hd, 1.0, 0.0)
        kcp[CMP_PAD:CMP_PAD + ncp, :] = jnp.concatenate([kc_ref[0, 0], zc], axis=1).astype(BF16)
        vcp[CMP_PAD:CMP_PAD + ncp, :] = (jnp.concatenate([vc_ref[0, 0], zc], axis=1) + onesc).astype(BF16)
        tail = kcp.shape[0] - CMP_PAD - ncp
        kcp[CMP_PAD + ncp:, :] = jnp.zeros((tail, LANES), BF16)
        vcp[CMP_PAD + ncp:, :] = jnp.zeros((tail, LANES), BF16)

    def stack(ref, lo, hi):
        return jnp.concatenate([ref[r, :, lo:hi] for r in range(NSA_GROUP)], axis=0)

    def prepare(qsrc_ref, t, slot):
        qblk = qsrc_ref[...].astype(F32) * QSCALE
        lane_t = lax.broadcasted_iota(jnp.int32, (TQ, hd), 1)
        qtail = jnp.where(lane_t == 0, NEG, 0.0)
        qs = [qblk[:, r * hd:(r + 1) * hd] for r in range(NSA_GROUP)]
        q128 = jnp.concatenate([jnp.concatenate([qs[r], qtail], axis=1) for r in range(NSA_GROUP)],
                               axis=0).astype(BF16)

        s_far = _dot_nt(q128, kcp[CMP_PAD:CMP_PAD + ncp, :])
        c_idx = lax.broadcasted_iota(jnp.int32, (1, ncp), 1)
        far_ok = c_idx < CMP_PER_TQ * t - CMP_PAD
        near0 = pl.multiple_of(CMP_PER_TQ * t, CMP_PER_TQ)
        s_near = _dot_nt(q128, kcp[pl.ds(near0, CMP_NEAR), :]) + stack(scmp_ref, 0, CMP_NEAR)
        near_ok = s_near > 0.5 * NEG
        s_far = jnp.where(far_ok, s_far, NEG)
        m_c = jnp.maximum(jnp.max(s_far, axis=1, keepdims=True), jnp.max(s_near, axis=1, keepdims=True))
        p_far = jnp.where(far_ok, jnp.exp2(s_far - m_c), 0.0)
        p_near = jnp.where(near_ok, jnp.exp2(s_near - m_c), 0.0)
        den = jnp.maximum(jnp.sum(p_far, axis=1, keepdims=True) + jnp.sum(p_near, axis=1, keepdims=True), 1e-30)
        pc_far = p_far / den
        pc_near = p_near / den
        ocw_sc[slot, 0] = (_dot(pc_far.astype(BF16), vcp[CMP_PAD:CMP_PAD + ncp, :])
                           + _dot(pc_near.astype(BF16), vcp[pl.ds(near0, CMP_NEAR), :]))

        band0 = pl.multiple_of(TQ * t, TK) + BAND_PAD - 2 * TK
        steps = [(band0, 2 * TK, lambda: stack(swin_ref, 0, 2 * TK)),
                 (band0 + 2 * TK, TK, lambda: stack(swin_ref, 2 * TK, 3 * TK))]
        acc = _flash_tail((jnp.full((rows, 1), NEG, F32), jnp.zeros((rows, LANES), F32)), q128, kwp, vwp, steps)
        ocw_sc[slot, 1] = acc / acc[:, hd:hd + 1]

        pcs_far = sum(pc_far[r * TQ:(r + 1) * TQ] for r in range(NSA_GROUP))
        pcs_near = sum(pc_near[r * TQ:(r + 1) * TQ] for r in range(NSA_GROUP))

        def hi_lo_dot(p, w):
            hi = p.astype(BF16)
            lo = (p - hi.astype(F32)).astype(BF16)
            return _dot(hi, w) + _dot(lo, w)

        imp = (hi_lo_dot(pcs_far, ov_ref[CMP_PAD:CMP_PAD + ncp, :])
               + hi_lo_dot(pcs_near, ov_ref[pl.ds(near0, CMP_NEAR), :]))
        imp_t = imp.T
        n_idx = lax.broadcasted_iota(jnp.int32, (LANES, TQ), 0)
        cur = (TQ * t + lax.broadcasted_iota(jnp.int32, (LANES, TQ), 1)) // SEL_BLOCK
        eligible = n_idx <= cur
        forced = ((n_idx == 0) | (n_idx == cur) | (n_idx == cur - 1)) & eligible
        rest = jnp.where(forced, -3.0e38, jnp.where(eligible, imp_t, -1.0))
        sel_t = _topk_rows(rest, n_idx, SEL_TOPN - 3, -3.0e38)
        maskneg = jnp.where(forced | (sel_t > 0.5), 0.0, NEG).T
        qaug_sc[slot] = jnp.concatenate(
            [jnp.concatenate([maskneg, qs[r], qtail], axis=1) for r in range(NSA_GROUP)],
            axis=0).astype(BF16)

    @pl.when(i == 0)
    def _():
        prepare(q_ref, 0, 0)

    slot = i % 2
    n_far = jnp.maximum(i - 2, 0)
    win = pl.multiple_of(TQ * i, TK)
    tail = _tail_steps(win, n_far % 2 == 1, rows, functools.partial(stack, ssel_ref), True)
    prepare_next = functools.partial(prepare, qn_ref, jnp.minimum(i + 1, pl.num_programs(2) - 1), 1 - slot)
    acc = _flash_chain(qaug_sc[slot], kaug, vaug, bufs, BAND_PAD, n_far, tail, prepare_next)
    o_sel = acc[:, :hd] / acc[:, hd:hd + 1]

    o_cmp = ocw_sc[slot, 0]
    o_win = ocw_sc[slot, 1]
    sig = jax.nn.sigmoid(gate_ref[...])
    outs = []
    for r in range(NSA_GROUP):
        def gcol(branch):
            c0 = branch * NSA_HEADS + r
            c1 = c0 + NSA_GROUP
            return jnp.where(g == 0, sig[:, c0:c0 + 1], sig[:, c1:c1 + 1])
        sl = slice(r * TQ, (r + 1) * TQ)
        outs.append(gcol(0) * o_cmp[sl, :hd] + gcol(1) * o_sel[sl] + gcol(2) * o_win[sl, :hd])
    o_ref[...] = jnp.concatenate(outs, axis=1).astype(o_ref.dtype)


def _nsa(main, aux, ckv, ssel, swin, scmp, ov, batch, seq):
    nq = seq // TQ
    ncp = seq // CMP_STRIDE
    gw = NSA_GROUP * HEAD_DIM
    once = pl.Buffered(1)
    kv_spec = lambda colblk: pl.BlockSpec((seq, LANES), lambda b, g, i: (b, colblk), pipeline_mode=once)
    strip_spec = lambda cols: pl.BlockSpec((NSA_GROUP, TQ, cols), lambda b, g, i: (g, 0, 0), pipeline_mode=once)
    return pl.pallas_call(
        functools.partial(_nsa_kernel, seq=seq),
        out_shape=jax.ShapeDtypeStruct((batch * seq, NSA_HEADS * HEAD_DIM), BF16),
        grid=(batch, NSA_KV_HEADS, nq),
        in_specs=[pl.BlockSpec((TQ, gw), lambda b, g, i: (b * nq + i, g)),
                  pl.BlockSpec((TQ, gw), lambda b, g, i: (b * nq + jnp.minimum(i + 1, nq - 1), g)),
                  kv_spec(4), kv_spec(5), kv_spec(6), kv_spec(7),
                  pl.BlockSpec((1, 1, ncp, HEAD_DIM), lambda b, g, i: (0, b * NSA_KV_HEADS + g, 0, 0)),
                  pl.BlockSpec((1, 1, ncp, HEAD_DIM), lambda b, g, i: (1, b * NSA_KV_HEADS + g, 0, 0)),
                  pl.BlockSpec((TQ, LANES), lambda b, g, i: (b * nq + i, 0)),
                  strip_spec(3 * TK), strip_spec(3 * TK), strip_spec(CMP_NEAR),
                  pl.BlockSpec(ov.shape, lambda b, g, i: (0, 0))],
        out_specs=pl.BlockSpec((TQ, gw), lambda b, g, i: (b * nq + i, g)),
        scratch_shapes=[pltpu.VMEM((BAND_PAD + seq, 2 * LANES), BF16),
                        pltpu.VMEM((BAND_PAD + seq, LANES), BF16),
                        pltpu.VMEM((BAND_PAD + seq, LANES), BF16),
                        pltpu.VMEM((BAND_PAD + seq, LANES), BF16),
                        pltpu.VMEM((ov.shape[0], LANES), BF16),
                        pltpu.VMEM((ov.shape[0], LANES), BF16),
                        pltpu.VMEM((2, NSA_GROUP * TQ, 2 * LANES), BF16),
                        pltpu.VMEM((2, 2, NSA_GROUP * TQ, LANES), F32),
                        *_flash_buffers(NSA_GROUP * TQ, LANES)],
        compiler_params=_attention_params(),
        name="nsa_attention",
    )(main, main, main, main, main, main, ckv, ckv, aux, ssel, swin, scmp, ov)


def _ones_column():
    return jnp.where(lax.broadcasted_iota(jnp.int32, (TK, LANES), 1) == 0, 1.0, 0.0).astype(BF16)


def _causal_flash(q, k_sc, v_sc, bufs, strip, i):
    n_far = jnp.maximum(2 * i - 1, 0)
    win = pl.multiple_of(i * TQD, SUB)
    tail = _tail_steps(win, i >= 1, q.shape[0], strip, i >= 1)
    return _flash_chain(q, k_sc, v_sc, bufs, FRONT, n_far, tail)


def _diff_kernel(q_ref, k_ref, v_ref, lam_ref, sub_ref, strip_ref, o_ref, k_sc, v_sc, *bufs, lam_init, seq):
    i = pl.program_id(2)
    hd = HEAD_DIM

    @pl.when(i == 0)
    def _():
        k_sc[0:FRONT, :] = jnp.zeros((FRONT, k_sc.shape[1]), BF16)
        v_sc[0:FRONT, :] = jnp.zeros((FRONT, v_sc.shape[1]), BF16)
        ones = _ones_column()

        def fill(c, _):
            r0 = pl.multiple_of(c * TK, TK)
            k_sc[pl.ds(FRONT + r0, TK), :] = k_ref[pl.ds(r0, TK), :]
            v_sc[pl.ds(FRONT + r0, TK), 0:LANES] = v_ref[pl.ds(r0, TK), :]
            v_sc[pl.ds(FRONT + r0, TK), LANES:2 * LANES] = ones
            return 0

        lax.fori_loop(0, seq // TK, fill, 0)

    q = q_ref[...].astype(F32) * QSCALE
    lane = lax.broadcasted_iota(jnp.int32, (TQD, 2 * hd), 1)
    qst = jnp.concatenate([jnp.where(lane < hd, q, 0.0), jnp.where(lane < hd, 0.0, q)],
                          axis=0).astype(BF16)
    strip = lambda lo, hi: jnp.concatenate([strip_ref[0, :, lo:hi]] * 2, axis=0)
    acc = _causal_flash(qst, k_sc, v_sc, bufs, strip, i)
    o = acc[:, :2 * hd] / acc[:, 2 * hd:2 * hd + 1]
    lam_p = lam_ref[...]
    lam = (jnp.exp(jnp.sum(lam_p[0:1] * lam_p[1:2], axis=1, keepdims=True))
           - jnp.exp(jnp.sum(lam_p[2:3] * lam_p[3:4], axis=1, keepdims=True)) + lam_init)
    d = o[:TQD] - lam * o[TQD:]
    o_ref[...] = (_rms(d, sub_ref[...]) * (1.0 - lam_init)).astype(o_ref.dtype)


def _diff(main, lam_p, subln, strips, batch, seq, lam_init):
    nq = seq // TQD
    blk = 2 * HEAD_DIM
    q0, k0, v0 = 1024 // blk, 1536 // blk, 2048 // blk
    return pl.pallas_call(
        functools.partial(_diff_kernel, lam_init=lam_init, seq=seq),
        out_shape=jax.ShapeDtypeStruct((batch * seq, DIFF_HEADS * blk), BF16),
        grid=(batch, DIFF_HEADS, nq),
        in_specs=[pl.BlockSpec((TQD, blk), lambda b, h, i: (b * nq + i, q0 + h)),
                  pl.BlockSpec((seq, blk), lambda b, h, i: (b, k0 + h)),
                  pl.BlockSpec((seq, blk), lambda b, h, i: (b, v0 + h)),
                  pl.BlockSpec((4, HEAD_DIM), lambda b, h, i: (0, 0)),
                  pl.BlockSpec((1, blk), lambda b, h, i: (0, 0)),
                  pl.BlockSpec((1, TQD, 3 * TK), lambda b, h, i: (NSA_HEADS + h, 0, 0))],
        out_specs=pl.BlockSpec((TQD, blk), lambda b, h, i: (b * nq + i, h)),
        scratch_shapes=[pltpu.VMEM((FRONT + seq, LANES), BF16), pltpu.VMEM((FRONT + seq, 2 * LANES), BF16),
                        *_flash_buffers(2 * TQD, 2 * LANES)],
        compiler_params=_attention_params(),
        name="diff_attention",
    )(main, main, main, lam_p, subln.reshape(1, blk), strips)


def _moba_kernel(q_ref, k_ref, v_ref, strip_ref, o_ref, kaug, vaug, kmean, *bufs, seq):
    i = pl.program_id(2)
    hd = HEAD_DIM
    nb = seq // MOBA_BLOCK

    @pl.when(i == 0)
    def _():
        kmean[...] = jnp.zeros(kmean.shape, F32)
        kaug[0:FRONT, :] = jnp.zeros((FRONT, kaug.shape[1]), BF16)
        vaug[0:FRONT, :] = jnp.zeros((FRONT, vaug.shape[1]), BF16)
        col = lax.broadcasted_iota(jnp.int32, (MOBA_BLOCK, LANES), 1)
        ones = _ones_column()

        def fill(n, _):
            r0 = pl.multiple_of(n * MOBA_BLOCK, MOBA_BLOCK)
            kb = k_ref[pl.ds(r0, MOBA_BLOCK), :]
            kaug[pl.ds(FRONT + r0, MOBA_BLOCK), 0:LANES] = jnp.where(col == n, 1.0, 0.0).astype(BF16)
            kaug[pl.ds(FRONT + r0, MOBA_BLOCK), LANES:2 * LANES] = kb
            vaug[pl.ds(FRONT + r0, MOBA_BLOCK), 0:LANES] = v_ref[pl.ds(r0, MOBA_BLOCK), :]
            vaug[pl.ds(FRONT + r0, MOBA_BLOCK), LANES:2 * LANES] = ones
            kmean[pl.ds(n, 1), :] = jnp.mean(kb.astype(F32), axis=0, keepdims=True)
            return 0

        lax.fori_loop(0, nb, fill, 0)

    qblk = q_ref[...]
    qf = qblk.astype(F32) * QSCALE
    lane = lax.broadcasted_iota(jnp.int32, (TQD, LANES), 1)
    n_idx = lax.broadcasted_iota(jnp.int32, (nb, TQD), 0)
    own = (i * TQD + lax.broadcasted_iota(jnp.int32, (nb, TQD), 1)) // MOBA_BLOCK
    past = n_idx < own
    km = kmean[0:nb, :]
    lane_k = lax.broadcasted_iota(jnp.int32, km.shape, 1)
    no_block = jnp.full((LANES - nb, TQD), NEG, F32)
    qrows = []
    for hh in range(2):
        mine = (lane < hd) if hh == 0 else (lane >= hd)
        mine_k = (lane_k < hd) if hh == 0 else (lane_k >= hd)
        gate_t = _dot_nt(jnp.where(mine_k, km, 0.0).astype(BF16), qblk)
        sel_t = _topk_rows(jnp.where(past, gate_t, NEG), n_idx, MOBA_TOPK, -3.0e38)
        allowed = ((sel_t > 0.5) & past) | (n_idx == own)
        maskneg = jnp.concatenate([jnp.where(allowed, 0.0, NEG), no_block], axis=0).T
        qrows.append(jnp.concatenate([maskneg, jnp.where(mine, qf, 0.0)], axis=1))
    qaug = jnp.concatenate(qrows, axis=0).astype(BF16)
    strip = lambda lo, hi: jnp.concatenate([strip_ref[0, :, lo:hi], strip_ref[1, :, lo:hi]], axis=0)
    acc = _causal_flash(qaug, kaug, vaug, bufs, strip, i)
    o = acc[:, :LANES] / acc[:, LANES:LANES + 1]
    o_ref[...] = jnp.where(lane < hd, o[:TQD], o[TQD:]).astype(o_ref.dtype)


def _moba(main, strips, batch, seq):
    nq = seq // TQD
    pairs = MOBA_HEADS // 2
    return pl.pallas_call(
        functools.partial(_moba_kernel, seq=seq),
        out_shape=jax.ShapeDtypeStruct((batch * seq, MOBA_HEADS * HEAD_DIM), BF16),
        grid=(batch, pairs, nq),
        in_specs=[pl.BlockSpec((TQD, LANES), lambda b, h, i: (b * nq + i, h)),
                  pl.BlockSpec((seq, LANES), lambda b, h, i: (b, pairs + h)),
                  pl.BlockSpec((seq, LANES), lambda b, h, i: (b, 2 * pairs + h)),
                  pl.BlockSpec((2, TQD, 3 * TK), lambda b, h, i: (h, 0, 0))],
        out_specs=pl.BlockSpec((TQD, LANES), lambda b, h, i: (b * nq + i, h)),
        scratch_shapes=[pltpu.VMEM((FRONT + seq, 2 * LANES), BF16), pltpu.VMEM((FRONT + seq, 2 * LANES), BF16),
                        pltpu.VMEM((LANES, LANES), F32), *_flash_buffers(2 * TQD, 2 * LANES)],
        compiler_params=_attention_params(),
        name="moba_attention",
    )(main, main, main, strips)


def _cmp_strip_geometry():
    r = np.arange(TQ)[:, None]
    cc = np.arange(CMP_NEAR)[None, :]
    rel_c = r + CMP_STRIDE * CMP_PAD - (CMP_BLOCK - 1) - CMP_STRIDE * cc
    return rel_c, rel_c >= 0


def _overlap_padded(seq):
    nc = (seq - CMP_BLOCK) // CMP_STRIDE + 1
    ncp = seq // CMP_STRIDE
    tok = np.arange(nc)[:, None] * CMP_STRIDE + np.arange(CMP_BLOCK)[None, :]
    ov = np.zeros((CMP_PAD + ncp + 2 * CMP_PER_TQ, LANES), np.float32)
    np.add.at(ov, (CMP_PAD + np.repeat(np.arange(nc), CMP_BLOCK), (tok // SEL_BLOCK).reshape(-1)),
              1.0 / CMP_BLOCK)
    return jnp.asarray(ov, dtype=BF16)


def _strips(bias_table):
    rel_c, valid_c = _cmp_strip_geometry()
    far = 1 << 30
    return (_toeplitz_strips(bias_table, TQD, 3 * TK, TK, 0, far, MOBA_HEADS, "bias_strip_diag"),
            _toeplitz_strips(bias_table, TQ, 3 * TK, 2 * TK, 0, far, NSA_HEADS, "bias_strip_sel"),
            _toeplitz_strips(bias_table, TQ, 3 * TK, 2 * TK, 0, WINDOW, NSA_HEADS, "bias_strip_win"),
            _bias_strips(bias_table, rel_c, valid_c, 0, NSA_HEADS, "bias_strip_cmp"))


def _even_heads(xf, batch, seq, bias_table, g_mix, w_in, pos_k, w1_k, w2_k, pos_v, w1_v, w2_v,
                lam_q1, lam_k1, lam_q2, lam_k2, subln, strips=None):
    d = xf.shape[1]
    strip_a, strip_sel, strip_win, strip_cmp = strips if strips is not None else _strips(bias_table)
    w_main = jnp.concatenate([w_in[:, 0:512], w_in[:, 768:1280], w_in[:, 1304:2840]], axis=1).astype(BF16)
    w_aux = jnp.concatenate([w_in[:, 512:768], w_in[:, 1280:1304],
                             jnp.zeros((d, 3 * LANES - 280), F32)], axis=1).astype(BF16)
    main0, kc0, kc1, vc0, vc1, gates = _in_proj_even(xf, g_mix, w_main, w_aux, 1024, w_main.shape[1] // 2)

    nch = seq // CMP_STRIDE
    chunks = lambda a: a.reshape(batch, nch, CMP_STRIDE * HEAD_DIM)
    pos = jnp.stack([pos_k, pos_v]).reshape(2, 2, CMP_STRIDE * HEAD_DIM)
    ckv = _compress(chunks(kc0), chunks(kc1), chunks(vc0), chunks(vc1), pos,
                    jnp.stack([w1_k, w1_v]).astype(BF16), jnp.stack([w2_k, w2_v]).astype(BF16))

    o_a = _nsa(main0, gates, ckv, strip_sel, strip_win, strip_cmp, _overlap_padded(seq), batch, seq)
    lam_p = jnp.stack([lam_q1, lam_k1, lam_q2, lam_k2])
    lam_init = 0.8 - 0.6 * math.exp(-0.3 * 0)
    o_b = _diff(main0, lam_p, subln, strip_a, batch, seq, lam_init)
    return o_a, o_b, ckv


def _odd_heads(xf, batch, seq, bias_table, g_mix, w_in, strips=None):
    strip_a = (strips if strips is not None else _strips(bias_table))[0]
    main1 = _norm_matmul(xf, g_mix, w_in.astype(BF16), BF16, 1024, w_in.shape[1] // 2, "in_proj_odd")
    return _moba(main1, strip_a, batch, seq)


def kernel(x, bias_table, norm_mix, norm_mlp, norm_final, mlp_w1, mlp_w2, ev_w_in, ev_w_out,
           ev_cmp_pos_k, ev_cmp_w1_k, ev_cmp_w2_k, ev_cmp_pos_v, ev_cmp_w1_v, ev_cmp_w2_v,
           ev_lam_q1, ev_lam_k1, ev_lam_q2, ev_lam_k2, ev_subln, od_w_in, od_w_out):
    batch, seq, d = x.shape
    assert d == D_MODEL and (batch * seq) % 1024 == 0 and seq % TQD == 0
    assert SEL_TOPN * SEL_BLOCK <= seq <= LANES * SEL_BLOCK and seq >= MOBA_TOPK * MOBA_BLOCK
    xf = x.reshape(batch * seq, d)
    strips = _strips(bias_table)

    o_a, o_b, _ = _even_heads(xf, batch, seq, bias_table, norm_mix[0], ev_w_in[0],
                              ev_cmp_pos_k[0], ev_cmp_w1_k[0], ev_cmp_w2_k[0],
                              ev_cmp_pos_v[0], ev_cmp_w1_v[0], ev_cmp_w2_v[0],
                              ev_lam_q1[0], ev_lam_k1[0], ev_lam_q2[0], ev_lam_k2[0], ev_subln[0], strips)
    wo = ev_w_out[0].astype(BF16)
    x1 = _post(xf, o_a, o_b, 0, wo[:512], wo[512:], norm_mlp[0], mlp_w1[0].astype(BF16),
               mlp_w2[0].astype(BF16), norm_final, False, 1024, 1024, "post_even")

    o_m = _odd_heads(x1, batch, seq, bias_table, norm_mix[1], od_w_in[0], strips)
    wo = od_w_out[0].astype(BF16)
    out = _post(x1, o_m, o_m, 1, wo[:512], wo[512:], norm_mlp[1], mlp_w1[1].astype(BF16),
                mlp_w2[1].astype(BF16), norm_final, True, 1024, 1024, "post_odd")
    return out.reshape(batch, seq, d)
```

```python
import functools
import math

import numpy as np
import jax
import jax.numpy as jnp
from jax import lax
from jax.experimental import pallas as pl
from jax.experimental.pallas import tpu as pltpu

F32 = jnp.float32
BF16 = jnp.bfloat16

D_MODEL = 1024
HEAD_DIM = 64
EPS = 1e-6
NEG = -1e30
NUM_BUCKETS = 32
MAX_DISTANCE = 128
NSA_HEADS = 8
NSA_KV_HEADS = 2
NSA_GROUP = 4
CMP_BLOCK = 32
CMP_STRIDE = 16
SEL_BLOCK = 64
SEL_TOPN = 16
WINDOW = 512
FORCE_BONUS = 1e3
DIFF_HEADS = 4
MOBA_HEADS = 16
MOBA_BLOCK = 256
MOBA_TOPK = 3
D_FF = 4 * D_MODEL
SCALE = HEAD_DIM ** -0.5

LANES = 128
TQ = 256
TQD = 512
TK = 256
FAR_REL = 113
CMP_PER_TQ = TQ // CMP_STRIDE
CMP_NEAR = 128
CMP_PAD = CMP_NEAR - 2 * CMP_PER_TQ
SUB = 2 * TK
BAND_PAD = 3 * TK
FRONT = 2 * TK
LOG2E = math.log2(math.e)
QSCALE = SCALE * LOG2E
VMEM_LIMIT = 56 * 1024 * 1024


def _attention_params():
    return pltpu.CompilerParams(dimension_semantics=("arbitrary", "arbitrary", "arbitrary"),
                                vmem_limit_bytes=VMEM_LIMIT)


def _dot_nt(a, b):
    return lax.dot_general(a, b, (((1,), (1,)), ((), ())), preferred_element_type=F32)


def _dot(a, b):
    return jnp.dot(a, b, preferred_element_type=F32)


def _rms(x, g):
    return x * lax.rsqrt(jnp.mean(x * x, axis=-1, keepdims=True) + EPS) * g


def _t5_bucket_np(rel):
    n = np.maximum(rel, 0)
    max_exact = NUM_BUCKETS // 2
    large = max_exact + (np.log(np.maximum(n, 1).astype(np.float32) / np.float32(max_exact))
                         / np.float32(math.log(MAX_DISTANCE / max_exact))
                         * np.float32(NUM_BUCKETS - max_exact)).astype(np.int32)
    large = np.minimum(large, NUM_BUCKETS - 1)
    return np.where(n < max_exact, n, large).astype(np.int32)


def _strip_kernel(tbl_ref, bucket_ref, valid_ref, o_ref, *, head0):
    h = pl.program_id(0) + head0
    bucket = bucket_ref[...]
    last = tbl_ref[NUM_BUCKETS - 1, h]
    acc = jnp.zeros(bucket.shape, F32)
    for b in range(NUM_BUCKETS - 1):
        acc = jnp.where(bucket == b, tbl_ref[b, h] - last, acc)
    o_ref[0] = jnp.where(valid_ref[...] != 0, acc * LOG2E, NEG)


def _toeplitz_kernel(tbl_ref, bucket_ref, valid_ref, o_ref, *, rows, cols):
    h = pl.program_id(0)
    bucket = bucket_ref[...]
    last = tbl_ref[NUM_BUCKETS - 1, h]
    acc = jnp.zeros(bucket.shape, F32)
    for b in range(NUM_BUCKETS - 1):
        acc = jnp.where(bucket == b, tbl_ref[b, h] - last, acc)
    line = jnp.where(valid_ref[...] != 0, acc * LOG2E, NEG)
    wide = jnp.broadcast_to(line[0:1], (rows, line.shape[1]))
    o_ref[0] = pltpu.roll(wide, 0, 1, stride=1, stride_axis=0)[:, :cols]


def _toeplitz_strips(table, rows, cols, off, lo, hi, nheads, name):
    width = -(-(rows + cols) // LANES) * LANES
    j = np.arange(width)
    rel = off - np.where(j < cols, j, j - width)
    valid = (rel >= lo) & (rel < hi)
    tile8 = lambda a: jnp.asarray(np.broadcast_to(a[None, :], (8, width)).astype(np.int32))
    return pl.pallas_call(
        functools.partial(_toeplitz_kernel, rows=rows, cols=cols),
        out_shape=jax.ShapeDtypeStruct((nheads, rows, cols), F32),
        grid=(nheads,),
        in_specs=[pl.BlockSpec(memory_space=pltpu.SMEM),
                  pl.BlockSpec((8, width), lambda h: (0, 0)),
                  pl.BlockSpec((8, width), lambda h: (0, 0))],
        out_specs=pl.BlockSpec((1, rows, cols), lambda h: (h, 0, 0)),
        name=name,
    )(table, tile8(_t5_bucket_np(rel)), tile8(valid))


def _bias_strips(table, rel, valid, head0, nheads, name):
    rows, cols = rel.shape
    bucket = jnp.asarray(_t5_bucket_np(rel))
    validi = jnp.asarray(valid.astype(np.int32))
    return pl.pallas_call(
        functools.partial(_strip_kernel, head0=head0),
        out_shape=jax.ShapeDtypeStruct((nheads, rows, cols), F32),
        grid=(nheads,),
        in_specs=[pl.BlockSpec(memory_space=pltpu.SMEM),
                  pl.BlockSpec((rows, cols), lambda h: (0, 0)),
                  pl.BlockSpec((rows, cols), lambda h: (0, 0))],
        out_specs=pl.BlockSpec((1, rows, cols), lambda h: (h, 0, 0)),
        name=name,
    )(table, bucket, validi)


def _norm_matmul_kernel(x_ref, g_ref, w_ref, o_ref, xn_ref):
    @pl.when(pl.program_id(1) == 0)
    def _():
        xn_ref[...] = _rms(x_ref[...], g_ref[...]).astype(BF16)

    o_ref[...] = _dot(xn_ref[...], w_ref[...]).astype(o_ref.dtype)


def _norm_matmul(x, g, w, out_dtype, tm, tn, name):
    m, d = x.shape
    n = w.shape[1]
    return pl.pallas_call(
        _norm_matmul_kernel,
        out_shape=jax.ShapeDtypeStruct((m, n), out_dtype),
        grid=(m // tm, n // tn),
        in_specs=[pl.BlockSpec((tm, d), lambda i, j: (i, 0)),
                  pl.BlockSpec((1, d), lambda i, j: (0, 0)),
                  pl.BlockSpec((d, tn), lambda i, j: (0, j))],
        out_specs=pl.BlockSpec((tm, tn), lambda i, j: (i, j)),
        scratch_shapes=[pltpu.VMEM((tm, d), BF16)],
        compiler_params=pltpu.CompilerParams(
            dimension_semantics=("arbitrary", "arbitrary"), vmem_limit_bytes=VMEM_LIMIT),
        name=name,
    )(x, g.reshape(1, d), w)


def _in_proj_even_kernel(x_ref, g_ref, w_ref, wa_ref, o_ref, kc0_ref, kc1_ref, vc0_ref, vc1_ref, gate_ref,
                         xn_ref):
    hd = HEAD_DIM

    @pl.when(pl.program_id(1) == 0)
    def _():
        xn = _rms(x_ref[...], g_ref[...]).astype(BF16)
        xn_ref[...] = xn
        aux = _dot(xn, wa_ref[...])
        kc0_ref[...] = aux[:, 0:hd]
        kc1_ref[...] = aux[:, hd:2 * hd]
        vc0_ref[...] = aux[:, 2 * hd:3 * hd]
        vc1_ref[...] = aux[:, 3 * hd:4 * hd]
        gate_ref[...] = aux[:, 4 * hd:]

    o_ref[...] = _dot(xn_ref[...], w_ref[...]).astype(o_ref.dtype)


def _in_proj_even(x, g, w_main, w_aux, tm, tn):
    m, d = x.shape
    n = w_main.shape[1]
    na = w_aux.shape[1]
    row_spec = lambda width: pl.BlockSpec((tm, width), lambda i, j: (i, 0))
    cmp_shape = jax.ShapeDtypeStruct((m, HEAD_DIM), F32)
    return pl.pallas_call(
        _in_proj_even_kernel,
        out_shape=(jax.ShapeDtypeStruct((m, n), BF16), cmp_shape, cmp_shape, cmp_shape, cmp_shape,
                   jax.ShapeDtypeStruct((m, na - 4 * HEAD_DIM), F32)),
        grid=(m // tm, n // tn),
        in_specs=[row_spec(d),
                  pl.BlockSpec((1, d), lambda i, j: (0, 0)),
                  pl.BlockSpec((d, tn), lambda i, j: (0, j)),
                  pl.BlockSpec((d, na), lambda i, j: (0, 0))],
        out_specs=(pl.BlockSpec((tm, tn), lambda i, j: (i, j)), row_spec(HEAD_DIM), row_spec(HEAD_DIM),
                   row_spec(HEAD_DIM), row_spec(HEAD_DIM), row_spec(na - 4 * HEAD_DIM)),
        scratch_shapes=[pltpu.VMEM((tm, d), BF16)],
        compiler_params=pltpu.CompilerParams(
            dimension_semantics=("arbitrary", "arbitrary"), vmem_limit_bytes=VMEM_LIMIT),
        name="in_proj_even",
    )(x, g.reshape(1, d), w_main, w_aux)


def _post_kernel(x_ref, a0_ref, a1_ref, wo0_ref, wo1_ref, g_ref, w1_ref, w2_ref, gf_ref,
                 o_ref, acc_ref, xn_ref, *, final_norm):
    f = pl.program_id(1)

    @pl.when(f == 0)
    def _():
        x1 = x_ref[...] + _dot(a0_ref[...], wo0_ref[...]) + _dot(a1_ref[...], wo1_ref[...])
        acc_ref[...] = x1
        xn_ref[...] = _rms(x1, g_ref[...]).astype(BF16)

    h = jnp.square(jnp.maximum(_dot(xn_ref[...], w1_ref[...]), 0.0))
    acc_ref[...] += _dot(h.astype(BF16), w2_ref[...])

    @pl.when(f == pl.num_programs(1) - 1)
    def _():
        y = acc_ref[...]
        if final_norm:
            y = _rms(y, gf_ref[...])
        o_ref[...] = y


def _post(x, a0, a1, a1_colblk, wo0, wo1, g, w1, w2, gf, final_norm, tm, tf, name):
    m, d = x.shape
    k0 = wo0.shape[0]
    k1 = wo1.shape[0]
    ff = w1.shape[1]
    return pl.pallas_call(
        functools.partial(_post_kernel, final_norm=final_norm),
        out_shape=jax.ShapeDtypeStruct((m, d), F32),
        grid=(m // tm, ff // tf),
        in_specs=[pl.BlockSpec((tm, d), lambda i, f: (i, 0)),
                  pl.BlockSpec((tm, k0), lambda i, f: (i, 0)),
                  pl.BlockSpec((tm, k1), lambda i, f: (i, a1_colblk)),
                  pl.BlockSpec((k0, d), lambda i, f: (0, 0)),
                  pl.BlockSpec((k1, d), lambda i, f: (0, 0)),
                  pl.BlockSpec((1, d), lambda i, f: (0, 0)),
                  pl.BlockSpec((d, tf), lambda i, f: (0, f)),
                  pl.BlockSpec((tf, d), lambda i, f: (f, 0)),
                  pl.BlockSpec((1, d), lambda i, f: (0, 0))],
        out_specs=pl.BlockSpec((tm, d), lambda i, f: (i, 0)),
        scratch_shapes=[pltpu.VMEM((tm, d), F32), pltpu.VMEM((tm, d), BF16)],
        compiler_params=pltpu.CompilerParams(
            dimension_semantics=("arbitrary", "arbitrary"), vmem_limit_bytes=VMEM_LIMIT),
        name=name,
    )(x, a0, a1, wo0, wo1, g.reshape(1, d), w1, w2, gf.reshape(1, d))


def _compress_kernel(kc0_ref, kc1_ref, vc0_ref, vc1_ref, pos_ref, w1_ref, w2_ref, o_ref):
    half = CMP_STRIDE * HEAD_DIM
    for kv, refs in enumerate(((kc0_ref, kc1_ref), (vc0_ref, vc1_ref))):
        w1a = w1_ref[kv, :half, :]
        w1b = w1_ref[kv, half:, :]
        pos = pos_ref[kv].astype(BF16)
        pa = jnp.broadcast_to(pos[0:1], (8, half))
        pb = jnp.broadcast_to(pos[1:2], (8, half))
        posterm = (_dot(pa, w1a) + _dot(pb, w1b))[0:1]
        for grp, x_ref in enumerate(refs):
            x = x_ref[0].astype(BF16)
            first = _dot(x, w1a)
            second = _dot(x, w1b)
            second = pltpu.roll(second, second.shape[0] - 1, 0)
            hid = jax.nn.gelu(first + second + posterm)
            o_ref[kv, grp] = _dot(hid.astype(BF16), w2_ref[kv])


def _compress(kc0, kc1, vc0, vc1, pos, w1, w2):
    batch, nch, wide = kc0.shape
    x_spec = pl.BlockSpec((1, nch, wide), lambda b: (b, 0, 0))
    full = lambda a: pl.BlockSpec(a.shape, lambda b: (0,) * a.ndim)
    return pl.pallas_call(
        _compress_kernel,
        out_shape=jax.ShapeDtypeStruct((2, batch * NSA_KV_HEADS, nch, HEAD_DIM), F32),
        grid=(batch,),
        in_specs=[x_spec, x_spec, x_spec, x_spec, full(pos), full(w1), full(w2)],
        out_specs=pl.BlockSpec((2, NSA_KV_HEADS, nch, HEAD_DIM), lambda b: (0, b, 0, 0)),
        compiler_params=pltpu.CompilerParams(
            dimension_semantics=("arbitrary",), vmem_limit_bytes=VMEM_LIMIT),
        name="nsa_compress",
    )(kc0, kc1, vc0, vc1, pos, w1, w2)


def _softmax_part(m, s):
    m_new = jnp.maximum(m, jnp.max(s, axis=1, keepdims=True))
    return m_new, jnp.exp2(m - m_new), jnp.exp2(s - m_new).astype(BF16)


def _pv(p, v_sc, r0, width):
    if not isinstance(v_sc, (tuple, list)):
        return _dot(p, v_sc[pl.ds(r0, width), :])
    share = p.shape[0] // len(v_sc)
    return jnp.concatenate([_dot(p[g * share:(g + 1) * share], v[pl.ds(r0, width), :])
                            for g, v in enumerate(v_sc)], axis=0)


def _flash_chain(q, k_sc, v_sc, bufs, far0, n_far, tail, side_work=None):
    s_sc, p_sc, acc_sc = bufs
    rows = q.shape[0]
    assert SUB == 2 * TK and tail[0][1] == SUB
    tail0 = tail[0][0]

    def qk(r0):
        return _dot_nt(q, k_sc[pl.ds(pl.multiple_of(r0, TK), SUB), :])

    n_steps = n_far // 2
    s_sc[...] = qk(jnp.where(n_steps > 0, far0, tail0))
    p_sc[...] = jnp.zeros(p_sc.shape, BF16)
    acc_sc[...] = jnp.zeros(acc_sc.shape, F32)

    def trips(start, n_sub, n_trips, after):
        def body(j, m):
            base = pl.multiple_of(start + j * (n_sub * SUB), SUB)
            acc = acc_sc[...] + _pv(p_sc[...], v_sc, base - SUB, SUB)
            s_cur = s_sc[...]
            for u in range(n_sub):
                m, alpha, p = _softmax_part(m, s_cur)
                if u + 1 < n_sub:
                    s_cur = qk(base + (u + 1) * SUB)
                    acc = alpha * acc + _pv(p, v_sc, base + u * SUB, SUB)
                else:
                    s_sc[...] = qk(jnp.where(j + 1 < n_trips, base + n_sub * SUB, after))
                    p_sc[...] = p
                    acc_sc[...] = alpha * acc
            return m
        return body

    m = jnp.full((rows, 1), NEG, F32)
    start, left = far0, n_steps
    for n_sub in (4, 2, 1):
        n_trips = left // n_sub
        left = left - n_trips * n_sub
        nxt = start + n_trips * (n_sub * SUB)
        after = jnp.where(left > 0, nxt, tail0)
        m = lax.fori_loop(0, n_trips, trips(start, n_sub, n_trips, after), m)
        start = nxt
    if side_work is not None:
        side_work()
    base = pl.multiple_of(far0 + n_steps * SUB, SUB)
    acc = acc_sc[...] + _pv(p_sc[...], v_sc, base - SUB, SUB)
    m, alpha, p = _softmax_part(m, s_sc[...] + tail[0][2]())
    acc = alpha * acc + _pv(p, v_sc, tail0, SUB)
    return _flash_tail((m, acc), q, k_sc, v_sc, tail[1:])


def _flash_buffers(rows, ncols):
    return [pltpu.VMEM((rows, SUB), F32), pltpu.VMEM((rows, SUB), BF16), pltpu.VMEM((rows, ncols), F32)]


def _flash_tail(carry, q, k_sc, v_sc, tail):
    m, acc = carry
    for r0, width, extra in tail:
        m, alpha, p = _softmax_part(m, _dot_nt(q, k_sc[pl.ds(r0, width), :]) + extra())
        acc = alpha * acc + _pv(p, v_sc, r0, width)
    return acc


def _tile_gate(cond):
    return jnp.where(cond, 0.0, NEG)


def _tail_steps(win, odd_ok, rows, strip, near_ok):
    def first():
        return jnp.concatenate([jnp.broadcast_to(_tile_gate(odd_ok), (rows, TK)),
                                strip(0, TK) + _tile_gate(near_ok)], axis=1)

    return [(win, SUB, first), (win + SUB, SUB, lambda: strip(TK, 3 * TK))]


def _topk_rows(score, idx, k, floor):
    sel = jnp.zeros(score.shape, F32)
    nrow = score.shape[0]
    for _ in range(k):
        mx = jnp.max(score, axis=0, keepdims=True)
        first = jnp.min(jnp.where(score == mx, idx, nrow), axis=0, keepdims=True)
        pick = idx == first
        sel = jnp.where(pick, 1.0, sel)
        score = jnp.where(pick, floor, score)
    return sel


def _nsa_kernel(q_ref, qn_ref, ks_ref, vs_ref, kw_ref, vw_ref, kc_ref, vc_ref, gate_ref,
                ssel_ref, swin_ref, scmp_ref, ov_ref, o_ref,
                kaug, vaug, kwp, vwp, kcp, vcp, qaug_sc, ocw_sc, *bufs, seq):
    g = pl.program_id(1)
    i = pl.program_id(2)
    ncp = seq // CMP_STRIDE
    hd = HEAD_DIM
    rows = NSA_GROUP * TQ

    @pl.when(i == 0)
    def _():
        lane = lax.broadcasted_iota(jnp.int32, (BAND_PAD, LANES), 1)
        padflag = jnp.where(lane == hd, 1.0, 0.0).astype(BF16)
        kaug[0:BAND_PAD, 0:LANES] = jnp.zeros((BAND_PAD, LANES), BF16)
        kaug[0:BAND_PAD, LANES:2 * LANES] = padflag
        kwp[0:BAND_PAD, :] = padflag
        vaug[0:BAND_PAD, :] = jnp.zeros((BAND_PAD, LANES), BF16)
        vwp[0:BAND_PAD, :] = jnp.zeros((BAND_PAD, LANES), BF16)
        onescol = jnp.where(lax.broadcasted_iota(jnp.int32, (TK, LANES), 1) == hd, 1.0, 0.0)
        zero_hd = jnp.zeros((TK, hd), F32)
        col = lax.broadcasted_iota(jnp.int32, (TK, LANES), 1)
        row = lax.broadcasted_iota(jnp.int32, (TK, LANES), 0)

        def fill(c, _):
            src = pl.multiple_of(c * TK, TK)
            dst = pl.multiple_of(BAND_PAD + c * TK, TK)

            def pick(ref):
                both = ref[pl.ds(src, TK), :].astype(F32)
                return jnp.concatenate([jnp.where(g == 0, both[:, :hd], both[:, hd:]), zero_hd], axis=1)

            kaug[pl.ds(dst, TK), 0:LANES] = jnp.where((src + row) // SEL_BLOCK == col, 1.0, 0.0).astype(BF16)
            kaug[pl.ds(dst, TK), LANES:2 * LANES] = pick(ks_ref).astype(BF16)
            kwp[pl.ds(dst, TK), :] = pick(kw_ref).astype(BF16)
            vaug[pl.ds(dst, TK), :] = (pick(vs_ref) + onescol).astype(BF16)
            vwp[pl.ds(dst, TK), :] = (pick(vw_ref) + onescol).astype(BF16)
            return 0

        lax.fori_loop(0, seq // TK, fill, 0)
        lane_c = lax.broadcasted_iota(jnp.int32, (CMP_PAD, LANES), 1)
        kcp[0:CMP_PAD, :] = jnp.where(lane_c == hd, 1.0, 0.0).astype(BF16)
        vcp[0:CMP_PAD, :] = jnp.zeros((CMP_PAD, LANES), BF16)
        zc = jnp.zeros((ncp, hd), F32)
        onesc = jnp.where(lax.broadcasted_iota(jnp.int32, (ncp, LANES), 1) == hd, 1.0, 0.0)
        kcp[CMP_PAD:CMP_PAD + ncp, :] = jnp.concatenate([kc_ref[0, 0], zc], axis=1).astype(BF16)
        vcp[CMP_PAD:CMP_PAD + ncp, :] = (jnp.concatenate([vc_ref[0, 0], zc], axis=1) + onesc).astype(BF16)
        tail = kcp.shape[0] - CMP_PAD - ncp
        kcp[CMP_PAD + ncp:, :] = jnp.zeros((tail, LANES), BF16)
        vcp[CMP_PAD + ncp:, :] = jnp.zeros((tail, LANES), BF16)

    def stack(ref, lo, hi):
        return jnp.concatenate([ref[r, :, lo:hi] for r in range(NSA_GROUP)], axis=0)

    def prepare(qsrc_ref, t, slot):
        qblk = qsrc_ref[...].astype(F32) * QSCALE
        lane_t = lax.broadcasted_iota(jnp.int32, (TQ, hd), 1)
        qtail = jnp.where(lane_t == 0, NEG, 0.0)
        qs = [qblk[:, r * hd:(r + 1) * hd] for r in range(NSA_GROUP)]
        q128 = jnp.concatenate([jnp.concatenate([qs[r], qtail], axis=1) for r in range(NSA_GROUP)],
                               axis=0).astype(BF16)

        s_far = _dot_nt(q128, kcp[CMP_PAD:CMP_PAD + ncp, :])
        c_idx = lax.broadcasted_iota(jnp.int32, (1, ncp), 1)
        far_ok = c_idx < CMP_PER_TQ * t - CMP_PAD
        near0 = pl.multiple_of(CMP_PER_TQ * t, CMP_PER_TQ)
        s_near = _dot_nt(q128, kcp[pl.ds(near0, CMP_NEAR), :]) + stack(scmp_ref, 0, CMP_NEAR)
        near_ok = s_near > 0.5 * NEG
        s_far = jnp.where(far_ok, s_far, NEG)
        m_c = jnp.maximum(jnp.max(s_far, axis=1, keepdims=True), jnp.max(s_near, axis=1, keepdims=True))
        p_far = jnp.where(far_ok, jnp.exp2(s_far - m_c), 0.0)
        p_near = jnp.where(near_ok, jnp.exp2(s_near - m_c), 0.0)
        den = jnp.maximum(jnp.sum(p_far, axis=1, keepdims=True) + jnp.sum(p_near, axis=1, keepdims=True), 1e-30)
        pc_far = p_far / den
        pc_near = p_near / den
        ocw_sc[slot, 0] = (_dot(pc_far.astype(BF16), vcp[CMP_PAD:CMP_PAD + ncp, :])
                           + _dot(pc_near.astype(BF16), vcp[pl.ds(near0, CMP_NEAR), :]))

        band0 = pl.multiple_of(TQ * t, TK) + BAND_PAD - 2 * TK
        steps = [(band0, 2 * TK, lambda: stack(swin_ref, 0, 2 * TK)),
                 (band0 + 2 * TK, TK, lambda: stack(swin_ref, 2 * TK, 3 * TK))]
        acc = _flash_tail((jnp.full((rows, 1), NEG, F32), jnp.zeros((rows, LANES), F32)), q128, kwp, vwp, steps)
        ocw_sc[slot, 1] = acc / acc[:, hd:hd + 1]

        pcs_far = sum(pc_far[r * TQ:(r + 1) * TQ] for r in range(NSA_GROUP))
        pcs_near = sum(pc_near[r * TQ:(r + 1) * TQ] for r in range(NSA_GROUP))

        def hi_lo_dot(p, w):
            hi = p.astype(BF16)
            lo = (p - hi.astype(F32)).astype(BF16)
            return _dot(hi, w) + _dot(lo, w)

        imp = (hi_lo_dot(pcs_far, ov_ref[CMP_PAD:CMP_PAD + ncp, :])
               + hi_lo_dot(pcs_near, ov_ref[pl.ds(near0, CMP_NEAR), :]))
        imp_t = imp.T
        n_idx = lax.broadcasted_iota(jnp.int32, (LANES, TQ), 0)
        cur = (TQ * t + lax.broadcasted_iota(jnp.int32, (LANES, TQ), 1)) // SEL_BLOCK
        eligible = n_idx <= cur
        forced = ((n_idx == 0) | (n_idx == cur) | (n_idx == cur - 1)) & eligible
        rest = jnp.where(forced, -3.0e38, jnp.where(eligible, imp_t, -1.0))
        sel_t = _topk_rows(rest, n_idx, SEL_TOPN - 3, -3.0e38)
        maskneg = jnp.where(forced | (sel_t > 0.5), 0.0, NEG).T
        qaug_sc[slot] = jnp.concatenate(
            [jnp.concatenate([maskneg, qs[r], qtail], axis=1) for r in range(NSA_GROUP)],
            axis=0).astype(BF16)

    @pl.when(i == 0)
    def _():
        prepare(q_ref, 0, 0)

    slot = i % 2
    n_far = jnp.maximum(i - 2, 0)
    win = pl.multiple_of(TQ * i, TK)
    tail = _tail_steps(win, n_far % 2 == 1, rows, functools.partial(stack, ssel_ref), True)
    prepare_next = functools.partial(prepare, qn_ref, jnp.minimum(i + 1, pl.num_programs(2) - 1), 1 - slot)
    acc = _flash_chain(qaug_sc[slot], kaug, vaug, bufs, BAND_PAD, n_far, tail, prepare_next)
    o_sel = acc[:, :hd] / acc[:, hd:hd + 1]

    o_cmp = ocw_sc[slot, 0]
    o_win = ocw_sc[slot, 1]
    sig = jax.nn.sigmoid(gate_ref[...])
    outs = []
    for r in range(NSA_GROUP):
        def gcol(branch):
            c0 = branch * NSA_HEADS + r
            c1 = c0 + NSA_GROUP
            return jnp.where(g == 0, sig[:, c0:c0 + 1], sig[:, c1:c1 + 1])
        sl = slice(r * TQ, (r + 1) * TQ)
        outs.append(gcol(0) * o_cmp[sl, :hd] + gcol(1) * o_sel[sl] + gcol(2) * o_win[sl, :hd])
    o_ref[...] = jnp.concatenate(outs, axis=1).astype(o_ref.dtype)


def _nsa(main, aux, ckv, ssel, swin, scmp, ov, batch, seq):
    nq = seq // TQ
    ncp = seq // CMP_STRIDE
    gw = NSA_GROUP * HEAD_DIM
    once = pl.Buffered(1)
    kv_spec = lambda colblk: pl.BlockSpec((seq, LANES), lambda b, g, i: (b, colblk), pipeline_mode=once)
    strip_spec = lambda cols: pl.BlockSpec((NSA_GROUP, TQ, cols), lambda b, g, i: (g, 0, 0), pipeline_mode=once)
    return pl.pallas_call(
        functools.partial(_nsa_kernel, seq=seq),
        out_shape=jax.ShapeDtypeStruct((batch * seq, NSA_HEADS * HEAD_DIM), BF16),
        grid=(batch, NSA_KV_HEADS, nq),
        in_specs=[pl.BlockSpec((TQ, gw), lambda b, g, i: (b * nq + i, g)),
                  pl.BlockSpec((TQ, gw), lambda b, g, i: (b * nq + jnp.minimum(i + 1, nq - 1), g)),
                  kv_spec(4), kv_spec(5), kv_spec(6), kv_spec(7),
                  pl.BlockSpec((1, 1, ncp, HEAD_DIM), lambda b, g, i: (0, b * NSA_KV_HEADS + g, 0, 0)),
                  pl.BlockSpec((1, 1, ncp, HEAD_DIM), lambda b, g, i: (1, b * NSA_KV_HEADS + g, 0, 0)),
                  pl.BlockSpec((TQ, LANES), lambda b, g, i: (b * nq + i, 0)),
                  strip_spec(3 * TK), strip_spec(3 * TK), strip_spec(CMP_NEAR),
                  pl.BlockSpec(ov.shape, lambda b, g, i: (0, 0))],
        out_specs=pl.BlockSpec((TQ, gw), lambda b, g, i: (b * nq + i, g)),
        scratch_shapes=[pltpu.VMEM((BAND_PAD + seq, 2 * LANES), BF16),
                        pltpu.VMEM((BAND_PAD + seq, LANES), BF16),
                        pltpu.VMEM((BAND_PAD + seq, LANES), BF16),
                        pltpu.VMEM((BAND_PAD + seq, LANES), BF16),
                        pltpu.VMEM((ov.shape[0], LANES), BF16),
                        pltpu.VMEM((ov.shape[0], LANES), BF16),
                        pltpu.VMEM((2, NSA_GROUP * TQ, 2 * LANES), BF16),
                        pltpu.VMEM((2, 2, NSA_GROUP * TQ, LANES), F32),
                        *_flash_buffers(NSA_GROUP * TQ, LANES)],
        compiler_params=_attention_params(),
        name="nsa_attention",
    )(main, main, main, main, main, main, ckv, ckv, aux, ssel, swin, scmp, ov)


def _ones_column():
    return jnp.where(lax.broadcasted_iota(jnp.int32, (TK, LANES), 1) == 0, 1.0, 0.0).astype(BF16)


def _causal_flash(q, k_sc, v_sc, bufs, strip, i):
    n_far = jnp.maximum(2 * i - 1, 0)
    win = pl.multiple_of(i * TQD, SUB)
    tail = _tail_steps(win, i >= 1, q.shape[0], strip, i >= 1)
    return _flash_chain(q, k_sc, v_sc, bufs, FRONT, n_far, tail)


def _diff_kernel(q_ref, k_ref, v_ref, lam_ref, sub_ref, strip_ref, o_ref, k_sc, v_sc, *bufs, lam_init, seq):
    i = pl.program_id(2)
    hd = HEAD_DIM

    @pl.when(i == 0)
    def _():
        k_sc[0:FRONT, :] = jnp.zeros((FRONT, k_sc.shape[1]), BF16)
        v_sc[0:FRONT, :] = jnp.zeros((FRONT, v_sc.shape[1]), BF16)
        ones = _ones_column()

        def fill(c, _):
            r0 = pl.multiple_of(c * TK, TK)
            k_sc[pl.ds(FRONT + r0, TK), :] = k_ref[pl.ds(r0, TK), :]
            v_sc[pl.ds(FRONT + r0, TK), 0:LANES] = v_ref[pl.ds(r0, TK), :]
            v_sc[pl.ds(FRONT + r0, TK), LANES:2 * LANES] = ones
            return 0

        lax.fori_loop(0, seq // TK, fill, 0)

    q = q_ref[...].astype(F32) * QSCALE
    lane = lax.broadcasted_iota(jnp.int32, (TQD, 2 * hd), 1)
    qst = jnp.concatenate([jnp.where(lane < hd, q, 0.0), jnp.where(lane < hd, 0.0, q)],
                          axis=0).astype(BF16)
    strip = lambda lo, hi: jnp.concatenate([strip_ref[0, :, lo:hi]] * 2, axis=0)
    acc = _causal_flash(qst, k_sc, v_sc, bufs, strip, i)
    o = acc[:, :2 * hd] / acc[:, 2 * hd:2 * hd + 1]
    lam_p = lam_ref[...]
    lam = (jnp.exp(jnp.sum(lam_p[0:1] * lam_p[1:2], axis=1, keepdims=True))
           - jnp.exp(jnp.sum(lam_p[2:3] * lam_p[3:4], axis=1, keepdims=True)) + lam_init)
    d = o[:TQD] - lam * o[TQD:]
    o_ref[...] = (_rms(d, sub_ref[...]) * (1.0 - lam_init)).astype(o_ref.dtype)


def _diff(main, lam_p, subln, strips, batch, seq, lam_init):
    nq = seq // TQD
    blk = 2 * HEAD_DIM
    q0, k0, v0 = 1024 // blk, 1536 // blk, 2048 // blk
    return pl.pallas_call(
        functools.partial(_diff_kernel, lam_init=lam_init, seq=seq),
        out_shape=jax.ShapeDtypeStruct((batch * seq, DIFF_HEADS * blk), BF16),
        grid=(batch, DIFF_HEADS, nq),
        in_specs=[pl.BlockSpec((TQD, blk), lambda b, h, i: (b * nq + i, q0 + h)),
                  pl.BlockSpec((seq, blk), lambda b, h, i: (b, k0 + h)),
                  pl.BlockSpec((seq, blk), lambda b, h, i: (b, v0 + h)),
                  pl.BlockSpec((4, HEAD_DIM), lambda b, h, i: (0, 0)),
                  pl.BlockSpec((1, blk), lambda b, h, i: (0, 0)),
                  pl.BlockSpec((1, TQD, 3 * TK), lambda b, h, i: (NSA_HEADS + h, 0, 0))],
        out_specs=pl.BlockSpec((TQD, blk), lambda b, h, i: (b * nq + i, h)),
        scratch_shapes=[pltpu.VMEM((FRONT + seq, LANES), BF16), pltpu.VMEM((FRONT + seq, 2 * LANES), BF16),
                        *_flash_buffers(2 * TQD, 2 * LANES)],
        compiler_params=_attention_params(),
        name="diff_attention",
    )(main, main, main, lam_p, subln.reshape(1, blk), strips)


def _moba_kernel(q_ref, k_ref, v_ref, strip_ref, o_ref, kaug, va_sc, vb_sc, kmean, *bufs, seq):
    i = pl.program_id(2)
    hd = HEAD_DIM
    nb = seq // MOBA_BLOCK

    @pl.when(i == 0)
    def _():
        kmean[...] = jnp.zeros(kmean.shape, F32)
        kaug[0:FRONT, :] = jnp.zeros((FRONT, kaug.shape[1]), BF16)
        va_sc[0:FRONT, :] = jnp.zeros((FRONT, LANES), BF16)
        vb_sc[0:FRONT, :] = jnp.zeros((FRONT, LANES), BF16)
        col = lax.broadcasted_iota(jnp.int32, (MOBA_BLOCK, LANES), 1)
        ones_tail = jnp.where(lax.broadcasted_iota(jnp.int32, (MOBA_BLOCK, hd), 1) == 0, 1.0, 0.0).astype(BF16)

        def fill(n, _):
            r0 = pl.multiple_of(n * MOBA_BLOCK, MOBA_BLOCK)
            kb = k_ref[pl.ds(r0, MOBA_BLOCK), :]
            vb = v_ref[pl.ds(r0, MOBA_BLOCK), :]
            kaug[pl.ds(FRONT + r0, MOBA_BLOCK), 0:LANES] = jnp.where(col == n, 1.0, 0.0).astype(BF16)
            kaug[pl.ds(FRONT + r0, MOBA_BLOCK), LANES:2 * LANES] = kb
            va_sc[pl.ds(FRONT + r0, MOBA_BLOCK), :] = jnp.concatenate([vb[:, :hd], ones_tail], axis=1)
            vb_sc[pl.ds(FRONT + r0, MOBA_BLOCK), :] = jnp.concatenate([vb[:, hd:], ones_tail], axis=1)
            kmean[pl.ds(n, 1), :] = jnp.mean(kb.astype(F32), axis=0, keepdims=True)
            return 0

        lax.fori_loop(0, nb, fill, 0)

    qblk = q_ref[...]
    qf = qblk.astype(F32) * QSCALE
    lane = lax.broadcasted_iota(jnp.int32, (TQD, LANES), 1)
    n_idx = lax.broadcasted_iota(jnp.int32, (nb, TQD), 0)
    own = (i * TQD + lax.broadcasted_iota(jnp.int32, (nb, TQD), 1)) // MOBA_BLOCK
    past = n_idx < own
    km = kmean[0:nb, :]
    lane_k = lax.broadcasted_iota(jnp.int32, km.shape, 1)
    no_block = jnp.full((LANES - nb, TQD), NEG, F32)
    qrows = []
    for hh in range(2):
        mine = (lane < hd) if hh == 0 else (lane >= hd)
        mine_k = (lane_k < hd) if hh == 0 else (lane_k >= hd)
        gate_t = _dot_nt(jnp.where(mine_k, km, 0.0).astype(BF16), qblk)
        sel_t = _topk_rows(jnp.where(past, gate_t, NEG), n_idx, MOBA_TOPK, -3.0e38)
        allowed = ((sel_t > 0.5) & past) | (n_idx == own)
        maskneg = jnp.concatenate([jnp.where(allowed, 0.0, NEG), no_block], axis=0).T
        qrows.append(jnp.concatenate([maskneg, jnp.where(mine, qf, 0.0)], axis=1))
    qaug = jnp.concatenate(qrows, axis=0).astype(BF16)
    strip = lambda lo, hi: jnp.concatenate([strip_ref[0, :, lo:hi], strip_ref[1, :, lo:hi]], axis=0)
    acc = _causal_flash(qaug, kaug, (va_sc, vb_sc), bufs, strip, i)
    o = acc[:, :hd] / acc[:, hd:hd + 1]
    o_ref[...] = jnp.concatenate([o[:TQD], o[TQD:]], axis=1).astype(o_ref.dtype)


def _moba(main, strips, batch, seq):
    nq = seq // TQD
    pairs = MOBA_HEADS // 2
    return pl.pallas_call(
        functools.partial(_moba_kernel, seq=seq),
        out_shape=jax.ShapeDtypeStruct((batch * seq, MOBA_HEADS * HEAD_DIM), BF16),
        grid=(batch, pairs, nq),
        in_specs=[pl.BlockSpec((TQD, LANES), lambda b, h, i: (b * nq + i, h)),
                  pl.BlockSpec((seq, LANES), lambda b, h, i: (b, pairs + h)),
                  pl.BlockSpec((seq, LANES), lambda b, h, i: (b, 2 * pairs + h)),
                  pl.BlockSpec((2, TQD, 3 * TK), lambda b, h, i: (h, 0, 0))],
        out_specs=pl.BlockSpec((TQD, LANES), lambda b, h, i: (b * nq + i, h)),
        scratch_shapes=[pltpu.VMEM((FRONT + seq, 2 * LANES), BF16), pltpu.VMEM((FRONT + seq, LANES), BF16),
                        pltpu.VMEM((FRONT + seq, LANES), BF16),
                        pltpu.VMEM((LANES, LANES), F32), *_flash_buffers(2 * TQD, LANES)],
        compiler_params=_attention_params(),
        name="moba_attention",
    )(main, main, main, strips)


def _cmp_strip_geometry():
    r = np.arange(TQ)[:, None]
    cc = np.arange(CMP_NEAR)[None, :]
    rel_c = r + CMP_STRIDE * CMP_PAD - (CMP_BLOCK - 1) - CMP_STRIDE * cc
    return rel_c, rel_c >= 0


def _overlap_padded(seq):
    nc = (seq - CMP_BLOCK) // CMP_STRIDE + 1
    ncp = seq // CMP_STRIDE
    tok = np.arange(nc)[:, None] * CMP_STRIDE + np.arange(CMP_BLOCK)[None, :]
    ov = np.zeros((CMP_PAD + ncp + 2 * CMP_PER_TQ, LANES), np.float32)
    np.add.at(ov, (CMP_PAD + np.repeat(np.arange(nc), CMP_BLOCK), (tok // SEL_BLOCK).reshape(-1)),
              1.0 / CMP_BLOCK)
    return jnp.asarray(ov, dtype=BF16)


def _strips(bias_table):
    rel_c, valid_c = _cmp_strip_geometry()
    far = 1 << 30
    return (_toeplitz_strips(bias_table, TQD, 3 * TK, TK, 0, far, MOBA_HEADS, "bias_strip_diag"),
            _toeplitz_strips(bias_table, TQ, 3 * TK, 2 * TK, 0, far, NSA_HEADS, "bias_strip_sel"),
            _toeplitz_strips(bias_table, TQ, 3 * TK, 2 * TK, 0, WINDOW, NSA_HEADS, "bias_strip_win"),
            _bias_strips(bias_table, rel_c, valid_c, 0, NSA_HEADS, "bias_strip_cmp"))


def _even_heads(xf, batch, seq, bias_table, g_mix, w_in, pos_k, w1_k, w2_k, pos_v, w1_v, w2_v,
                lam_q1, lam_k1, lam_q2, lam_k2, subln, strips=None):
    d = xf.shape[1]
    strip_a, strip_sel, strip_win, strip_cmp = strips if strips is not None else _strips(bias_table)
    w_main = jnp.concatenate([w_in[:, 0:512], w_in[:, 768:1280], w_in[:, 1304:2840]], axis=1).astype(BF16)
    w_aux = jnp.concatenate([w_in[:, 512:768], w_in[:, 1280:1304],
                             jnp.zeros((d, 3 * LANES - 280), F32)], axis=1).astype(BF16)
    main0, kc0, kc1, vc0, vc1, gates = _in_proj_even(xf, g_mix, w_main, w_aux, 1024, w_main.shape[1] // 2)

    nch = seq // CMP_STRIDE
    chunks = lambda a: a.reshape(batch, nch, CMP_STRIDE * HEAD_DIM)
    pos = jnp.stack([pos_k, pos_v]).reshape(2, 2, CMP_STRIDE * HEAD_DIM)
    ckv = _compress(chunks(kc0), chunks(kc1), chunks(vc0), chunks(vc1), pos,
                    jnp.stack([w1_k, w1_v]).astype(BF16), jnp.stack([w2_k, w2_v]).astype(BF16))

    o_a = _nsa(main0, gates, ckv, strip_sel, strip_win, strip_cmp, _overlap_padded(seq), batch, seq)
    lam_p = jnp.stack([lam_q1, lam_k1, lam_q2, lam_k2])
    lam_init = 0.8 - 0.6 * math.exp(-0.3 * 0)
    o_b = _diff(main0, lam_p, subln, strip_a, batch, seq, lam_init)
    return o_a, o_b, ckv


def _odd_heads(xf, batch, seq, bias_table, g_mix, w_in, strips=None):
    strip_a = (strips if strips is not None else _strips(bias_table))[0]
    main1 = _norm_matmul(xf, g_mix, w_in.astype(BF16), BF16, 1024, w_in.shape[1] // 2, "in_proj_odd")
    return _moba(main1, strip_a, batch, seq)


def kernel(x, bias_table, norm_mix, norm_mlp, norm_final, mlp_w1, mlp_w2, ev_w_in, ev_w_out,
           ev_cmp_pos_k, ev_cmp_w1_k, ev_cmp_w2_k, ev_cmp_pos_v, ev_cmp_w1_v, ev_cmp_w2_v,
           ev_lam_q1, ev_lam_k1, ev_lam_q2, ev_lam_k2, ev_subln, od_w_in, od_w_out):
    batch, seq, d = x.shape
    assert d == D_MODEL and (batch * seq) % 1024 == 0 and seq % TQD == 0
    assert SEL_TOPN * SEL_BLOCK <= seq <= LANES * SEL_BLOCK and seq >= MOBA_TOPK * MOBA_BLOCK
    xf = x.reshape(batch * seq, d)
    strips = _strips(bias_table)

    o_a, o_b, _ = _even_heads(xf, batch, seq, bias_table, norm_mix[0], ev_w_in[0],
                              ev_cmp_pos_k[0], ev_cmp_w1_k[0], ev_cmp_w2_k[0],
                              ev_cmp_pos_v[0], ev_cmp_w1_v[0], ev_cmp_w2_v[0],
                              ev_lam_q1[0], ev_lam_k1[0], ev_lam_q2[0], ev_lam_k2[0], ev_subln[0], strips)
    wo = ev_w_out[0].astype(BF16)
    x1 = _post(xf, o_a, o_b, 0, wo[:512], wo[512:], norm_mlp[0], mlp_w1[0].astype(BF16),
               mlp_w2[0].astype(BF16), norm_final, False, 1024, 1024, "post_even")

    o_m = _odd_heads(x1, batch, seq, bias_table, norm_mix[1], od_w_in[0], strips)
    wo = od_w_out[0].astype(BF16)
    out = _post(x1, o_m, o_m, 1, wo[:512], wo[512:], norm_mlp[1], mlp_w1[1].astype(BF16),
                mlp_w2[1].astype(BF16), norm_final, True, 1024, 1024, "post_odd")
    return out.reshape(batch, seq, d)
```

```python
import functools
import math

import numpy as np
import jax
import jax.numpy as jnp
from jax import lax
from jax.experimental import pallas as pl
from jax.experimental.pallas import tpu as pltpu

F32 = jnp.float32
BF16 = jnp.bfloat16

D_MODEL = 1024
HEAD_DIM = 64
EPS = 1e-6
NEG = -1e30
NUM_BUCKETS = 32
MAX_DISTANCE = 128
NSA_HEADS = 8
NSA_KV_HEADS = 2
NSA_GROUP = 4
CMP_BLOCK = 32
CMP_STRIDE = 16
SEL_BLOCK = 64
SEL_TOPN = 16
WINDOW = 512
FORCE_BONUS = 1e3
DIFF_HEADS = 4
MOBA_HEADS = 16
MOBA_BLOCK = 256
MOBA_TOPK = 3
D_FF = 4 * D_MODEL
SCALE = HEAD_DIM ** -0.5

LANES = 128
TQ = 256
TQD = 512
DIFF_TILES_PER_STEP = 2
TK = 256
FAR_REL = 113
CMP_PER_TQ = TQ // CMP_STRIDE
CMP_NEAR = 128
CMP_PAD = CMP_NEAR - 2 * CMP_PER_TQ
SUB = 2 * TK
BAND_PAD = 3 * TK
FRONT = 2 * TK
LOG2E = math.log2(math.e)
QSCALE = SCALE * LOG2E
VMEM_LIMIT = 56 * 1024 * 1024


def _attention_params():
    return pltpu.CompilerParams(dimension_semantics=("arbitrary", "arbitrary", "arbitrary"),
                                vmem_limit_bytes=VMEM_LIMIT)


def _dot_nt(a, b):
    return lax.dot_general(a, b, (((1,), (1,)), ((), ())), preferred_element_type=F32)


def _dot(a, b):
    return jnp.dot(a, b, preferred_element_type=F32)


def _rms(x, g):
    return x * lax.rsqrt(jnp.mean(x * x, axis=-1, keepdims=True) + EPS) * g


def _t5_bucket_np(rel):
    n = np.maximum(rel, 0)
    max_exact = NUM_BUCKETS // 2
    large = max_exact + (np.log(np.maximum(n, 1).astype(np.float32) / np.float32(max_exact))
                         / np.float32(math.log(MAX_DISTANCE / max_exact))
                         * np.float32(NUM_BUCKETS - max_exact)).astype(np.int32)
    large = np.minimum(large, NUM_BUCKETS - 1)
    return np.where(n < max_exact, n, large).astype(np.int32)


def _strip_kernel(tbl_ref, bucket_ref, valid_ref, o_ref, *, head0):
    h = pl.program_id(0) + head0
    bucket = bucket_ref[...]
    last = tbl_ref[NUM_BUCKETS - 1, h]
    acc = jnp.zeros(bucket.shape, F32)
    for b in range(NUM_BUCKETS - 1):
        acc = jnp.where(bucket == b, tbl_ref[b, h] - last, acc)
    o_ref[0] = jnp.where(valid_ref[...] != 0, acc * LOG2E, NEG)


def _toeplitz_kernel(tbl_ref, bucket_ref, valid_ref, o_ref, *, rows, cols):
    h = pl.program_id(0)
    bucket = bucket_ref[...]
    last = tbl_ref[NUM_BUCKETS - 1, h]
    acc = jnp.zeros(bucket.shape, F32)
    for b in range(NUM_BUCKETS - 1):
        acc = jnp.where(bucket == b, tbl_ref[b, h] - last, acc)
    line = jnp.where(valid_ref[...] != 0, acc * LOG2E, NEG)
    wide = jnp.broadcast_to(line[0:1], (rows, line.shape[1]))
    o_ref[0] = pltpu.roll(wide, 0, 1, stride=1, stride_axis=0)[:, :cols]


def _toeplitz_strips(table, rows, cols, off, lo, hi, nheads, name):
    width = -(-(rows + cols) // LANES) * LANES
    j = np.arange(width)
    rel = off - np.where(j < cols, j, j - width)
    valid = (rel >= lo) & (rel < hi)
    tile8 = lambda a: jnp.asarray(np.broadcast_to(a[None, :], (8, width)).astype(np.int32))
    return pl.pallas_call(
        functools.partial(_toeplitz_kernel, rows=rows, cols=cols),
        out_shape=jax.ShapeDtypeStruct((nheads, rows, cols), F32),
        grid=(nheads,),
        in_specs=[pl.BlockSpec(memory_space=pltpu.SMEM),
                  pl.BlockSpec((8, width), lambda h: (0, 0)),
                  pl.BlockSpec((8, width), lambda h: (0, 0))],
        out_specs=pl.BlockSpec((1, rows, cols), lambda h: (h, 0, 0)),
        name=name,
    )(table, tile8(_t5_bucket_np(rel)), tile8(valid))


def _bias_strips(table, rel, valid, head0, nheads, name):
    rows, cols = rel.shape
    bucket = jnp.asarray(_t5_bucket_np(rel))
    validi = jnp.asarray(valid.astype(np.int32))
    return pl.pallas_call(
        functools.partial(_strip_kernel, head0=head0),
        out_shape=jax.ShapeDtypeStruct((nheads, rows, cols), F32),
        grid=(nheads,),
        in_specs=[pl.BlockSpec(memory_space=pltpu.SMEM),
                  pl.BlockSpec((rows, cols), lambda h: (0, 0)),
                  pl.BlockSpec((rows, cols), lambda h: (0, 0))],
        out_specs=pl.BlockSpec((1, rows, cols), lambda h: (h, 0, 0)),
        name=name,
    )(table, bucket, validi)


def _norm_matmul_kernel(x_ref, g_ref, w_ref, o_ref, xn_ref):
    @pl.when(pl.program_id(1) == 0)
    def _():
        xn_ref[...] = _rms(x_ref[...], g_ref[...]).astype(BF16)

    o_ref[...] = _dot(xn_ref[...], w_ref[...]).astype(o_ref.dtype)


def _norm_matmul(x, g, w, out_dtype, tm, tn, name):
    m, d = x.shape
    n = w.shape[1]
    return pl.pallas_call(
        _norm_matmul_kernel,
        out_shape=jax.ShapeDtypeStruct((m, n), out_dtype),
        grid=(m // tm, n // tn),
        in_specs=[pl.BlockSpec((tm, d), lambda i, j: (i, 0)),
                  pl.BlockSpec((1, d), lambda i, j: (0, 0)),
                  pl.BlockSpec((d, tn), lambda i, j: (0, j))],
        out_specs=pl.BlockSpec((tm, tn), lambda i, j: (i, j)),
        scratch_shapes=[pltpu.VMEM((tm, d), BF16)],
        compiler_params=pltpu.CompilerParams(
            dimension_semantics=("arbitrary", "arbitrary"), vmem_limit_bytes=VMEM_LIMIT),
        name=name,
    )(x, g.reshape(1, d), w)


def _in_proj_even_kernel(x_ref, g_ref, w_ref, wa_ref, o_ref, kc0_ref, kc1_ref, vc0_ref, vc1_ref, gate_ref,
                         xn_ref):
    hd = HEAD_DIM

    @pl.when(pl.program_id(1) == 0)
    def _():
        xn = _rms(x_ref[...], g_ref[...]).astype(BF16)
        xn_ref[...] = xn
        aux = _dot(xn, wa_ref[...])
        kc0_ref[...] = aux[:, 0:hd]
        kc1_ref[...] = aux[:, hd:2 * hd]
        vc0_ref[...] = aux[:, 2 * hd:3 * hd]
        vc1_ref[...] = aux[:, 3 * hd:4 * hd]
        gate_ref[...] = aux[:, 4 * hd:]

    o_ref[...] = _dot(xn_ref[...], w_ref[...]).astype(o_ref.dtype)


def _in_proj_even(x, g, w_main, w_aux, tm, tn):
    m, d = x.shape
    n = w_main.shape[1]
    na = w_aux.shape[1]
    row_spec = lambda width: pl.BlockSpec((tm, width), lambda i, j: (i, 0))
    cmp_shape = jax.ShapeDtypeStruct((m, HEAD_DIM), F32)
    return pl.pallas_call(
        _in_proj_even_kernel,
        out_shape=(jax.ShapeDtypeStruct((m, n), BF16), cmp_shape, cmp_shape, cmp_shape, cmp_shape,
                   jax.ShapeDtypeStruct((m, na - 4 * HEAD_DIM), F32)),
        grid=(m // tm, n // tn),
        in_specs=[row_spec(d),
                  pl.BlockSpec((1, d), lambda i, j: (0, 0)),
                  pl.BlockSpec((d, tn), lambda i, j: (0, j)),
                  pl.BlockSpec((d, na), lambda i, j: (0, 0))],
        out_specs=(pl.BlockSpec((tm, tn), lambda i, j: (i, j)), row_spec(HEAD_DIM), row_spec(HEAD_DIM),
                   row_spec(HEAD_DIM), row_spec(HEAD_DIM), row_spec(na - 4 * HEAD_DIM)),
        scratch_shapes=[pltpu.VMEM((tm, d), BF16)],
        compiler_params=pltpu.CompilerParams(
            dimension_semantics=("arbitrary", "arbitrary"), vmem_limit_bytes=VMEM_LIMIT),
        name="in_proj_even",
    )(x, g.reshape(1, d), w_main, w_aux)


def _post_kernel(x_ref, a0_ref, a1_ref, wo0_ref, wo1_ref, g_ref, w1_ref, w2_ref, gf_ref,
                 o_ref, acc_ref, xn_ref, *, final_norm):
    f = pl.program_id(1)

    @pl.when(f == 0)
    def _():
        x1 = x_ref[...] + _dot(a0_ref[...], wo0_ref[...]) + _dot(a1_ref[...], wo1_ref[...])
        acc_ref[...] = x1
        xn_ref[...] = _rms(x1, g_ref[...]).astype(BF16)

    h = jnp.square(jnp.maximum(_dot(xn_ref[...], w1_ref[...]), 0.0))
    acc_ref[...] += _dot(h.astype(BF16), w2_ref[...])

    @pl.when(f == pl.num_programs(1) - 1)
    def _():
        y = acc_ref[...]
        if final_norm:
            y = _rms(y, gf_ref[...])
        o_ref[...] = y


def _post(x, a0, a1, a1_colblk, wo0, wo1, g, w1, w2, gf, final_norm, tm, tf, name):
    m, d = x.shape
    k0 = wo0.shape[0]
    k1 = wo1.shape[0]
    ff = w1.shape[1]
    return pl.pallas_call(
        functools.partial(_post_kernel, final_norm=final_norm),
        out_shape=jax.ShapeDtypeStruct((m, d), F32),
        grid=(m // tm, ff // tf),
        in_specs=[pl.BlockSpec((tm, d), lambda i, f: (i, 0)),
                  pl.BlockSpec((tm, k0), lambda i, f: (i, 0)),
                  pl.BlockSpec((tm, k1), lambda i, f: (i, a1_colblk)),
                  pl.BlockSpec((k0, d), lambda i, f: (0, 0)),
                  pl.BlockSpec((k1, d), lambda i, f: (0, 0)),
                  pl.BlockSpec((1, d), lambda i, f: (0, 0)),
                  pl.BlockSpec((d, tf), lambda i, f: (0, f)),
                  pl.BlockSpec((tf, d), lambda i, f: (f, 0)),
                  pl.BlockSpec((1, d), lambda i, f: (0, 0))],
        out_specs=pl.BlockSpec((tm, d), lambda i, f: (i, 0)),
        scratch_shapes=[pltpu.VMEM((tm, d), F32), pltpu.VMEM((tm, d), BF16)],
        compiler_params=pltpu.CompilerParams(
            dimension_semantics=("arbitrary", "arbitrary"), vmem_limit_bytes=VMEM_LIMIT),
        name=name,
    )(x, a0, a1, wo0, wo1, g.reshape(1, d), w1, w2, gf.reshape(1, d))


def _compress_kernel(kc0_ref, kc1_ref, vc0_ref, vc1_ref, pos_ref, w1_ref, w2_ref, o_ref):
    half = CMP_STRIDE * HEAD_DIM
    for kv, refs in enumerate(((kc0_ref, kc1_ref), (vc0_ref, vc1_ref))):
        w1a = w1_ref[kv, :half, :]
        w1b = w1_ref[kv, half:, :]
        pos = pos_ref[kv].astype(BF16)
        pa = jnp.broadcast_to(pos[0:1], (8, half))
        pb = jnp.broadcast_to(pos[1:2], (8, half))
        posterm = (_dot(pa, w1a) + _dot(pb, w1b))[0:1]
        for grp, x_ref in enumerate(refs):
            x = x_ref[0].astype(BF16)
            first = _dot(x, w1a)
            second = _dot(x, w1b)
            second = pltpu.roll(second, second.shape[0] - 1, 0)
            hid = jax.nn.gelu(first + second + posterm)
            o_ref[kv, grp] = _dot(hid.astype(BF16), w2_ref[kv])


def _compress(kc0, kc1, vc0, vc1, pos, w1, w2):
    batch, nch, wide = kc0.shape
    x_spec = pl.BlockSpec((1, nch, wide), lambda b: (b, 0, 0))
    full = lambda a: pl.BlockSpec(a.shape, lambda b: (0,) * a.ndim)
    return pl.pallas_call(
        _compress_kernel,
        out_shape=jax.ShapeDtypeStruct((2, batch * NSA_KV_HEADS, nch, HEAD_DIM), F32),
        grid=(batch,),
        in_specs=[x_spec, x_spec, x_spec, x_spec, full(pos), full(w1), full(w2)],
        out_specs=pl.BlockSpec((2, NSA_KV_HEADS, nch, HEAD_DIM), lambda b: (0, b, 0, 0)),
        compiler_params=pltpu.CompilerParams(
            dimension_semantics=("arbitrary",), vmem_limit_bytes=VMEM_LIMIT),
        name="nsa_compress",
    )(kc0, kc1, vc0, vc1, pos, w1, w2)


def _softmax_part(m, s):
    m_new = jnp.maximum(m, jnp.max(s, axis=1, keepdims=True))
    return m_new, jnp.exp2(m - m_new), jnp.exp2(s - m_new).astype(BF16)


def _pv(p, v_sc, r0, width):
    if not isinstance(v_sc, (tuple, list)):
        return _dot(p, v_sc[pl.ds(r0, width), :])
    share = p.shape[0] // len(v_sc)
    return jnp.concatenate([_dot(p[g * share:(g + 1) * share], v[pl.ds(r0, width), :])
                            for g, v in enumerate(v_sc)], axis=0)


def _flash_chain(q, k_sc, v_sc, bufs, far0, n_far, tail, side_work=None):
    s_sc, p_sc, acc_sc = bufs
    rows = q.shape[0]
    assert SUB == 2 * TK and tail[0][1] == SUB
    tail0 = tail[0][0]

    def qk(r0):
        return _dot_nt(q, k_sc[pl.ds(pl.multiple_of(r0, TK), SUB), :])

    n_steps = n_far // 2
    s_sc[...] = qk(jnp.where(n_steps > 0, far0, tail0))
    p_sc[...] = jnp.zeros(p_sc.shape, BF16)
    acc_sc[...] = jnp.zeros(acc_sc.shape, F32)

    def trips(start, n_sub, n_trips, after):
        def body(j, m):
            base = pl.multiple_of(start + j * (n_sub * SUB), SUB)
            acc = acc_sc[...] + _pv(p_sc[...], v_sc, base - SUB, SUB)
            s_cur = s_sc[...]
            for u in range(n_sub):
                m, alpha, p = _softmax_part(m, s_cur)
                if u + 1 < n_sub:
                    s_cur = qk(base + (u + 1) * SUB)
                    acc = alpha * acc + _pv(p, v_sc, base + u * SUB, SUB)
                else:
                    s_sc[...] = qk(jnp.where(j + 1 < n_trips, base + n_sub * SUB, after))
                    p_sc[...] = p
                    acc_sc[...] = alpha * acc
            return m
        return body

    m = jnp.full((rows, 1), NEG, F32)
    start, left = far0, n_steps
    for n_sub in (4, 2, 1):
        n_trips = left // n_sub
        left = left - n_trips * n_sub
        nxt = start + n_trips * (n_sub * SUB)
        after = jnp.where(left > 0, nxt, tail0)
        m = lax.fori_loop(0, n_trips, trips(start, n_sub, n_trips, after), m)
        start = nxt
    if side_work is not None:
        side_work()
    base = pl.multiple_of(far0 + n_steps * SUB, SUB)
    acc = acc_sc[...] + _pv(p_sc[...], v_sc, base - SUB, SUB)
    m, alpha, p = _softmax_part(m, s_sc[...] + tail[0][2]())
    acc = alpha * acc + _pv(p, v_sc, tail0, SUB)
    return _flash_tail((m, acc), q, k_sc, v_sc, tail[1:])


def _flash_buffers(rows, ncols):
    return [pltpu.VMEM((rows, SUB), F32), pltpu.VMEM((rows, SUB), BF16), pltpu.VMEM((rows, ncols), F32)]


def _flash_tail(carry, q, k_sc, v_sc, tail):
    m, acc = carry
    for r0, width, extra in tail:
        m, alpha, p = _softmax_part(m, _dot_nt(q, k_sc[pl.ds(r0, width), :]) + extra())
        acc = alpha * acc + _pv(p, v_sc, r0, width)
    return acc


def _tile_gate(cond):
    return jnp.where(cond, 0.0, NEG)


def _tail_steps(win, odd_ok, rows, strip, near_ok):
    def first():
        return jnp.concatenate([jnp.broadcast_to(_tile_gate(odd_ok), (rows, TK)),
                                strip(0, TK) + _tile_gate(near_ok)], axis=1)

    return [(win, SUB, first), (win + SUB, SUB, lambda: strip(TK, 3 * TK))]


def _topk_rows(score, idx, k, floor):
    sel = jnp.zeros(score.shape, F32)
    nrow = score.shape[0]
    for _ in range(k):
        mx = jnp.max(score, axis=0, keepdims=True)
        first = jnp.min(jnp.where(score == mx, idx, nrow), axis=0, keepdims=True)
        pick = idx == first
        sel = jnp.where(pick, 1.0, sel)
        score = jnp.where(pick, floor, score)
    return sel


def _nsa_kernel(q_ref, qn_ref, ks_ref, vs_ref, kw_ref, vw_ref, kc_ref, vc_ref, gate_ref,
                ssel_ref, swin_ref, scmp_ref, ov_ref, o_ref,
                kaug, vaug, kwp, vwp, kcp, vcp, qaug_sc, ocw_sc, *bufs, seq):
    g = pl.program_id(1)
    i = pl.program_id(2)
    ncp = seq // CMP_STRIDE
    hd = HEAD_DIM
    rows = NSA_GROUP * TQ

    @pl.when(i == 0)
    def _():
        lane = lax.broadcasted_iota(jnp.int32, (BAND_PAD, LANES), 1)
        padflag = jnp.where(lane == hd, 1.0, 0.0).astype(BF16)
        kaug[0:BAND_PAD, 0:LANES] = jnp.zeros((BAND_PAD, LANES), BF16)
        kaug[0:BAND_PAD, LANES:2 * LANES] = padflag
        kwp[0:BAND_PAD, :] = padflag
        vaug[0:BAND_PAD, :] = jnp.zeros((BAND_PAD, LANES), BF16)
        vwp[0:BAND_PAD, :] = jnp.zeros((BAND_PAD, LANES), BF16)
        onescol = jnp.where(lax.broadcasted_iota(jnp.int32, (TK, LANES), 1) == hd, 1.0, 0.0)
        zero_hd = jnp.zeros((TK, hd), F32)
        col = lax.broadcasted_iota(jnp.int32, (TK, LANES), 1)
        row = lax.broadcasted_iota(jnp.int32, (TK, LANES), 0)

        def fill(c, _):
            src = pl.multiple_of(c * TK, TK)
            dst = pl.multiple_of(BAND_PAD + c * TK, TK)

            def pick(ref):
                both = ref[pl.ds(src, TK), :].astype(F32)
                return jnp.concatenate([jnp.where(g == 0, both[:, :hd], both[:, hd:]), zero_hd], axis=1)

            kaug[pl.ds(dst, TK), 0:LANES] = jnp.where((src + row) // SEL_BLOCK == col, 1.0, 0.0).astype(BF16)
            kaug[pl.ds(dst, TK), LANES:2 * LANES] = pick(ks_ref).astype(BF16)
            kwp[pl.ds(dst, TK), :] = pick(kw_ref).astype(BF16)
            vaug[pl.ds(dst, TK), :] = (pick(vs_ref) + onescol).astype(BF16)
            vwp[pl.ds(dst, TK), :] = (pick(vw_ref) + onescol).astype(BF16)
            return 0

        lax.fori_loop(0, seq // TK, fill, 0)
        lane_c = lax.broadcasted_iota(jnp.int32, (CMP_PAD, LANES), 1)
        kcp[0:CMP_PAD, :] = jnp.where(lane_c == hd, 1.0, 0.0).astype(BF16)
        vcp[0:CMP_PAD, :] = jnp.zeros((CMP_PAD, LANES), BF16)
        zc = jnp.zeros((ncp, hd), F32)
        onesc = jnp.where(lax.broadcasted_iota(jnp.int32, (ncp, LANES), 1) == hd, 1.0, 0.0)
        kcp[CMP_PAD:CMP_PAD + ncp, :] = jnp.concatenate([kc_ref[0, 0], zc], axis=1).astype(BF16)
        vcp[CMP_PAD:CMP_PAD + ncp, :] = (jnp.concatenate([vc_ref[0, 0], zc], axis=1) + onesc).astype(BF16)
        tail = kcp.shape[0] - CMP_PAD - ncp
        kcp[CMP_PAD + ncp:, :] = jnp.zeros((tail, LANES), BF16)
        vcp[CMP_PAD + ncp:, :] = jnp.zeros((tail, LANES), BF16)

    def stack(ref, lo, hi):
        return jnp.concatenate([ref[r, :, lo:hi] for r in range(NSA_GROUP)], axis=0)

    def prepare(qsrc_ref, t, slot):
        qblk = qsrc_ref[...].astype(F32) * QSCALE
        lane_t = lax.broadcasted_iota(jnp.int32, (TQ, hd), 1)
        qtail = jnp.where(lane_t == 0, NEG, 0.0)
        qs = [qblk[:, r * hd:(r + 1) * hd] for r in range(NSA_GROUP)]
        q128 = jnp.concatenate([jnp.concatenate([qs[r], qtail], axis=1) for r in range(NSA_GROUP)],
                               axis=0).astype(BF16)

        s_far = _dot_nt(q128, kcp[CMP_PAD:CMP_PAD + ncp, :])
        c_idx = lax.broadcasted_iota(jnp.int32, (1, ncp), 1)
        far_ok = c_idx < CMP_PER_TQ * t - CMP_PAD
        near0 = pl.multiple_of(CMP_PER_TQ * t, CMP_PER_TQ)
        s_near = _dot_nt(q128, kcp[pl.ds(near0, CMP_NEAR), :]) + stack(scmp_ref, 0, CMP_NEAR)
        near_ok = s_near > 0.5 * NEG
        s_far = jnp.where(far_ok, s_far, NEG)
        m_c = jnp.maximum(jnp.max(s_far, axis=1, keepdims=True), jnp.max(s_near, axis=1, keepdims=True))
        p_far = jnp.where(far_ok, jnp.exp2(s_far - m_c), 0.0)
        p_near = jnp.where(near_ok, jnp.exp2(s_near - m_c), 0.0)
        den = jnp.maximum(jnp.sum(p_far, axis=1, keepdims=True) + jnp.sum(p_near, axis=1, keepdims=True), 1e-30)
        pc_far = p_far / den
        pc_near = p_near / den
        ocw_sc[slot, 0] = (_dot(pc_far.astype(BF16), vcp[CMP_PAD:CMP_PAD + ncp, :])
                           + _dot(pc_near.astype(BF16), vcp[pl.ds(near0, CMP_NEAR), :]))

        band0 = pl.multiple_of(TQ * t, TK) + BAND_PAD - 2 * TK
        steps = [(band0, 2 * TK, lambda: stack(swin_ref, 0, 2 * TK)),
                 (band0 + 2 * TK, TK, lambda: stack(swin_ref, 2 * TK, 3 * TK))]
        acc = _flash_tail((jnp.full((rows, 1), NEG, F32), jnp.zeros((rows, LANES), F32)), q128, kwp, vwp, steps)
        ocw_sc[slot, 1] = acc / acc[:, hd:hd + 1]

        pcs_far = sum(pc_far[r * TQ:(r + 1) * TQ] for r in range(NSA_GROUP))
        pcs_near = sum(pc_near[r * TQ:(r + 1) * TQ] for r in range(NSA_GROUP))

        def hi_lo_dot(p, w):
            hi = p.astype(BF16)
            lo = (p - hi.astype(F32)).astype(BF16)
            return _dot(hi, w) + _dot(lo, w)

        imp = (hi_lo_dot(pcs_far, ov_ref[CMP_PAD:CMP_PAD + ncp, :])
               + hi_lo_dot(pcs_near, ov_ref[pl.ds(near0, CMP_NEAR), :]))
        imp_t = imp.T
        n_idx = lax.broadcasted_iota(jnp.int32, (LANES, TQ), 0)
        cur = (TQ * t + lax.broadcasted_iota(jnp.int32, (LANES, TQ), 1)) // SEL_BLOCK
        eligible = n_idx <= cur
        forced = ((n_idx == 0) | (n_idx == cur) | (n_idx == cur - 1)) & eligible
        rest = jnp.where(forced, -3.0e38, jnp.where(eligible, imp_t, -1.0))
        sel_t = _topk_rows(rest, n_idx, SEL_TOPN - 3, -3.0e38)
        maskneg = jnp.where(forced | (sel_t > 0.5), 0.0, NEG).T
        qaug_sc[slot] = jnp.concatenate(
            [jnp.concatenate([maskneg, qs[r], qtail], axis=1) for r in range(NSA_GROUP)],
            axis=0).astype(BF16)

    @pl.when(i == 0)
    def _():
        prepare(q_ref, 0, 0)

    slot = i % 2
    n_far = jnp.maximum(i - 2, 0)
    win = pl.multiple_of(TQ * i, TK)
    tail = _tail_steps(win, n_far % 2 == 1, rows, functools.partial(stack, ssel_ref), True)
    prepare_next = functools.partial(prepare, qn_ref, jnp.minimum(i + 1, pl.num_programs(2) - 1), 1 - slot)
    acc = _flash_chain(qaug_sc[slot], kaug, vaug, bufs, BAND_PAD, n_far, tail, prepare_next)
    o_sel = acc[:, :hd] / acc[:, hd:hd + 1]

    o_cmp = ocw_sc[slot, 0]
    o_win = ocw_sc[slot, 1]
    sig = jax.nn.sigmoid(gate_ref[...])
    outs = []
    for r in range(NSA_GROUP):
        def gcol(branch):
            c0 = branch * NSA_HEADS + r
            c1 = c0 + NSA_GROUP
            return jnp.where(g == 0, sig[:, c0:c0 + 1], sig[:, c1:c1 + 1])
        sl = slice(r * TQ, (r + 1) * TQ)
        outs.append(gcol(0) * o_cmp[sl, :hd] + gcol(1) * o_sel[sl] + gcol(2) * o_win[sl, :hd])
    o_ref[...] = jnp.concatenate(outs, axis=1).astype(o_ref.dtype)


def _nsa(main, aux, ckv, ssel, swin, scmp, ov, batch, seq):
    nq = seq // TQ
    ncp = seq // CMP_STRIDE
    gw = NSA_GROUP * HEAD_DIM
    once = pl.Buffered(1)
    kv_spec = lambda colblk: pl.BlockSpec((seq, LANES), lambda b, g, i: (b, colblk), pipeline_mode=once)
    strip_spec = lambda cols: pl.BlockSpec((NSA_GROUP, TQ, cols), lambda b, g, i: (g, 0, 0), pipeline_mode=once)
    return pl.pallas_call(
        functools.partial(_nsa_kernel, seq=seq),
        out_shape=jax.ShapeDtypeStruct((batch * seq, NSA_HEADS * HEAD_DIM), BF16),
        grid=(batch, NSA_KV_HEADS, nq),
        in_specs=[pl.BlockSpec((TQ, gw), lambda b, g, i: (b * nq + i, g)),
                  pl.BlockSpec((TQ, gw), lambda b, g, i: (b * nq + jnp.minimum(i + 1, nq - 1), g)),
                  kv_spec(4), kv_spec(5), kv_spec(6), kv_spec(7),
                  pl.BlockSpec((1, 1, ncp, HEAD_DIM), lambda b, g, i: (0, b * NSA_KV_HEADS + g, 0, 0)),
                  pl.BlockSpec((1, 1, ncp, HEAD_DIM), lambda b, g, i: (1, b * NSA_KV_HEADS + g, 0, 0)),
                  pl.BlockSpec((TQ, LANES), lambda b, g, i: (b * nq + i, 0)),
                  strip_spec(3 * TK), strip_spec(3 * TK), strip_spec(CMP_NEAR),
                  pl.BlockSpec(ov.shape, lambda b, g, i: (0, 0))],
        out_specs=pl.BlockSpec((TQ, gw), lambda b, g, i: (b * nq + i, g)),
        scratch_shapes=[pltpu.VMEM((BAND_PAD + seq, 2 * LANES), BF16),
                        pltpu.VMEM((BAND_PAD + seq, LANES), BF16),
                        pltpu.VMEM((BAND_PAD + seq, LANES), BF16),
                        pltpu.VMEM((BAND_PAD + seq, LANES), BF16),
                        pltpu.VMEM((ov.shape[0], LANES), BF16),
                        pltpu.VMEM((ov.shape[0], LANES), BF16),
                        pltpu.VMEM((2, NSA_GROUP * TQ, 2 * LANES), BF16),
                        pltpu.VMEM((2, 2, NSA_GROUP * TQ, LANES), F32),
                        *_flash_buffers(NSA_GROUP * TQ, LANES)],
        compiler_params=_attention_params(),
        name="nsa_attention",
    )(main, main, main, main, main, main, ckv, ckv, aux, ssel, swin, scmp, ov)


def _ones_column():
    return jnp.where(lax.broadcasted_iota(jnp.int32, (TK, LANES), 1) == 0, 1.0, 0.0).astype(BF16)


def _causal_flash(q, k_sc, v_sc, bufs, strip, i):
    n_far = jnp.maximum(2 * i - 1, 0)
    win = pl.multiple_of(i * TQD, SUB)
    tail = _tail_steps(win, i >= 1, q.shape[0], strip, i >= 1)
    return _flash_chain(q, k_sc, v_sc, bufs, FRONT, n_far, tail)


def _diff_kernel(q_ref, k_ref, v_ref, lam_ref, sub_ref, strip_ref, o_ref, k_sc, v_sc, *bufs, lam_init, seq):
    step = pl.program_id(2)
    hd = HEAD_DIM

    @pl.when(step == 0)
    def _():
        k_sc[0:FRONT, :] = jnp.zeros((FRONT, k_sc.shape[1]), BF16)
        v_sc[0:FRONT, :] = jnp.zeros((FRONT, v_sc.shape[1]), BF16)
        ones = _ones_column()

        def fill(c, _):
            r0 = pl.multiple_of(c * TK, TK)
            k_sc[pl.ds(FRONT + r0, TK), :] = k_ref[pl.ds(r0, TK), :]
            v_sc[pl.ds(FRONT + r0, TK), 0:LANES] = v_ref[pl.ds(r0, TK), :]
            v_sc[pl.ds(FRONT + r0, TK), LANES:2 * LANES] = ones
            return 0

        lax.fori_loop(0, seq // TK, fill, 0)

    lane = lax.broadcasted_iota(jnp.int32, (TQD, 2 * hd), 1)
    strip = lambda lo, hi: jnp.concatenate([strip_ref[0, :, lo:hi]] * 2, axis=0)
    lam_p = lam_ref[...]
    lam = (jnp.exp(jnp.sum(lam_p[0:1] * lam_p[1:2], axis=1, keepdims=True))
           - jnp.exp(jnp.sum(lam_p[2:3] * lam_p[3:4], axis=1, keepdims=True)) + lam_init)

    for part in range(DIFF_TILES_PER_STEP):
        tile_rows = slice(part * TQD, (part + 1) * TQD)
        q = q_ref[tile_rows, :].astype(F32) * QSCALE
        qst = jnp.concatenate([jnp.where(lane < hd, q, 0.0), jnp.where(lane < hd, 0.0, q)],
                              axis=0).astype(BF16)
        acc = _causal_flash(qst, k_sc, v_sc, bufs[3 * part:3 * part + 3], strip, step * DIFF_TILES_PER_STEP + part)
        o = acc[:, :2 * hd] / acc[:, 2 * hd:2 * hd + 1]
        d = o[:TQD] - lam * o[TQD:]
        o_ref[tile_rows, :] = (_rms(d, sub_ref[...]) * (1.0 - lam_init)).astype(o_ref.dtype)


def _diff(main, lam_p, subln, strips, batch, seq, lam_init):
    per_step = DIFF_TILES_PER_STEP * TQD
    nq = seq // per_step
    blk = 2 * HEAD_DIM
    q0, k0, v0 = 1024 // blk, 1536 // blk, 2048 // blk
    return pl.pallas_call(
        functools.partial(_diff_kernel, lam_init=lam_init, seq=seq),
        out_shape=jax.ShapeDtypeStruct((batch * seq, DIFF_HEADS * blk), BF16),
        grid=(batch, DIFF_HEADS, nq),
        in_specs=[pl.BlockSpec((per_step, blk), lambda b, h, i: (b * nq + i, q0 + h)),
                  pl.BlockSpec((seq, blk), lambda b, h, i: (b, k0 + h)),
                  pl.BlockSpec((seq, blk), lambda b, h, i: (b, v0 + h)),
                  pl.BlockSpec((4, HEAD_DIM), lambda b, h, i: (0, 0)),
                  pl.BlockSpec((1, blk), lambda b, h, i: (0, 0)),
                  pl.BlockSpec((1, TQD, 3 * TK), lambda b, h, i: (NSA_HEADS + h, 0, 0))],
        out_specs=pl.BlockSpec((per_step, blk), lambda b, h, i: (b * nq + i, h)),
        scratch_shapes=[pltpu.VMEM((FRONT + seq, LANES), BF16), pltpu.VMEM((FRONT + seq, 2 * LANES), BF16),
                        *(DIFF_TILES_PER_STEP * _flash_buffers(2 * TQD, 2 * LANES))],
        compiler_params=_attention_params(),
        name="diff_attention",
    )(main, main, main, lam_p, subln.reshape(1, blk), strips)


def _moba_kernel(q_ref, k_ref, v_ref, strip_ref, o_ref, kaug, va_sc, vb_sc, kmean, *bufs, seq):
    i = pl.program_id(2)
    hd = HEAD_DIM
    nb = seq // MOBA_BLOCK

    @pl.when(i == 0)
    def _():
        kmean[...] = jnp.zeros(kmean.shape, F32)
        kaug[0:FRONT, :] = jnp.zeros((FRONT, kaug.shape[1]), BF16)
        va_sc[0:FRONT, :] = jnp.zeros((FRONT, LANES), BF16)
        vb_sc[0:FRONT, :] = jnp.zeros((FRONT, LANES), BF16)
        col = lax.broadcasted_iota(jnp.int32, (MOBA_BLOCK, LANES), 1)
        ones_tail = jnp.where(lax.broadcasted_iota(jnp.int32, (MOBA_BLOCK, hd), 1) == 0, 1.0, 0.0).astype(BF16)

        def fill(n, _):
            r0 = pl.multiple_of(n * MOBA_BLOCK, MOBA_BLOCK)
            kb = k_ref[pl.ds(r0, MOBA_BLOCK), :]
            vb = v_ref[pl.ds(r0, MOBA_BLOCK), :]
            kaug[pl.ds(FRONT + r0, MOBA_BLOCK), 0:LANES] = jnp.where(col == n, 1.0, 0.0).astype(BF16)
            kaug[pl.ds(FRONT + r0, MOBA_BLOCK), LANES:2 * LANES] = kb
            va_sc[pl.ds(FRONT + r0, MOBA_BLOCK), :] = jnp.concatenate([vb[:, :hd], ones_tail], axis=1)
            vb_sc[pl.ds(FRONT + r0, MOBA_BLOCK), :] = jnp.concatenate([vb[:, hd:], ones_tail], axis=1)
            kmean[pl.ds(n, 1), :] = jnp.mean(kb.astype(F32), axis=0, keepdims=True)
            return 0

        lax.fori_loop(0, nb, fill, 0)

    qblk = q_ref[...]
    qf = qblk.astype(F32) * QSCALE
    lane = lax.broadcasted_iota(jnp.int32, (TQD, LANES), 1)
    n_idx = lax.broadcasted_iota(jnp.int32, (nb, TQD), 0)
    own = (i * TQD + lax.broadcasted_iota(jnp.int32, (nb, TQD), 1)) // MOBA_BLOCK
    past = n_idx < own
    km = kmean[0:nb, :]
    lane_k = lax.broadcasted_iota(jnp.int32, km.shape, 1)
    no_block = jnp.full((LANES - nb, TQD), NEG, F32)
    qrows = []
    for hh in range(2):
        mine = (lane < hd) if hh == 0 else (lane >= hd)
        mine_k = (lane_k < hd) if hh == 0 else (lane_k >= hd)
        gate_t = _dot_nt(jnp.where(mine_k, km, 0.0).astype(BF16), qblk)
        sel_t = _topk_rows(jnp.where(past, gate_t, NEG), n_idx, MOBA_TOPK, -3.0e38)
        allowed = ((sel_t > 0.5) & past) | (n_idx == own)
        maskneg = jnp.concatenate([jnp.where(allowed, 0.0, NEG), no_block], axis=0).T
        qrows.append(jnp.concatenate([maskneg, jnp.where(mine, qf, 0.0)], axis=1))
    qaug = jnp.concatenate(qrows, axis=0).astype(BF16)
    strip = lambda lo, hi: jnp.concatenate([strip_ref[0, :, lo:hi], strip_ref[1, :, lo:hi]], axis=0)
    acc = _causal_flash(qaug, kaug, (va_sc, vb_sc), bufs, strip, i)
    o = acc[:, :hd] / acc[:, hd:hd + 1]
    o_ref[...] = jnp.concatenate([o[:TQD], o[TQD:]], axis=1).astype(o_ref.dtype)


def _moba(main, strips, batch, seq):
    nq = seq // TQD
    pairs = MOBA_HEADS // 2
    return pl.pallas_call(
        functools.partial(_moba_kernel, seq=seq),
        out_shape=jax.ShapeDtypeStruct((batch * seq, MOBA_HEADS * HEAD_DIM), BF16),
        grid=(batch, pairs, nq),
        in_specs=[pl.BlockSpec((TQD, LANES), lambda b, h, i: (b * nq + i, h)),
                  pl.BlockSpec((seq, LANES), lambda b, h, i: (b, pairs + h)),
                  pl.BlockSpec((seq, LANES), lambda b, h, i: (b, 2 * pairs + h)),
                  pl.BlockSpec((2, TQD, 3 * TK), lambda b, h, i: (h, 0, 0))],
        out_specs=pl.BlockSpec((TQD, LANES), lambda b, h, i: (b * nq + i, h)),
        scratch_shapes=[pltpu.VMEM((FRONT + seq, 2 * LANES), BF16), pltpu.VMEM((FRONT + seq, LANES), BF16),
                        pltpu.VMEM((FRONT + seq, LANES), BF16),
                        pltpu.VMEM((LANES, LANES), F32), *_flash_buffers(2 * TQD, LANES)],
        compiler_params=_attention_params(),
        name="moba_attention",
    )(main, main, main, strips)


def _cmp_strip_geometry():
    r = np.arange(TQ)[:, None]
    cc = np.arange(CMP_NEAR)[None, :]
    rel_c = r + CMP_STRIDE * CMP_PAD - (CMP_BLOCK - 1) - CMP_STRIDE * cc
    return rel_c, rel_c >= 0


def _overlap_padded(seq):
    nc = (seq - CMP_BLOCK) // CMP_STRIDE + 1
    ncp = seq // CMP_STRIDE
    tok = np.arange(nc)[:, None] * CMP_STRIDE + np.arange(CMP_BLOCK)[None, :]
    ov = np.zeros((CMP_PAD + ncp + 2 * CMP_PER_TQ, LANES), np.float32)
    np.add.at(ov, (CMP_PAD + np.repeat(np.arange(nc), CMP_BLOCK), (tok // SEL_BLOCK).reshape(-1)),
              1.0 / CMP_BLOCK)
    return jnp.asarray(ov, dtype=BF16)


def _strips(bias_table):
    rel_c, valid_c = _cmp_strip_geometry()
    far = 1 << 30
    return (_toeplitz_strips(bias_table, TQD, 3 * TK, TK, 0, far, MOBA_HEADS, "bias_strip_diag"),
            _toeplitz_strips(bias_table, TQ, 3 * TK, 2 * TK, 0, far, NSA_HEADS, "bias_strip_sel"),
            _toeplitz_strips(bias_table, TQ, 3 * TK, 2 * TK, 0, WINDOW, NSA_HEADS, "bias_strip_win"),
            _bias_strips(bias_table, rel_c, valid_c, 0, NSA_HEADS, "bias_strip_cmp"))


def _even_heads(xf, batch, seq, bias_table, g_mix, w_in, pos_k, w1_k, w2_k, pos_v, w1_v, w2_v,
                lam_q1, lam_k1, lam_q2, lam_k2, subln, strips=None):
    d = xf.shape[1]
    strip_a, strip_sel, strip_win, strip_cmp = strips if strips is not None else _strips(bias_table)
    w_main = jnp.concatenate([w_in[:, 0:512], w_in[:, 768:1280], w_in[:, 1304:2840]], axis=1).astype(BF16)
    w_aux = jnp.concatenate([w_in[:, 512:768], w_in[:, 1280:1304],
                             jnp.zeros((d, 3 * LANES - 280), F32)], axis=1).astype(BF16)
    main0, kc0, kc1, vc0, vc1, gates = _in_proj_even(xf, g_mix, w_main, w_aux, 1024, w_main.shape[1] // 2)

    nch = seq // CMP_STRIDE
    chunks = lambda a: a.reshape(batch, nch, CMP_STRIDE * HEAD_DIM)
    pos = jnp.stack([pos_k, pos_v]).reshape(2, 2, CMP_STRIDE * HEAD_DIM)
    ckv = _compress(chunks(kc0), chunks(kc1), chunks(vc0), chunks(vc1), pos,
                    jnp.stack([w1_k, w1_v]).astype(BF16), jnp.stack([w2_k, w2_v]).astype(BF16))

    o_a = _nsa(main0, gates, ckv, strip_sel, strip_win, strip_cmp, _overlap_padded(seq), batch, seq)
    lam_p = jnp.stack([lam_q1, lam_k1, lam_q2, lam_k2])
    lam_init = 0.8 - 0.6 * math.exp(-0.3 * 0)
    o_b = _diff(main0, lam_p, subln, strip_a, batch, seq, lam_init)
    return o_a, o_b, ckv


def _odd_heads(xf, batch, seq, bias_table, g_mix, w_in, strips=None):
    strip_a = (strips if strips is not None else _strips(bias_table))[0]
    main1 = _norm_matmul(xf, g_mix, w_in.astype(BF16), BF16, 1024, w_in.shape[1] // 2, "in_proj_odd")
    return _moba(main1, strip_a, batch, seq)


def kernel(x, bias_table, norm_mix, norm_mlp, norm_final, mlp_w1, mlp_w2, ev_w_in, ev_w_out,
           ev_cmp_pos_k, ev_cmp_w1_k, ev_cmp_w2_k, ev_cmp_pos_v, ev_cmp_w1_v, ev_cmp_w2_v,
           ev_lam_q1, ev_lam_k1, ev_lam_q2, ev_lam_k2, ev_subln, od_w_in, od_w_out):
    batch, seq, d = x.shape
    assert d == D_MODEL and (batch * seq) % 1024 == 0 and seq % TQD == 0
    assert SEL_TOPN * SEL_BLOCK <= seq <= LANES * SEL_BLOCK and seq >= MOBA_TOPK * MOBA_BLOCK
    xf = x.reshape(batch * seq, d)
    strips = _strips(bias_table)

    o_a, o_b, _ = _even_heads(xf, batch, seq, bias_table, norm_mix[0], ev_w_in[0],
                              ev_cmp_pos_k[0], ev_cmp_w1_k[0], ev_cmp_w2_k[0],
                              ev_cmp_pos_v[0], ev_cmp_w1_v[0], ev_cmp_w2_v[0],
                              ev_lam_q1[0], ev_lam_k1[0], ev_lam_q2[0], ev_lam_k2[0], ev_subln[0], strips)
    wo = ev_w_out[0].astype(BF16)
    x1 = _post(xf, o_a, o_b, 0, wo[:512], wo[512:], norm_mlp[0], mlp_w1[0].astype(BF16),
               mlp_w2[0].astype(BF16), norm_final, False, 1024, 1024, "post_even")

    o_m = _odd_heads(x1, batch, seq, bias_table, norm_mix[1], od_w_in[0], strips)
    wo = od_w_out[0].astype(BF16)
    out = _post(x1, o_m, o_m, 1, wo[:512], wo[512:], norm_mlp[1], mlp_w1[1].astype(BF16),
                mlp_w2[1].astype(BF16), norm_final, True, 1024, 1024, "post_odd")
    return out.reshape(batch, seq, d)
```

```python
import functools
import math

import numpy as np
import jax
import jax.numpy as jnp
from jax import lax
from jax.experimental import pallas as pl
from jax.experimental.pallas import tpu as pltpu

F32 = jnp.float32
BF16 = jnp.bfloat16

D_MODEL = 1024
HEAD_DIM = 64
EPS = 1e-6
NEG = -1e30
NUM_BUCKETS = 32
MAX_DISTANCE = 128
NSA_HEADS = 8
NSA_KV_HEADS = 2
NSA_GROUP = 4
CMP_BLOCK = 32
CMP_STRIDE = 16
SEL_BLOCK = 64
SEL_TOPN = 16
WINDOW = 512
FORCE_BONUS = 1e3
DIFF_HEADS = 4
MOBA_HEADS = 16
MOBA_BLOCK = 256
MOBA_TOPK = 3
D_FF = 4 * D_MODEL
SCALE = HEAD_DIM ** -0.5

LANES = 128
TQ = 256
TQD = 512
DIFF_TILES_PER_STEP = 4
MOBA_TILES_PER_STEP = 4
TK = 256
FAR_REL = 113
CMP_PER_TQ = TQ // CMP_STRIDE
CMP_NEAR = 128
CMP_PAD = CMP_NEAR - 2 * CMP_PER_TQ
SUB = 2 * TK
BAND_PAD = 3 * TK
FRONT = 2 * TK
LOG2E = math.log2(math.e)
QSCALE = SCALE * LOG2E
VMEM_LIMIT = 56 * 1024 * 1024


def _attention_params():
    return pltpu.CompilerParams(dimension_semantics=("arbitrary", "arbitrary", "arbitrary"),
                                vmem_limit_bytes=VMEM_LIMIT)


def _dot_nt(a, b):
    return lax.dot_general(a, b, (((1,), (1,)), ((), ())), preferred_element_type=F32)


def _dot(a, b):
    return jnp.dot(a, b, preferred_element_type=F32)


def _rms(x, g):
    return x * lax.rsqrt(jnp.mean(x * x, axis=-1, keepdims=True) + EPS) * g


def _t5_bucket_np(rel):
    n = np.maximum(rel, 0)
    max_exact = NUM_BUCKETS // 2
    large = max_exact + (np.log(np.maximum(n, 1).astype(np.float32) / np.float32(max_exact))
                         / np.float32(math.log(MAX_DISTANCE / max_exact))
                         * np.float32(NUM_BUCKETS - max_exact)).astype(np.int32)
    large = np.minimum(large, NUM_BUCKETS - 1)
    return np.where(n < max_exact, n, large).astype(np.int32)


def _strip_kernel(tbl_ref, bucket_ref, valid_ref, o_ref, *, head0):
    h = pl.program_id(0) + head0
    bucket = bucket_ref[...]
    last = tbl_ref[NUM_BUCKETS - 1, h]
    acc = jnp.zeros(bucket.shape, F32)
    for b in range(NUM_BUCKETS - 1):
        acc = jnp.where(bucket == b, tbl_ref[b, h] - last, acc)
    o_ref[0] = jnp.where(valid_ref[...] != 0, acc * LOG2E, NEG)


def _toeplitz_kernel(tbl_ref, bucket_ref, valid_ref, o_ref, *, rows, cols):
    h = pl.program_id(0)
    bucket = bucket_ref[...]
    last = tbl_ref[NUM_BUCKETS - 1, h]
    acc = jnp.zeros(bucket.shape, F32)
    for b in range(NUM_BUCKETS - 1):
        acc = jnp.where(bucket == b, tbl_ref[b, h] - last, acc)
    line = jnp.where(valid_ref[...] != 0, acc * LOG2E, NEG)
    wide = jnp.broadcast_to(line[0:1], (rows, line.shape[1]))
    o_ref[0] = pltpu.roll(wide, 0, 1, stride=1, stride_axis=0)[:, :cols]


def _toeplitz_strips(table, rows, cols, off, lo, hi, nheads, name):
    width = -(-(rows + cols) // LANES) * LANES
    j = np.arange(width)
    rel = off - np.where(j < cols, j, j - width)
    valid = (rel >= lo) & (rel < hi)
    tile8 = lambda a: jnp.asarray(np.broadcast_to(a[None, :], (8, width)).astype(np.int32))
    return pl.pallas_call(
        functools.partial(_toeplitz_kernel, rows=rows, cols=cols),
        out_shape=jax.ShapeDtypeStruct((nheads, rows, cols), F32),
        grid=(nheads,),
        in_specs=[pl.BlockSpec(memory_space=pltpu.SMEM),
                  pl.BlockSpec((8, width), lambda h: (0, 0)),
                  pl.BlockSpec((8, width), lambda h: (0, 0))],
        out_specs=pl.BlockSpec((1, rows, cols), lambda h: (h, 0, 0)),
        name=name,
    )(table, tile8(_t5_bucket_np(rel)), tile8(valid))


def _bias_strips(table, rel, valid, head0, nheads, name):
    rows, cols = rel.shape
    bucket = jnp.asarray(_t5_bucket_np(rel))
    validi = jnp.asarray(valid.astype(np.int32))
    return pl.pallas_call(
        functools.partial(_strip_kernel, head0=head0),
        out_shape=jax.ShapeDtypeStruct((nheads, rows, cols), F32),
        grid=(nheads,),
        in_specs=[pl.BlockSpec(memory_space=pltpu.SMEM),
                  pl.BlockSpec((rows, cols), lambda h: (0, 0)),
                  pl.BlockSpec((rows, cols), lambda h: (0, 0))],
        out_specs=pl.BlockSpec((1, rows, cols), lambda h: (h, 0, 0)),
        name=name,
    )(table, bucket, validi)


def _norm_matmul_kernel(x_ref, g_ref, w_ref, o_ref, xn_ref):
    @pl.when(pl.program_id(1) == 0)
    def _():
        xn_ref[...] = _rms(x_ref[...], g_ref[...]).astype(BF16)

    o_ref[...] = _dot(xn_ref[...], w_ref[...]).astype(o_ref.dtype)


def _norm_matmul(x, g, w, out_dtype, tm, tn, name):
    m, d = x.shape
    n = w.shape[1]
    return pl.pallas_call(
        _norm_matmul_kernel,
        out_shape=jax.ShapeDtypeStruct((m, n), out_dtype),
        grid=(m // tm, n // tn),
        in_specs=[pl.BlockSpec((tm, d), lambda i, j: (i, 0)),
                  pl.BlockSpec((1, d), lambda i, j: (0, 0)),
                  pl.BlockSpec((d, tn), lambda i, j: (0, j))],
        out_specs=pl.BlockSpec((tm, tn), lambda i, j: (i, j)),
        scratch_shapes=[pltpu.VMEM((tm, d), BF16)],
        compiler_params=pltpu.CompilerParams(
            dimension_semantics=("arbitrary", "arbitrary"), vmem_limit_bytes=VMEM_LIMIT),
        name=name,
    )(x, g.reshape(1, d), w)


def _in_proj_even_kernel(x_ref, g_ref, w_ref, wa_ref, o_ref, kc0_ref, kc1_ref, vc0_ref, vc1_ref, gate_ref,
                         xn_ref):
    hd = HEAD_DIM

    @pl.when(pl.program_id(1) == 0)
    def _():
        xn = _rms(x_ref[...], g_ref[...]).astype(BF16)
        xn_ref[...] = xn
        aux = _dot(xn, wa_ref[...])
        kc0_ref[...] = aux[:, 0:hd]
        kc1_ref[...] = aux[:, hd:2 * hd]
        vc0_ref[...] = aux[:, 2 * hd:3 * hd]
        vc1_ref[...] = aux[:, 3 * hd:4 * hd]
        gate_ref[...] = aux[:, 4 * hd:]

    o_ref[...] = _dot(xn_ref[...], w_ref[...]).astype(o_ref.dtype)


def _in_proj_even(x, g, w_main, w_aux, tm, tn):
    m, d = x.shape
    n = w_main.shape[1]
    na = w_aux.shape[1]
    row_spec = lambda width: pl.BlockSpec((tm, width), lambda i, j: (i, 0))
    cmp_shape = jax.ShapeDtypeStruct((m, HEAD_DIM), F32)
    return pl.pallas_call(
        _in_proj_even_kernel,
        out_shape=(jax.ShapeDtypeStruct((m, n), BF16), cmp_shape, cmp_shape, cmp_shape, cmp_shape,
                   jax.ShapeDtypeStruct((m, na - 4 * HEAD_DIM), F32)),
        grid=(m // tm, n // tn),
        in_specs=[row_spec(d),
                  pl.BlockSpec((1, d), lambda i, j: (0, 0)),
                  pl.BlockSpec((d, tn), lambda i, j: (0, j)),
                  pl.BlockSpec((d, na), lambda i, j: (0, 0))],
        out_specs=(pl.BlockSpec((tm, tn), lambda i, j: (i, j)), row_spec(HEAD_DIM), row_spec(HEAD_DIM),
                   row_spec(HEAD_DIM), row_spec(HEAD_DIM), row_spec(na - 4 * HEAD_DIM)),
        scratch_shapes=[pltpu.VMEM((tm, d), BF16)],
        compiler_params=pltpu.CompilerParams(
            dimension_semantics=("arbitrary", "arbitrary"), vmem_limit_bytes=VMEM_LIMIT),
        name="in_proj_even",
    )(x, g.reshape(1, d), w_main, w_aux)


def _post_kernel(x_ref, a0_ref, a1_ref, wo0_ref, wo1_ref, g_ref, w1_ref, w2_ref, gf_ref,
                 o_ref, acc_ref, xn_ref, *, final_norm):
    f = pl.program_id(1)

    @pl.when(f == 0)
    def _():
        x1 = x_ref[...] + _dot(a0_ref[...], wo0_ref[...]) + _dot(a1_ref[...], wo1_ref[...])
        acc_ref[...] = x1
        xn_ref[...] = _rms(x1, g_ref[...]).astype(BF16)

    h = jnp.square(jnp.maximum(_dot(xn_ref[...], w1_ref[...]), 0.0))
    acc_ref[...] += _dot(h.astype(BF16), w2_ref[...])

    @pl.when(f == pl.num_programs(1) - 1)
    def _():
        y = acc_ref[...]
        if final_norm:
            y = _rms(y, gf_ref[...])
        o_ref[...] = y


def _post(x, a0, a1, a1_colblk, wo0, wo1, g, w1, w2, gf, final_norm, tm, tf, name):
    m, d = x.shape
    k0 = wo0.shape[0]
    k1 = wo1.shape[0]
    ff = w1.shape[1]
    return pl.pallas_call(
        functools.partial(_post_kernel, final_norm=final_norm),
        out_shape=jax.ShapeDtypeStruct((m, d), F32),
        grid=(m // tm, ff // tf),
        in_specs=[pl.BlockSpec((tm, d), lambda i, f: (i, 0)),
                  pl.BlockSpec((tm, k0), lambda i, f: (i, 0)),
                  pl.BlockSpec((tm, k1), lambda i, f: (i, a1_colblk)),
                  pl.BlockSpec((k0, d), lambda i, f: (0, 0)),
                  pl.BlockSpec((k1, d), lambda i, f: (0, 0)),
                  pl.BlockSpec((1, d), lambda i, f: (0, 0)),
                  pl.BlockSpec((d, tf), lambda i, f: (0, f)),
                  pl.BlockSpec((tf, d), lambda i, f: (f, 0)),
                  pl.BlockSpec((1, d), lambda i, f: (0, 0))],
        out_specs=pl.BlockSpec((tm, d), lambda i, f: (i, 0)),
        scratch_shapes=[pltpu.VMEM((tm, d), F32), pltpu.VMEM((tm, d), BF16)],
        compiler_params=pltpu.CompilerParams(
            dimension_semantics=("arbitrary", "arbitrary"), vmem_limit_bytes=VMEM_LIMIT),
        name=name,
    )(x, a0, a1, wo0, wo1, g.reshape(1, d), w1, w2, gf.reshape(1, d))


def _compress_kernel(kc0_ref, kc1_ref, vc0_ref, vc1_ref, pos_ref, w1_ref, w2_ref, o_ref):
    half = CMP_STRIDE * HEAD_DIM
    for kv, refs in enumerate(((kc0_ref, kc1_ref), (vc0_ref, vc1_ref))):
        w1a = w1_ref[kv, :half, :]
        w1b = w1_ref[kv, half:, :]
        pos = pos_ref[kv].astype(BF16)
        pa = jnp.broadcast_to(pos[0:1], (8, half))
        pb = jnp.broadcast_to(pos[1:2], (8, half))
        posterm = (_dot(pa, w1a) + _dot(pb, w1b))[0:1]
        for grp, x_ref in enumerate(refs):
            x = x_ref[0].astype(BF16)
            first = _dot(x, w1a)
            second = _dot(x, w1b)
            second = pltpu.roll(second, second.shape[0] - 1, 0)
            hid = jax.nn.gelu(first + second + posterm)
            o_ref[kv, grp] = _dot(hid.astype(BF16), w2_ref[kv])


def _compress(kc0, kc1, vc0, vc1, pos, w1, w2):
    batch, nch, wide = kc0.shape
    x_spec = pl.BlockSpec((1, nch, wide), lambda b: (b, 0, 0))
    full = lambda a: pl.BlockSpec(a.shape, lambda b: (0,) * a.ndim)
    return pl.pallas_call(
        _compress_kernel,
        out_shape=jax.ShapeDtypeStruct((2, batch * NSA_KV_HEADS, nch, HEAD_DIM), F32),
        grid=(batch,),
        in_specs=[x_spec, x_spec, x_spec, x_spec, full(pos), full(w1), full(w2)],
        out_specs=pl.BlockSpec((2, NSA_KV_HEADS, nch, HEAD_DIM), lambda b: (0, b, 0, 0)),
        compiler_params=pltpu.CompilerParams(
            dimension_semantics=("arbitrary",), vmem_limit_bytes=VMEM_LIMIT),
        name="nsa_compress",
    )(kc0, kc1, vc0, vc1, pos, w1, w2)


def _softmax_part(m, s):
    m_new = jnp.maximum(m, jnp.max(s, axis=1, keepdims=True))
    return m_new, jnp.exp2(m - m_new), jnp.exp2(s - m_new).astype(BF16)


def _pv(p, v_sc, r0, width):
    if not isinstance(v_sc, (tuple, list)):
        return _dot(p, v_sc[pl.ds(r0, width), :])
    share = p.shape[0] // len(v_sc)
    return jnp.concatenate([_dot(p[g * share:(g + 1) * share], v[pl.ds(r0, width), :])
                            for g, v in enumerate(v_sc)], axis=0)


def _flash_chain(q, k_sc, v_sc, bufs, far0, n_far, tail, side_work=None):
    s_sc, p_sc, acc_sc = bufs
    rows = q.shape[0]
    assert SUB == 2 * TK and tail[0][1] == SUB
    tail0 = tail[0][0]

    def qk(r0):
        return _dot_nt(q, k_sc[pl.ds(pl.multiple_of(r0, TK), SUB), :])

    n_steps = n_far // 2
    s_sc[...] = qk(jnp.where(n_steps > 0, far0, tail0))
    p_sc[...] = jnp.zeros(p_sc.shape, BF16)
    acc_sc[...] = jnp.zeros(acc_sc.shape, F32)

    def trips(start, n_sub, n_trips, after):
        def body(j, m):
            base = pl.multiple_of(start + j * (n_sub * SUB), SUB)
            acc = acc_sc[...] + _pv(p_sc[...], v_sc, base - SUB, SUB)
            s_cur = s_sc[...]
            for u in range(n_sub):
                m, alpha, p = _softmax_part(m, s_cur)
                if u + 1 < n_sub:
                    s_cur = qk(base + (u + 1) * SUB)
                    acc = alpha * acc + _pv(p, v_sc, base + u * SUB, SUB)
                else:
                    s_sc[...] = qk(jnp.where(j + 1 < n_trips, base + n_sub * SUB, after))
                    p_sc[...] = p
                    acc_sc[...] = alpha * acc
            return m
        return body

    m = jnp.full((rows, 1), NEG, F32)
    start, left = far0, n_steps
    for n_sub in (4, 2, 1):
        n_trips = left // n_sub
        left = left - n_trips * n_sub
        nxt = start + n_trips * (n_sub * SUB)
        after = jnp.where(left > 0, nxt, tail0)
        m = lax.fori_loop(0, n_trips, trips(start, n_sub, n_trips, after), m)
        start = nxt
    if side_work is not None:
        side_work()
    base = pl.multiple_of(far0 + n_steps * SUB, SUB)
    acc = acc_sc[...] + _pv(p_sc[...], v_sc, base - SUB, SUB)
    m, alpha, p = _softmax_part(m, s_sc[...] + tail[0][2]())
    acc = alpha * acc + _pv(p, v_sc, tail0, SUB)
    return _flash_tail((m, acc), q, k_sc, v_sc, tail[1:])


def _flash_buffers(rows, ncols):
    return [pltpu.VMEM((rows, SUB), F32), pltpu.VMEM((rows, SUB), BF16), pltpu.VMEM((rows, ncols), F32)]


def _flash_tail(carry, q, k_sc, v_sc, tail):
    m, acc = carry
    for r0, width, extra in tail:
        m, alpha, p = _softmax_part(m, _dot_nt(q, k_sc[pl.ds(r0, width), :]) + extra())
        acc = alpha * acc + _pv(p, v_sc, r0, width)
    return acc


def _tile_gate(cond):
    return jnp.where(cond, 0.0, NEG)


def _tail_steps(win, odd_ok, rows, strip, near_ok):
    def first():
        return jnp.concatenate([jnp.broadcast_to(_tile_gate(odd_ok), (rows, TK)),
                                strip(0, TK) + _tile_gate(near_ok)], axis=1)

    return [(win, SUB, first), (win + SUB, SUB, lambda: strip(TK, 3 * TK))]


def _topk_rows(score, idx, k, floor):
    sel = jnp.zeros(score.shape, F32)
    nrow = score.shape[0]
    for _ in range(k):
        mx = jnp.max(score, axis=0, keepdims=True)
        first = jnp.min(jnp.where(score == mx, idx, nrow), axis=0, keepdims=True)
        pick = idx == first
        sel = jnp.where(pick, 1.0, sel)
        score = jnp.where(pick, floor, score)
    return sel


def _nsa_kernel(q_ref, qn_ref, ks_ref, vs_ref, kw_ref, vw_ref, kc_ref, vc_ref, gate_ref,
                ssel_ref, swin_ref, scmp_ref, ov_ref, o_ref,
                kaug, vaug, kwp, vwp, kcp, vcp, qaug_sc, ocw_sc, *bufs, seq):
    g = pl.program_id(1)
    i = pl.program_id(2)
    ncp = seq // CMP_STRIDE
    hd = HEAD_DIM
    rows = NSA_GROUP * TQ

    @pl.when(i == 0)
    def _():
        lane = lax.broadcasted_iota(jnp.int32, (BAND_PAD, LANES), 1)
        padflag = jnp.where(lane == hd, 1.0, 0.0).astype(BF16)
        kaug[0:BAND_PAD, 0:LANES] = jnp.zeros((BAND_PAD, LANES), BF16)
        kaug[0:BAND_PAD, LANES:2 * LANES] = padflag
        kwp[0:BAND_PAD, :] = padflag
        vaug[0:BAND_PAD, :] = jnp.zeros((BAND_PAD, LANES), BF16)
        vwp[0:BAND_PAD, :] = jnp.zeros((BAND_PAD, LANES), BF16)
        onescol = jnp.where(lax.broadcasted_iota(jnp.int32, (TK, LANES), 1) == hd, 1.0, 0.0)
        zero_hd = jnp.zeros((TK, hd), F32)
        col = lax.broadcasted_iota(jnp.int32, (TK, LANES), 1)
        row = lax.broadcasted_iota(jnp.int32, (TK, LANES), 0)

        def fill(c, _):
            src = pl.multiple_of(c * TK, TK)
            dst = pl.multiple_of(BAND_PAD + c * TK, TK)

            def pick(ref):
                both = ref[pl.ds(src, TK), :].astype(F32)
                return jnp.concatenate([jnp.where(g == 0, both[:, :hd], both[:, hd:]), zero_hd], axis=1)

            kaug[pl.ds(dst, TK), 0:LANES] = jnp.where((src + row) // SEL_BLOCK == col, 1.0, 0.0).astype(BF16)
            kaug[pl.ds(dst, TK), LANES:2 * LANES] = pick(ks_ref).astype(BF16)
            kwp[pl.ds(dst, TK), :] = pick(kw_ref).astype(BF16)
            vaug[pl.ds(dst, TK), :] = (pick(vs_ref) + onescol).astype(BF16)
            vwp[pl.ds(dst, TK), :] = (pick(vw_ref) + onescol).astype(BF16)
            return 0

        lax.fori_loop(0, seq // TK, fill, 0)
        lane_c = lax.broadcasted_iota(jnp.int32, (CMP_PAD, LANES), 1)
        kcp[0:CMP_PAD, :] = jnp.where(lane_c == hd, 1.0, 0.0).astype(BF16)
        vcp[0:CMP_PAD, :] = jnp.zeros((CMP_PAD, LANES), BF16)
        zc = jnp.zeros((ncp, hd), F32)
        onesc = jnp.where(lax.broadcasted_iota(jnp.int32, (ncp, LANES), 1) == hd, 1.0, 0.0)
        kcp[CMP_PAD:CMP_PAD + ncp, :] = jnp.concatenate([kc_ref[0, 0], zc], axis=1).astype(BF16)
        vcp[CMP_PAD:CMP_PAD + ncp, :] = (jnp.concatenate([vc_ref[0, 0], zc], axis=1) + onesc).astype(BF16)
        tail = kcp.shape[0] - CMP_PAD - ncp
        kcp[CMP_PAD + ncp:, :] = jnp.zeros((tail, LANES), BF16)
        vcp[CMP_PAD + ncp:, :] = jnp.zeros((tail, LANES), BF16)

    def stack(ref, lo, hi):
        return jnp.concatenate([ref[r, :, lo:hi] for r in range(NSA_GROUP)], axis=0)

    def prepare(qsrc_ref, t, slot):
        qblk = qsrc_ref[...].astype(F32) * QSCALE
        lane_t = lax.broadcasted_iota(jnp.int32, (TQ, hd), 1)
        qtail = jnp.where(lane_t == 0, NEG, 0.0)
        qs = [qblk[:, r * hd:(r + 1) * hd] for r in range(NSA_GROUP)]
        q128 = jnp.concatenate([jnp.concatenate([qs[r], qtail], axis=1) for r in range(NSA_GROUP)],
                               axis=0).astype(BF16)

        s_far = _dot_nt(q128, kcp[CMP_PAD:CMP_PAD + ncp, :])
        c_idx = lax.broadcasted_iota(jnp.int32, (1, ncp), 1)
        far_ok = c_idx < CMP_PER_TQ * t - CMP_PAD
        near0 = pl.multiple_of(CMP_PER_TQ * t, CMP_PER_TQ)
        s_near = _dot_nt(q128, kcp[pl.ds(near0, CMP_NEAR), :]) + stack(scmp_ref, 0, CMP_NEAR)
        near_ok = s_near > 0.5 * NEG
        s_far = jnp.where(far_ok, s_far, NEG)
        m_c = jnp.maximum(jnp.max(s_far, axis=1, keepdims=True), jnp.max(s_near, axis=1, keepdims=True))
        p_far = jnp.where(far_ok, jnp.exp2(s_far - m_c), 0.0)
        p_near = jnp.where(near_ok, jnp.exp2(s_near - m_c), 0.0)
        den = jnp.maximum(jnp.sum(p_far, axis=1, keepdims=True) + jnp.sum(p_near, axis=1, keepdims=True), 1e-30)
        pc_far = p_far / den
        pc_near = p_near / den
        ocw_sc[slot, 0] = (_dot(pc_far.astype(BF16), vcp[CMP_PAD:CMP_PAD + ncp, :])
                           + _dot(pc_near.astype(BF16), vcp[pl.ds(near0, CMP_NEAR), :]))

        band0 = pl.multiple_of(TQ * t, TK) + BAND_PAD - 2 * TK
        steps = [(band0, 2 * TK, lambda: stack(swin_ref, 0, 2 * TK)),
                 (band0 + 2 * TK, TK, lambda: stack(swin_ref, 2 * TK, 3 * TK))]
        acc = _flash_tail((jnp.full((rows, 1), NEG, F32), jnp.zeros((rows, LANES), F32)), q128, kwp, vwp, steps)
        ocw_sc[slot, 1] = acc / acc[:, hd:hd + 1]

        pcs_far = sum(pc_far[r * TQ:(r + 1) * TQ] for r in range(NSA_GROUP))
        pcs_near = sum(pc_near[r * TQ:(r + 1) * TQ] for r in range(NSA_GROUP))

        def hi_lo_dot(p, w):
            hi = p.astype(BF16)
            lo = (p - hi.astype(F32)).astype(BF16)
            return _dot(hi, w) + _dot(lo, w)

        imp = (hi_lo_dot(pcs_far, ov_ref[CMP_PAD:CMP_PAD + ncp, :])
               + hi_lo_dot(pcs_near, ov_ref[pl.ds(near0, CMP_NEAR), :]))
        imp_t = imp.T
        n_idx = lax.broadcasted_iota(jnp.int32, (LANES, TQ), 0)
        cur = (TQ * t + lax.broadcasted_iota(jnp.int32, (LANES, TQ), 1)) // SEL_BLOCK
        eligible = n_idx <= cur
        forced = ((n_idx == 0) | (n_idx == cur) | (n_idx == cur - 1)) & eligible
        rest = jnp.where(forced, -3.0e38, jnp.where(eligible, imp_t, -1.0))
        sel_t = _topk_rows(rest, n_idx, SEL_TOPN - 3, -3.0e38)
        maskneg = jnp.where(forced | (sel_t > 0.5), 0.0, NEG).T
        qaug_sc[slot] = jnp.concatenate(
            [jnp.concatenate([maskneg, qs[r], qtail], axis=1) for r in range(NSA_GROUP)],
            axis=0).astype(BF16)

    @pl.when(i == 0)
    def _():
        prepare(q_ref, 0, 0)

    slot = i % 2
    n_far = jnp.maximum(i - 2, 0)
    win = pl.multiple_of(TQ * i, TK)
    tail = _tail_steps(win, n_far % 2 == 1, rows, functools.partial(stack, ssel_ref), True)
    prepare_next = functools.partial(prepare, qn_ref, jnp.minimum(i + 1, pl.num_programs(2) - 1), 1 - slot)
    acc = _flash_chain(qaug_sc[slot], kaug, vaug, bufs, BAND_PAD, n_far, tail, prepare_next)
    o_sel = acc[:, :hd] / acc[:, hd:hd + 1]

    o_cmp = ocw_sc[slot, 0]
    o_win = ocw_sc[slot, 1]
    sig = jax.nn.sigmoid(gate_ref[...])
    outs = []
    for r in range(NSA_GROUP):
        def gcol(branch):
            c0 = branch * NSA_HEADS + r
            c1 = c0 + NSA_GROUP
            return jnp.where(g == 0, sig[:, c0:c0 + 1], sig[:, c1:c1 + 1])
        sl = slice(r * TQ, (r + 1) * TQ)
        outs.append(gcol(0) * o_cmp[sl, :hd] + gcol(1) * o_sel[sl] + gcol(2) * o_win[sl, :hd])
    o_ref[...] = jnp.concatenate(outs, axis=1).astype(o_ref.dtype)


def _nsa(main, aux, ckv, ssel, swin, scmp, ov, batch, seq):
    nq = seq // TQ
    ncp = seq // CMP_STRIDE
    gw = NSA_GROUP * HEAD_DIM
    once = pl.Buffered(1)
    kv_spec = lambda colblk: pl.BlockSpec((seq, LANES), lambda b, g, i: (b, colblk), pipeline_mode=once)
    strip_spec = lambda cols: pl.BlockSpec((NSA_GROUP, TQ, cols), lambda b, g, i: (g, 0, 0), pipeline_mode=once)
    return pl.pallas_call(
        functools.partial(_nsa_kernel, seq=seq),
        out_shape=jax.ShapeDtypeStruct((batch * seq, NSA_HEADS * HEAD_DIM), BF16),
        grid=(batch, NSA_KV_HEADS, nq),
        in_specs=[pl.BlockSpec((TQ, gw), lambda b, g, i: (b * nq + i, g)),
                  pl.BlockSpec((TQ, gw), lambda b, g, i: (b * nq + jnp.minimum(i + 1, nq - 1), g)),
                  kv_spec(4), kv_spec(5), kv_spec(6), kv_spec(7),
                  pl.BlockSpec((1, 1, ncp, HEAD_DIM), lambda b, g, i: (0, b * NSA_KV_HEADS + g, 0, 0)),
                  pl.BlockSpec((1, 1, ncp, HEAD_DIM), lambda b, g, i: (1, b * NSA_KV_HEADS + g, 0, 0)),
                  pl.BlockSpec((TQ, LANES), lambda b, g, i: (b * nq + i, 0)),
                  strip_spec(3 * TK), strip_spec(3 * TK), strip_spec(CMP_NEAR),
                  pl.BlockSpec(ov.shape, lambda b, g, i: (0, 0))],
        out_specs=pl.BlockSpec((TQ, gw), lambda b, g, i: (b * nq + i, g)),
        scratch_shapes=[pltpu.VMEM((BAND_PAD + seq, 2 * LANES), BF16),
                        pltpu.VMEM((BAND_PAD + seq, LANES), BF16),
                        pltpu.VMEM((BAND_PAD + seq, LANES), BF16),
                        pltpu.VMEM((BAND_PAD + seq, LANES), BF16),
                        pltpu.VMEM((ov.shape[0], LANES), BF16),
                        pltpu.VMEM((ov.shape[0], LANES), BF16),
                        pltpu.VMEM((2, NSA_GROUP * TQ, 2 * LANES), BF16),
                        pltpu.VMEM((2, 2, NSA_GROUP * TQ, LANES), F32),
                        *_flash_buffers(NSA_GROUP * TQ, LANES)],
        compiler_params=_attention_params(),
        name="nsa_attention",
    )(main, main, main, main, main, main, ckv, ckv, aux, ssel, swin, scmp, ov)


def _ones_column():
    return jnp.where(lax.broadcasted_iota(jnp.int32, (TK, LANES), 1) == 0, 1.0, 0.0).astype(BF16)


def _causal_flash(q, k_sc, v_sc, bufs, strip, i):
    n_far = jnp.maximum(2 * i - 1, 0)
    win = pl.multiple_of(i * TQD, SUB)
    tail = _tail_steps(win, i >= 1, q.shape[0], strip, i >= 1)
    return _flash_chain(q, k_sc, v_sc, bufs, FRONT, n_far, tail)


def _diff_kernel(q_ref, k_ref, v_ref, lam_ref, sub_ref, strip_ref, o_ref, k_sc, v_sc, *bufs, lam_init, seq):
    step = pl.program_id(2)
    hd = HEAD_DIM

    @pl.when(step == 0)
    def _():
        k_sc[0:FRONT, :] = jnp.zeros((FRONT, k_sc.shape[1]), BF16)
        v_sc[0:FRONT, :] = jnp.zeros((FRONT, v_sc.shape[1]), BF16)
        ones = _ones_column()

        def fill(c, _):
            r0 = pl.multiple_of(c * TK, TK)
            k_sc[pl.ds(FRONT + r0, TK), :] = k_ref[pl.ds(r0, TK), :]
            v_sc[pl.ds(FRONT + r0, TK), 0:LANES] = v_ref[pl.ds(r0, TK), :]
            v_sc[pl.ds(FRONT + r0, TK), LANES:2 * LANES] = ones
            return 0

        lax.fori_loop(0, seq // TK, fill, 0)

    lane = lax.broadcasted_iota(jnp.int32, (TQD, 2 * hd), 1)
    strip = lambda lo, hi: jnp.concatenate([strip_ref[0, :, lo:hi]] * 2, axis=0)
    lam_p = lam_ref[...]
    lam = (jnp.exp(jnp.sum(lam_p[0:1] * lam_p[1:2], axis=1, keepdims=True))
           - jnp.exp(jnp.sum(lam_p[2:3] * lam_p[3:4], axis=1, keepdims=True)) + lam_init)

    for part in range(DIFF_TILES_PER_STEP):
        tile_rows = slice(part * TQD, (part + 1) * TQD)
        q = q_ref[tile_rows, :].astype(F32) * QSCALE
        qst = jnp.concatenate([jnp.where(lane < hd, q, 0.0), jnp.where(lane < hd, 0.0, q)],
                              axis=0).astype(BF16)
        acc = _causal_flash(qst, k_sc, v_sc, bufs[3 * part:3 * part + 3], strip, step * DIFF_TILES_PER_STEP + part)
        o = acc[:, :2 * hd] / acc[:, 2 * hd:2 * hd + 1]
        d = o[:TQD] - lam * o[TQD:]
        o_ref[tile_rows, :] = (_rms(d, sub_ref[...]) * (1.0 - lam_init)).astype(o_ref.dtype)


def _diff(main, lam_p, subln, strips, batch, seq, lam_init):
    per_step = DIFF_TILES_PER_STEP * TQD
    nq = seq // per_step
    blk = 2 * HEAD_DIM
    q0, k0, v0 = 1024 // blk, 1536 // blk, 2048 // blk
    return pl.pallas_call(
        functools.partial(_diff_kernel, lam_init=lam_init, seq=seq),
        out_shape=jax.ShapeDtypeStruct((batch * seq, DIFF_HEADS * blk), BF16),
        grid=(batch, DIFF_HEADS, nq),
        in_specs=[pl.BlockSpec((per_step, blk), lambda b, h, i: (b * nq + i, q0 + h)),
                  pl.BlockSpec((seq, blk), lambda b, h, i: (b, k0 + h)),
                  pl.BlockSpec((seq, blk), lambda b, h, i: (b, v0 + h)),
                  pl.BlockSpec((4, HEAD_DIM), lambda b, h, i: (0, 0)),
                  pl.BlockSpec((1, blk), lambda b, h, i: (0, 0)),
                  pl.BlockSpec((1, TQD, 3 * TK), lambda b, h, i: (NSA_HEADS + h, 0, 0))],
        out_specs=pl.BlockSpec((per_step, blk), lambda b, h, i: (b * nq + i, h)),
        scratch_shapes=[pltpu.VMEM((FRONT + seq, LANES), BF16), pltpu.VMEM((FRONT + seq, 2 * LANES), BF16),
                        *(DIFF_TILES_PER_STEP * _flash_buffers(2 * TQD, 2 * LANES))],
        compiler_params=_attention_params(),
        name="diff_attention",
    )(main, main, main, lam_p, subln.reshape(1, blk), strips)


def _moba_kernel(q_ref, k_ref, v_ref, strip_ref, o_ref, kaug, va_sc, vb_sc, kmean, *bufs, seq):
    step = pl.program_id(2)
    hd = HEAD_DIM
    nb = seq // MOBA_BLOCK

    @pl.when(step == 0)
    def _():
        kmean[...] = jnp.zeros(kmean.shape, F32)
        kaug[0:FRONT, :] = jnp.zeros((FRONT, kaug.shape[1]), BF16)
        va_sc[0:FRONT, :] = jnp.zeros((FRONT, LANES), BF16)
        vb_sc[0:FRONT, :] = jnp.zeros((FRONT, LANES), BF16)
        col = lax.broadcasted_iota(jnp.int32, (MOBA_BLOCK, LANES), 1)
        ones_tail = jnp.where(lax.broadcasted_iota(jnp.int32, (MOBA_BLOCK, hd), 1) == 0, 1.0, 0.0).astype(BF16)

        def fill(n, _):
            r0 = pl.multiple_of(n * MOBA_BLOCK, MOBA_BLOCK)
            kb = k_ref[pl.ds(r0, MOBA_BLOCK), :]
            vb = v_ref[pl.ds(r0, MOBA_BLOCK), :]
            kaug[pl.ds(FRONT + r0, MOBA_BLOCK), 0:LANES] = jnp.where(col == n, 1.0, 0.0).astype(BF16)
            kaug[pl.ds(FRONT + r0, MOBA_BLOCK), LANES:2 * LANES] = kb
            va_sc[pl.ds(FRONT + r0, MOBA_BLOCK), :] = jnp.concatenate([vb[:, :hd], ones_tail], axis=1)
            vb_sc[pl.ds(FRONT + r0, MOBA_BLOCK), :] = jnp.concatenate([vb[:, hd:], ones_tail], axis=1)
            kmean[pl.ds(n, 1), :] = jnp.mean(kb.astype(F32), axis=0, keepdims=True)
            return 0

        lax.fori_loop(0, nb, fill, 0)

    lane = lax.broadcasted_iota(jnp.int32, (TQD, LANES), 1)
    n_idx = lax.broadcasted_iota(jnp.int32, (nb, TQD), 0)
    km = kmean[0:nb, :]
    lane_k = lax.broadcasted_iota(jnp.int32, km.shape, 1)
    no_block = jnp.full((LANES - nb, TQD), NEG, F32)
    strip = lambda lo, hi: jnp.concatenate([strip_ref[0, :, lo:hi], strip_ref[1, :, lo:hi]], axis=0)

    for part in range(MOBA_TILES_PER_STEP):
        i = step * MOBA_TILES_PER_STEP + part
        tile_rows = slice(part * TQD, (part + 1) * TQD)
        qblk = q_ref[tile_rows, :]
        qf = qblk.astype(F32) * QSCALE
        own = (i * TQD + lax.broadcasted_iota(jnp.int32, (nb, TQD), 1)) // MOBA_BLOCK
        past = n_idx < own
        qrows = []
        for hh in range(2):
            mine = (lane < hd) if hh == 0 else (lane >= hd)
            mine_k = (lane_k < hd) if hh == 0 else (lane_k >= hd)
            gate_t = _dot_nt(jnp.where(mine_k, km, 0.0).astype(BF16), qblk)
            sel_t = _topk_rows(jnp.where(past, gate_t, NEG), n_idx, MOBA_TOPK, -3.0e38)
            allowed = ((sel_t > 0.5) & past) | (n_idx == own)
            maskneg = jnp.concatenate([jnp.where(allowed, 0.0, NEG), no_block], axis=0).T
            qrows.append(jnp.concatenate([maskneg, jnp.where(mine, qf, 0.0)], axis=1))
        qaug = jnp.concatenate(qrows, axis=0).astype(BF16)
        acc = _causal_flash(qaug, kaug, (va_sc, vb_sc), bufs[3 * part:3 * part + 3], strip, i)
        o = acc[:, :hd] / acc[:, hd:hd + 1]
        o_ref[tile_rows, :] = jnp.concatenate([o[:TQD], o[TQD:]], axis=1).astype(o_ref.dtype)


def _moba(main, strips, batch, seq):
    per_step = MOBA_TILES_PER_STEP * TQD
    nq = seq // per_step
    pairs = MOBA_HEADS // 2
    return pl.pallas_call(
        functools.partial(_moba_kernel, seq=seq),
        out_shape=jax.ShapeDtypeStruct((batch * seq, MOBA_HEADS * HEAD_DIM), BF16),
        grid=(batch, pairs, nq),
        in_specs=[pl.BlockSpec((per_step, LANES), lambda b, h, i: (b * nq + i, h)),
                  pl.BlockSpec((seq, LANES), lambda b, h, i: (b, pairs + h)),
                  pl.BlockSpec((seq, LANES), lambda b, h, i: (b, 2 * pairs + h)),
                  pl.BlockSpec((2, TQD, 3 * TK), lambda b, h, i: (h, 0, 0))],
        out_specs=pl.BlockSpec((per_step, LANES), lambda b, h, i: (b * nq + i, h)),
        scratch_shapes=[pltpu.VMEM((FRONT + seq, 2 * LANES), BF16), pltpu.VMEM((FRONT + seq, LANES), BF16),
                        pltpu.VMEM((FRONT + seq, LANES), BF16),
                        pltpu.VMEM((LANES, LANES), F32),
                        *(MOBA_TILES_PER_STEP * _flash_buffers(2 * TQD, LANES))],
        compiler_params=_attention_params(),
        name="moba_attention",
    )(main, main, main, strips)


def _cmp_strip_geometry():
    r = np.arange(TQ)[:, None]
    cc = np.arange(CMP_NEAR)[None, :]
    rel_c = r + CMP_STRIDE * CMP_PAD - (CMP_BLOCK - 1) - CMP_STRIDE * cc
    return rel_c, rel_c >= 0


def _overlap_padded(seq):
    nc = (seq - CMP_BLOCK) // CMP_STRIDE + 1
    ncp = seq // CMP_STRIDE
    tok = np.arange(nc)[:, None] * CMP_STRIDE + np.arange(CMP_BLOCK)[None, :]
    ov = np.zeros((CMP_PAD + ncp + 2 * CMP_PER_TQ, LANES), np.float32)
    np.add.at(ov, (CMP_PAD + np.repeat(np.arange(nc), CMP_BLOCK), (tok // SEL_BLOCK).reshape(-1)),
              1.0 / CMP_BLOCK)
    return jnp.asarray(ov, dtype=BF16)


def _strips(bias_table):
    rel_c, valid_c = _cmp_strip_geometry()
    far = 1 << 30
    return (_toeplitz_strips(bias_table, TQD, 3 * TK, TK, 0, far, MOBA_HEADS, "bias_strip_diag"),
            _toeplitz_strips(bias_table, TQ, 3 * TK, 2 * TK, 0, far, NSA_HEADS, "bias_strip_sel"),
            _toeplitz_strips(bias_table, TQ, 3 * TK, 2 * TK, 0, WINDOW, NSA_HEADS, "bias_strip_win"),
            _bias_strips(bias_table, rel_c, valid_c, 0, NSA_HEADS, "bias_strip_cmp"))


def _even_heads(xf, batch, seq, bias_table, g_mix, w_in, pos_k, w1_k, w2_k, pos_v, w1_v, w2_v,
                lam_q1, lam_k1, lam_q2, lam_k2, subln, strips=None):
    d = xf.shape[1]
    strip_a, strip_sel, strip_win, strip_cmp = strips if strips is not None else _strips(bias_table)
    w_main = jnp.concatenate([w_in[:, 0:512], w_in[:, 768:1280], w_in[:, 1304:2840]], axis=1).astype(BF16)
    w_aux = jnp.concatenate([w_in[:, 512:768], w_in[:, 1280:1304],
                             jnp.zeros((d, 3 * LANES - 280), F32)], axis=1).astype(BF16)
    main0, kc0, kc1, vc0, vc1, gates = _in_proj_even(xf, g_mix, w_main, w_aux, 1024, w_main.shape[1] // 2)

    nch = seq // CMP_STRIDE
    chunks = lambda a: a.reshape(batch, nch, CMP_STRIDE * HEAD_DIM)
    pos = jnp.stack([pos_k, pos_v]).reshape(2, 2, CMP_STRIDE * HEAD_DIM)
    ckv = _compress(chunks(kc0), chunks(kc1), chunks(vc0), chunks(vc1), pos,
                    jnp.stack([w1_k, w1_v]).astype(BF16), jnp.stack([w2_k, w2_v]).astype(BF16))

    o_a = _nsa(main0, gates, ckv, strip_sel, strip_win, strip_cmp, _overlap_padded(seq), batch, seq)
    lam_p = jnp.stack([lam_q1, lam_k1, lam_q2, lam_k2])
    lam_init = 0.8 - 0.6 * math.exp(-0.3 * 0)
    o_b = _diff(main0, lam_p, subln, strip_a, batch, seq, lam_init)
    return o_a, o_b, ckv


def _odd_heads(xf, batch, seq, bias_table, g_mix, w_in, strips=None):
    strip_a = (strips if strips is not None else _strips(bias_table))[0]
    main1 = _norm_matmul(xf, g_mix, w_in.astype(BF16), BF16, 1024, w_in.shape[1] // 2, "in_proj_odd")
    return _moba(main1, strip_a, batch, seq)


def kernel(x, bias_table, norm_mix, norm_mlp, norm_final, mlp_w1, mlp_w2, ev_w_in, ev_w_out,
           ev_cmp_pos_k, ev_cmp_w1_k, ev_cmp_w2_k, ev_cmp_pos_v, ev_cmp_w1_v, ev_cmp_w2_v,
           ev_lam_q1, ev_lam_k1, ev_lam_q2, ev_lam_k2, ev_subln, od_w_in, od_w_out):
    batch, seq, d = x.shape
    assert d == D_MODEL and (batch * seq) % 1024 == 0
    assert seq % (max(DIFF_TILES_PER_STEP, MOBA_TILES_PER_STEP) * TQD) == 0
    assert SEL_TOPN * SEL_BLOCK <= seq <= LANES * SEL_BLOCK and seq >= MOBA_TOPK * MOBA_BLOCK
    xf = x.reshape(batch * seq, d)
    strips = _strips(bias_table)

    o_a, o_b, _ = _even_heads(xf, batch, seq, bias_table, norm_mix[0], ev_w_in[0],
                              ev_cmp_pos_k[0], ev_cmp_w1_k[0], ev_cmp_w2_k[0],
                              ev_cmp_pos_v[0], ev_cmp_w1_v[0], ev_cmp_w2_v[0],
                              ev_lam_q1[0], ev_lam_k1[0], ev_lam_q2[0], ev_lam_k2[0], ev_subln[0], strips)
    wo = ev_w_out[0].astype(BF16)
    x1 = _post(xf, o_a, o_b, 0, wo[:512], wo[512:], norm_mlp[0], mlp_w1[0].astype(BF16),
               mlp_w2[0].astype(BF16), norm_final, False, 1024, 1024, "post_even")

    o_m = _odd_heads(x1, batch, seq, bias_table, norm_mix[1], od_w_in[0], strips)
    wo = od_w_out[0].astype(BF16)
    out = _post(x1, o_m, o_m, 1, wo[:512], wo[512:], norm_mlp[1], mlp_w1[1].astype(BF16),
                mlp_w2[1].astype(BF16), norm_final, True, 1024, 1024, "post_odd")
    return out.reshape(batch, seq, d)
```

```python
import functools
import math

import numpy as np
import jax
import jax.numpy as jnp
from jax import lax
from jax.experimental import pallas as pl
from jax.experimental.pallas import tpu as pltpu

F32 = jnp.float32
BF16 = jnp.bfloat16

D_MODEL = 1024
HEAD_DIM = 64
EPS = 1e-6
NEG = -1e30
NUM_BUCKETS = 32
MAX_DISTANCE = 128
NSA_HEADS = 8
NSA_KV_HEADS = 2
NSA_GROUP = 4
CMP_BLOCK = 32
CMP_STRIDE = 16
SEL_BLOCK = 64
SEL_TOPN = 16
WINDOW = 512
FORCE_BONUS = 1e3
DIFF_HEADS = 4
MOBA_HEADS = 16
MOBA_BLOCK = 256
MOBA_TOPK = 3
D_FF = 4 * D_MODEL
SCALE = HEAD_DIM ** -0.5

LANES = 128
TQ = 256
TQD = 512
NSA_TILES_PER_STEP = 2
DIFF_TILES_PER_STEP = 4
MOBA_TILES_PER_STEP = 4
TK = 256
FAR_REL = 113
CMP_PER_TQ = TQ // CMP_STRIDE
CMP_NEAR = 128
CMP_PAD = CMP_NEAR - 2 * CMP_PER_TQ
SUB = 2 * TK
BAND_PAD = 3 * TK
FRONT = 2 * TK
LOG2E = math.log2(math.e)
QSCALE = SCALE * LOG2E
VMEM_LIMIT = 56 * 1024 * 1024


def _attention_params():
    return pltpu.CompilerParams(dimension_semantics=("arbitrary", "arbitrary", "arbitrary"),
                                vmem_limit_bytes=VMEM_LIMIT)


def _dot_nt(a, b):
    return lax.dot_general(a, b, (((1,), (1,)), ((), ())), preferred_element_type=F32)


def _dot(a, b):
    return jnp.dot(a, b, preferred_element_type=F32)


def _rms(x, g):
    return x * lax.rsqrt(jnp.mean(x * x, axis=-1, keepdims=True) + EPS) * g


def _t5_bucket_np(rel):
    n = np.maximum(rel, 0)
    max_exact = NUM_BUCKETS // 2
    large = max_exact + (np.log(np.maximum(n, 1).astype(np.float32) / np.float32(max_exact))
                         / np.float32(math.log(MAX_DISTANCE / max_exact))
                         * np.float32(NUM_BUCKETS - max_exact)).astype(np.int32)
    large = np.minimum(large, NUM_BUCKETS - 1)
    return np.where(n < max_exact, n, large).astype(np.int32)


def _strip_kernel(tbl_ref, bucket_ref, valid_ref, o_ref, *, head0):
    h = pl.program_id(0) + head0
    bucket = bucket_ref[...]
    last = tbl_ref[NUM_BUCKETS - 1, h]
    acc = jnp.zeros(bucket.shape, F32)
    for b in range(NUM_BUCKETS - 1):
        acc = jnp.where(bucket == b, tbl_ref[b, h] - last, acc)
    o_ref[0] = jnp.where(valid_ref[...] != 0, acc * LOG2E, NEG)


def _toeplitz_kernel(tbl_ref, bucket_ref, valid_ref, o_ref, *, rows, cols):
    h = pl.program_id(0)
    bucket = bucket_ref[...]
    last = tbl_ref[NUM_BUCKETS - 1, h]
    acc = jnp.zeros(bucket.shape, F32)
    for b in range(NUM_BUCKETS - 1):
        acc = jnp.where(bucket == b, tbl_ref[b, h] - last, acc)
    line = jnp.where(valid_ref[...] != 0, acc * LOG2E, NEG)
    wide = jnp.broadcast_to(line[0:1], (rows, line.shape[1]))
    o_ref[0] = pltpu.roll(wide, 0, 1, stride=1, stride_axis=0)[:, :cols]


def _toeplitz_strips(table, rows, cols, off, lo, hi, nheads, name):
    width = -(-(rows + cols) // LANES) * LANES
    j = np.arange(width)
    rel = off - np.where(j < cols, j, j - width)
    valid = (rel >= lo) & (rel < hi)
    tile8 = lambda a: jnp.asarray(np.broadcast_to(a[None, :], (8, width)).astype(np.int32))
    return pl.pallas_call(
        functools.partial(_toeplitz_kernel, rows=rows, cols=cols),
        out_shape=jax.ShapeDtypeStruct((nheads, rows, cols), F32),
        grid=(nheads,),
        in_specs=[pl.BlockSpec(memory_space=pltpu.SMEM),
                  pl.BlockSpec((8, width), lambda h: (0, 0)),
                  pl.BlockSpec((8, width), lambda h: (0, 0))],
        out_specs=pl.BlockSpec((1, rows, cols), lambda h: (h, 0, 0)),
        name=name,
    )(table, tile8(_t5_bucket_np(rel)), tile8(valid))


def _bias_strips(table, rel, valid, head0, nheads, name):
    rows, cols = rel.shape
    bucket = jnp.asarray(_t5_bucket_np(rel))
    validi = jnp.asarray(valid.astype(np.int32))
    return pl.pallas_call(
        functools.partial(_strip_kernel, head0=head0),
        out_shape=jax.ShapeDtypeStruct((nheads, rows, cols), F32),
        grid=(nheads,),
        in_specs=[pl.BlockSpec(memory_space=pltpu.SMEM),
                  pl.BlockSpec((rows, cols), lambda h: (0, 0)),
                  pl.BlockSpec((rows, cols), lambda h: (0, 0))],
        out_specs=pl.BlockSpec((1, rows, cols), lambda h: (h, 0, 0)),
        name=name,
    )(table, bucket, validi)


def _norm_matmul_kernel(x_ref, g_ref, w_ref, o_ref, xn_ref):
    @pl.when(pl.program_id(1) == 0)
    def _():
        xn_ref[...] = _rms(x_ref[...], g_ref[...]).astype(BF16)

    o_ref[...] = _dot(xn_ref[...], w_ref[...]).astype(o_ref.dtype)


def _norm_matmul(x, g, w, out_dtype, tm, tn, name):
    m, d = x.shape
    n = w.shape[1]
    return pl.pallas_call(
        _norm_matmul_kernel,
        out_shape=jax.ShapeDtypeStruct((m, n), out_dtype),
        grid=(m // tm, n // tn),
        in_specs=[pl.BlockSpec((tm, d), lambda i, j: (i, 0)),
                  pl.BlockSpec((1, d), lambda i, j: (0, 0)),
                  pl.BlockSpec((d, tn), lambda i, j: (0, j))],
        out_specs=pl.BlockSpec((tm, tn), lambda i, j: (i, j)),
        scratch_shapes=[pltpu.VMEM((tm, d), BF16)],
        compiler_params=pltpu.CompilerParams(
            dimension_semantics=("arbitrary", "arbitrary"), vmem_limit_bytes=VMEM_LIMIT),
        name=name,
    )(x, g.reshape(1, d), w)


def _in_proj_even_kernel(x_ref, g_ref, w_ref, wa_ref, o_ref, kc0_ref, kc1_ref, vc0_ref, vc1_ref, gate_ref,
                         xn_ref):
    hd = HEAD_DIM

    @pl.when(pl.program_id(1) == 0)
    def _():
        xn = _rms(x_ref[...], g_ref[...]).astype(BF16)
        xn_ref[...] = xn
        aux = _dot(xn, wa_ref[...])
        kc0_ref[...] = aux[:, 0:hd]
        kc1_ref[...] = aux[:, hd:2 * hd]
        vc0_ref[...] = aux[:, 2 * hd:3 * hd]
        vc1_ref[...] = aux[:, 3 * hd:4 * hd]
        gate_ref[...] = aux[:, 4 * hd:]

    o_ref[...] = _dot(xn_ref[...], w_ref[...]).astype(o_ref.dtype)


def _in_proj_even(x, g, w_main, w_aux, tm, tn):
    m, d = x.shape
    n = w_main.shape[1]
    na = w_aux.shape[1]
    row_spec = lambda width: pl.BlockSpec((tm, width), lambda i, j: (i, 0))
    cmp_shape = jax.ShapeDtypeStruct((m, HEAD_DIM), F32)
    return pl.pallas_call(
        _in_proj_even_kernel,
        out_shape=(jax.ShapeDtypeStruct((m, n), BF16), cmp_shape, cmp_shape, cmp_shape, cmp_shape,
                   jax.ShapeDtypeStruct((m, na - 4 * HEAD_DIM), F32)),
        grid=(m // tm, n // tn),
        in_specs=[row_spec(d),
                  pl.BlockSpec((1, d), lambda i, j: (0, 0)),
                  pl.BlockSpec((d, tn), lambda i, j: (0, j)),
                  pl.BlockSpec((d, na), lambda i, j: (0, 0))],
        out_specs=(pl.BlockSpec((tm, tn), lambda i, j: (i, j)), row_spec(HEAD_DIM), row_spec(HEAD_DIM),
                   row_spec(HEAD_DIM), row_spec(HEAD_DIM), row_spec(na - 4 * HEAD_DIM)),
        scratch_shapes=[pltpu.VMEM((tm, d), BF16)],
        compiler_params=pltpu.CompilerParams(
            dimension_semantics=("arbitrary", "arbitrary"), vmem_limit_bytes=VMEM_LIMIT),
        name="in_proj_even",
    )(x, g.reshape(1, d), w_main, w_aux)


def _post_kernel(x_ref, a0_ref, a1_ref, wo0_ref, wo1_ref, g_ref, w1_ref, w2_ref, gf_ref,
                 o_ref, acc_ref, xn_ref, *, final_norm):
    f = pl.program_id(1)

    @pl.when(f == 0)
    def _():
        x1 = x_ref[...] + _dot(a0_ref[...], wo0_ref[...]) + _dot(a1_ref[...], wo1_ref[...])
        acc_ref[...] = x1
        xn_ref[...] = _rms(x1, g_ref[...]).astype(BF16)

    h = jnp.square(jnp.maximum(_dot(xn_ref[...], w1_ref[...]), 0.0))
    acc_ref[...] += _dot(h.astype(BF16), w2_ref[...])

    @pl.when(f == pl.num_programs(1) - 1)
    def _():
        y = acc_ref[...]
        if final_norm:
            y = _rms(y, gf_ref[...])
        o_ref[...] = y


def _post(x, a0, a1, a1_colblk, wo0, wo1, g, w1, w2, gf, final_norm, tm, tf, name):
    m, d = x.shape
    k0 = wo0.shape[0]
    k1 = wo1.shape[0]
    ff = w1.shape[1]
    return pl.pallas_call(
        functools.partial(_post_kernel, final_norm=final_norm),
        out_shape=jax.ShapeDtypeStruct((m, d), F32),
        grid=(m // tm, ff // tf),
        in_specs=[pl.BlockSpec((tm, d), lambda i, f: (i, 0)),
                  pl.BlockSpec((tm, k0), lambda i, f: (i, 0)),
                  pl.BlockSpec((tm, k1), lambda i, f: (i, a1_colblk)),
                  pl.BlockSpec((k0, d), lambda i, f: (0, 0)),
                  pl.BlockSpec((k1, d), lambda i, f: (0, 0)),
                  pl.BlockSpec((1, d), lambda i, f: (0, 0)),
                  pl.BlockSpec((d, tf), lambda i, f: (0, f)),
                  pl.BlockSpec((tf, d), lambda i, f: (f, 0)),
                  pl.BlockSpec((1, d), lambda i, f: (0, 0))],
        out_specs=pl.BlockSpec((tm, d), lambda i, f: (i, 0)),
        scratch_shapes=[pltpu.VMEM((tm, d), F32), pltpu.VMEM((tm, d), BF16)],
        compiler_params=pltpu.CompilerParams(
            dimension_semantics=("arbitrary", "arbitrary"), vmem_limit_bytes=VMEM_LIMIT),
        name=name,
    )(x, a0, a1, wo0, wo1, g.reshape(1, d), w1, w2, gf.reshape(1, d))


def _compress_kernel(kc0_ref, kc1_ref, vc0_ref, vc1_ref, pos_ref, w1_ref, w2_ref, o_ref):
    half = CMP_STRIDE * HEAD_DIM
    for kv, refs in enumerate(((kc0_ref, kc1_ref), (vc0_ref, vc1_ref))):
        w1a = w1_ref[kv, :half, :]
        w1b = w1_ref[kv, half:, :]
        pos = pos_ref[kv].astype(BF16)
        pa = jnp.broadcast_to(pos[0:1], (8, half))
        pb = jnp.broadcast_to(pos[1:2], (8, half))
        posterm = (_dot(pa, w1a) + _dot(pb, w1b))[0:1]
        for grp, x_ref in enumerate(refs):
            x = x_ref[0].astype(BF16)
            first = _dot(x, w1a)
            second = _dot(x, w1b)
            second = pltpu.roll(second, second.shape[0] - 1, 0)
            hid = jax.nn.gelu(first + second + posterm)
            o_ref[kv, grp] = _dot(hid.astype(BF16), w2_ref[kv])


def _compress(kc0, kc1, vc0, vc1, pos, w1, w2):
    batch, nch, wide = kc0.shape
    x_spec = pl.BlockSpec((1, nch, wide), lambda b: (b, 0, 0))
    full = lambda a: pl.BlockSpec(a.shape, lambda b: (0,) * a.ndim)
    return pl.pallas_call(
        _compress_kernel,
        out_shape=jax.ShapeDtypeStruct((2, batch * NSA_KV_HEADS, nch, HEAD_DIM), F32),
        grid=(batch,),
        in_specs=[x_spec, x_spec, x_spec, x_spec, full(pos), full(w1), full(w2)],
        out_specs=pl.BlockSpec((2, NSA_KV_HEADS, nch, HEAD_DIM), lambda b: (0, b, 0, 0)),
        compiler_params=pltpu.CompilerParams(
            dimension_semantics=("arbitrary",), vmem_limit_bytes=VMEM_LIMIT),
        name="nsa_compress",
    )(kc0, kc1, vc0, vc1, pos, w1, w2)


def _softmax_part(m, s):
    m_new = jnp.maximum(m, jnp.max(s, axis=1, keepdims=True))
    return m_new, jnp.exp2(m - m_new), jnp.exp2(s - m_new).astype(BF16)


def _pv(p, v_sc, r0, width):
    if not isinstance(v_sc, (tuple, list)):
        return _dot(p, v_sc[pl.ds(r0, width), :])
    share = p.shape[0] // len(v_sc)
    return jnp.concatenate([_dot(p[g * share:(g + 1) * share], v[pl.ds(r0, width), :])
                            for g, v in enumerate(v_sc)], axis=0)


def _flash_chain(q, k_sc, v_sc, bufs, far0, n_far, tail, side_work=None):
    s_sc, p_sc, acc_sc = bufs
    rows = q.shape[0]
    assert SUB == 2 * TK and tail[0][1] == SUB
    tail0 = tail[0][0]

    def qk(r0):
        return _dot_nt(q, k_sc[pl.ds(pl.multiple_of(r0, TK), SUB), :])

    n_steps = n_far // 2
    s_sc[...] = qk(jnp.where(n_steps > 0, far0, tail0))
    p_sc[...] = jnp.zeros(p_sc.shape, BF16)
    acc_sc[...] = jnp.zeros(acc_sc.shape, F32)

    def trips(start, n_sub, n_trips, after):
        def body(j, m):
            base = pl.multiple_of(start + j * (n_sub * SUB), SUB)
            acc = acc_sc[...] + _pv(p_sc[...], v_sc, base - SUB, SUB)
            s_cur = s_sc[...]
            for u in range(n_sub):
                m, alpha, p = _softmax_part(m, s_cur)
                if u + 1 < n_sub:
                    s_cur = qk(base + (u + 1) * SUB)
                    acc = alpha * acc + _pv(p, v_sc, base + u * SUB, SUB)
                else:
                    s_sc[...] = qk(jnp.where(j + 1 < n_trips, base + n_sub * SUB, after))
                    p_sc[...] = p
                    acc_sc[...] = alpha * acc
            return m
        return body

    m = jnp.full((rows, 1), NEG, F32)
    start, left = far0, n_steps
    for n_sub in (4, 2, 1):
        n_trips = left // n_sub
        left = left - n_trips * n_sub
        nxt = start + n_trips * (n_sub * SUB)
        after = jnp.where(left > 0, nxt, tail0)
        m = lax.fori_loop(0, n_trips, trips(start, n_sub, n_trips, after), m)
        start = nxt
    if side_work is not None:
        side_work()
    base = pl.multiple_of(far0 + n_steps * SUB, SUB)
    acc = acc_sc[...] + _pv(p_sc[...], v_sc, base - SUB, SUB)
    m, alpha, p = _softmax_part(m, s_sc[...] + tail[0][2]())
    acc = alpha * acc + _pv(p, v_sc, tail0, SUB)
    return _flash_tail((m, acc), q, k_sc, v_sc, tail[1:])


def _flash_buffers(rows, ncols):
    return [pltpu.VMEM((rows, SUB), F32), pltpu.VMEM((rows, SUB), BF16), pltpu.VMEM((rows, ncols), F32)]


def _flash_tail(carry, q, k_sc, v_sc, tail):
    m, acc = carry
    for r0, width, extra in tail:
        m, alpha, p = _softmax_part(m, _dot_nt(q, k_sc[pl.ds(r0, width), :]) + extra())
        acc = alpha * acc + _pv(p, v_sc, r0, width)
    return acc


def _tile_gate(cond):
    return jnp.where(cond, 0.0, NEG)


def _tail_steps(win, odd_ok, rows, strip, near_ok):
    def first():
        return jnp.concatenate([jnp.broadcast_to(_tile_gate(odd_ok), (rows, TK)),
                                strip(0, TK) + _tile_gate(near_ok)], axis=1)

    return [(win, SUB, first), (win + SUB, SUB, lambda: strip(TK, 3 * TK))]


def _topk_rows(score, idx, k, floor):
    sel = jnp.zeros(score.shape, F32)
    nrow = score.shape[0]
    for _ in range(k):
        mx = jnp.max(score, axis=0, keepdims=True)
        first = jnp.min(jnp.where(score == mx, idx, nrow), axis=0, keepdims=True)
        pick = idx == first
        sel = jnp.where(pick, 1.0, sel)
        score = jnp.where(pick, floor, score)
    return sel


def _nsa_kernel(q_ref, qn_ref, ks_ref, vs_ref, kw_ref, vw_ref, kc_ref, vc_ref, gate_ref,
                ssel_ref, swin_ref, scmp_ref, ov_ref, o_ref,
                kaug, vaug, kwp, vwp, kcp, vcp, qaug_sc, ocw_sc, *bufs, seq):
    g = pl.program_id(1)
    step = pl.program_id(2)
    ncp = seq // CMP_STRIDE
    hd = HEAD_DIM
    rows = NSA_GROUP * TQ

    @pl.when(step == 0)
    def _():
        lane = lax.broadcasted_iota(jnp.int32, (BAND_PAD, LANES), 1)
        padflag = jnp.where(lane == hd, 1.0, 0.0).astype(BF16)
        kaug[0:BAND_PAD, 0:LANES] = jnp.zeros((BAND_PAD, LANES), BF16)
        kaug[0:BAND_PAD, LANES:2 * LANES] = padflag
        kwp[0:BAND_PAD, :] = padflag
        vaug[0:BAND_PAD, :] = jnp.zeros((BAND_PAD, LANES), BF16)
        vwp[0:BAND_PAD, :] = jnp.zeros((BAND_PAD, LANES), BF16)
        onescol = jnp.where(lax.broadcasted_iota(jnp.int32, (TK, LANES), 1) == hd, 1.0, 0.0)
        zero_hd = jnp.zeros((TK, hd), F32)
        col = lax.broadcasted_iota(jnp.int32, (TK, LANES), 1)
        row = lax.broadcasted_iota(jnp.int32, (TK, LANES), 0)

        def fill(c, _):
            src = pl.multiple_of(c * TK, TK)
            dst = pl.multiple_of(BAND_PAD + c * TK, TK)

            def pick(ref):
                both = ref[pl.ds(src, TK), :].astype(F32)
                return jnp.concatenate([jnp.where(g == 0, both[:, :hd], both[:, hd:]), zero_hd], axis=1)

            kaug[pl.ds(dst, TK), 0:LANES] = jnp.where((src + row) // SEL_BLOCK == col, 1.0, 0.0).astype(BF16)
            kaug[pl.ds(dst, TK), LANES:2 * LANES] = pick(ks_ref).astype(BF16)
            kwp[pl.ds(dst, TK), :] = pick(kw_ref).astype(BF16)
            vaug[pl.ds(dst, TK), :] = (pick(vs_ref) + onescol).astype(BF16)
            vwp[pl.ds(dst, TK), :] = (pick(vw_ref) + onescol).astype(BF16)
            return 0

        lax.fori_loop(0, seq // TK, fill, 0)
        lane_c = lax.broadcasted_iota(jnp.int32, (CMP_PAD, LANES), 1)
        kcp[0:CMP_PAD, :] = jnp.where(lane_c == hd, 1.0, 0.0).astype(BF16)
        vcp[0:CMP_PAD, :] = jnp.zeros((CMP_PAD, LANES), BF16)
        zc = jnp.zeros((ncp, hd), F32)
        onesc = jnp.where(lax.broadcasted_iota(jnp.int32, (ncp, LANES), 1) == hd, 1.0, 0.0)
        kcp[CMP_PAD:CMP_PAD + ncp, :] = jnp.concatenate([kc_ref[0, 0], zc], axis=1).astype(BF16)
        vcp[CMP_PAD:CMP_PAD + ncp, :] = (jnp.concatenate([vc_ref[0, 0], zc], axis=1) + onesc).astype(BF16)
        tail = kcp.shape[0] - CMP_PAD - ncp
        kcp[CMP_PAD + ncp:, :] = jnp.zeros((tail, LANES), BF16)
        vcp[CMP_PAD + ncp:, :] = jnp.zeros((tail, LANES), BF16)

    def stack(ref, lo, hi):
        return jnp.concatenate([ref[r, :, lo:hi] for r in range(NSA_GROUP)], axis=0)

    def prepare(qsrc_ref, t, slot):
        qblk = qsrc_ref[...].astype(F32) * QSCALE
        lane_t = lax.broadcasted_iota(jnp.int32, (TQ, hd), 1)
        qtail = jnp.where(lane_t == 0, NEG, 0.0)
        qs = [qblk[:, r * hd:(r + 1) * hd] for r in range(NSA_GROUP)]
        q128 = jnp.concatenate([jnp.concatenate([qs[r], qtail], axis=1) for r in range(NSA_GROUP)],
                               axis=0).astype(BF16)

        s_far = _dot_nt(q128, kcp[CMP_PAD:CMP_PAD + ncp, :])
        c_idx = lax.broadcasted_iota(jnp.int32, (1, ncp), 1)
        far_ok = c_idx < CMP_PER_TQ * t - CMP_PAD
        near0 = pl.multiple_of(CMP_PER_TQ * t, CMP_PER_TQ)
        s_near = _dot_nt(q128, kcp[pl.ds(near0, CMP_NEAR), :]) + stack(scmp_ref, 0, CMP_NEAR)
        near_ok = s_near > 0.5 * NEG
        s_far = jnp.where(far_ok, s_far, NEG)
        m_c = jnp.maximum(jnp.max(s_far, axis=1, keepdims=True), jnp.max(s_near, axis=1, keepdims=True))
        p_far = jnp.where(far_ok, jnp.exp2(s_far - m_c), 0.0)
        p_near = jnp.where(near_ok, jnp.exp2(s_near - m_c), 0.0)
        den = jnp.maximum(jnp.sum(p_far, axis=1, keepdims=True) + jnp.sum(p_near, axis=1, keepdims=True), 1e-30)
        pc_far = p_far / den
        pc_near = p_near / den
        ocw_sc[slot, 0] = (_dot(pc_far.astype(BF16), vcp[CMP_PAD:CMP_PAD + ncp, :])
                           + _dot(pc_near.astype(BF16), vcp[pl.ds(near0, CMP_NEAR), :]))

        band0 = pl.multiple_of(TQ * t, TK) + BAND_PAD - 2 * TK
        steps = [(band0, 2 * TK, lambda: stack(swin_ref, 0, 2 * TK)),
                 (band0 + 2 * TK, TK, lambda: stack(swin_ref, 2 * TK, 3 * TK))]
        acc = _flash_tail((jnp.full((rows, 1), NEG, F32), jnp.zeros((rows, LANES), F32)), q128, kwp, vwp, steps)
        ocw_sc[slot, 1] = acc / acc[:, hd:hd + 1]

        pcs_far = sum(pc_far[r * TQ:(r + 1) * TQ] for r in range(NSA_GROUP))
        pcs_near = sum(pc_near[r * TQ:(r + 1) * TQ] for r in range(NSA_GROUP))

        def hi_lo_dot(p, w):
            hi = p.astype(BF16)
            lo = (p - hi.astype(F32)).astype(BF16)
            return _dot(hi, w) + _dot(lo, w)

        imp = (hi_lo_dot(pcs_far, ov_ref[CMP_PAD:CMP_PAD + ncp, :])
               + hi_lo_dot(pcs_near, ov_ref[pl.ds(near0, CMP_NEAR), :]))
        imp_t = imp.T
        n_idx = lax.broadcasted_iota(jnp.int32, (LANES, TQ), 0)
        cur = (TQ * t + lax.broadcasted_iota(jnp.int32, (LANES, TQ), 1)) // SEL_BLOCK
        eligible = n_idx <= cur
        forced = ((n_idx == 0) | (n_idx == cur) | (n_idx == cur - 1)) & eligible
        rest = jnp.where(forced, -3.0e38, jnp.where(eligible, imp_t, -1.0))
        sel_t = _topk_rows(rest, n_idx, SEL_TOPN - 3, -3.0e38)
        maskneg = jnp.where(forced | (sel_t > 0.5), 0.0, NEG).T
        qaug_sc[slot] = jnp.concatenate(
            [jnp.concatenate([maskneg, qs[r], qtail], axis=1) for r in range(NSA_GROUP)],
            axis=0).astype(BF16)

    @pl.when(step == 0)
    def _():
        prepare(q_ref.at[0:TQ], 0, 0)

    n_tiles = NSA_TILES_PER_STEP * pl.num_programs(2)
    for part in range(NSA_TILES_PER_STEP):
        i = step * NSA_TILES_PER_STEP + part
        slot = part
        tok = slice(part * TQ, (part + 1) * TQ)
        n_far = jnp.maximum(i - 2, 0)
        win = pl.multiple_of(TQ * i, TK)
        tail = _tail_steps(win, n_far % 2 == 1, rows, functools.partial(stack, ssel_ref), True)
        if part + 1 < NSA_TILES_PER_STEP:
            prepare_next = functools.partial(prepare, q_ref.at[(part + 1) * TQ:(part + 2) * TQ], i + 1, slot + 1)
        else:
            prepare_next = functools.partial(prepare, qn_ref, jnp.minimum(i + 1, n_tiles - 1), 0)
        acc = _flash_chain(qaug_sc[slot], kaug, vaug, bufs[3 * part:3 * part + 3], BAND_PAD, n_far, tail,
                           prepare_next)
        o_sel = acc[:, :hd] / acc[:, hd:hd + 1]

        o_cmp = ocw_sc[slot, 0]
        o_win = ocw_sc[slot, 1]
        sig = jax.nn.sigmoid(gate_ref[tok, :])
        outs = []
        for r in range(NSA_GROUP):
            def gcol(branch):
                c0 = branch * NSA_HEADS + r
                c1 = c0 + NSA_GROUP
                return jnp.where(g == 0, sig[:, c0:c0 + 1], sig[:, c1:c1 + 1])
            sl = slice(r * TQ, (r + 1) * TQ)
            outs.append(gcol(0) * o_cmp[sl, :hd] + gcol(1) * o_sel[sl] + gcol(2) * o_win[sl, :hd])
        o_ref[tok, :] = jnp.concatenate(outs, axis=1).astype(o_ref.dtype)


def _nsa(main, aux, ckv, ssel, swin, scmp, ov, batch, seq):
    T = NSA_TILES_PER_STEP
    per_step = T * TQ
    nq = seq // TQ
    ns = seq // per_step
    ncp = seq // CMP_STRIDE
    gw = NSA_GROUP * HEAD_DIM
    once = pl.Buffered(1)
    kv_spec = lambda colblk: pl.BlockSpec((seq, LANES), lambda b, g, i: (b, colblk), pipeline_mode=once)
    strip_spec = lambda cols: pl.BlockSpec((NSA_GROUP, TQ, cols), lambda b, g, i: (g, 0, 0), pipeline_mode=once)
    return pl.pallas_call(
        functools.partial(_nsa_kernel, seq=seq),
        out_shape=jax.ShapeDtypeStruct((batch * seq, NSA_HEADS * HEAD_DIM), BF16),
        grid=(batch, NSA_KV_HEADS, ns),
        in_specs=[pl.BlockSpec((per_step, gw), lambda b, g, i: (b * ns + i, g)),
                  pl.BlockSpec((TQ, gw), lambda b, g, i: (b * nq + jnp.minimum(T * (i + 1), nq - 1), g)),
                  kv_spec(4), kv_spec(5), kv_spec(6), kv_spec(7),
                  pl.BlockSpec((1, 1, ncp, HEAD_DIM), lambda b, g, i: (0, b * NSA_KV_HEADS + g, 0, 0)),
                  pl.BlockSpec((1, 1, ncp, HEAD_DIM), lambda b, g, i: (1, b * NSA_KV_HEADS + g, 0, 0)),
                  pl.BlockSpec((per_step, LANES), lambda b, g, i: (b * ns + i, 0)),
                  strip_spec(3 * TK), strip_spec(3 * TK), strip_spec(CMP_NEAR),
                  pl.BlockSpec(ov.shape, lambda b, g, i: (0, 0))],
        out_specs=pl.BlockSpec((per_step, gw), lambda b, g, i: (b * ns + i, g)),
        scratch_shapes=[pltpu.VMEM((BAND_PAD + seq, 2 * LANES), BF16),
                        pltpu.VMEM((BAND_PAD + seq, LANES), BF16),
                        pltpu.VMEM((BAND_PAD + seq, LANES), BF16),
                        pltpu.VMEM((BAND_PAD + seq, LANES), BF16),
                        pltpu.VMEM((ov.shape[0], LANES), BF16),
                        pltpu.VMEM((ov.shape[0], LANES), BF16),
                        pltpu.VMEM((T, NSA_GROUP * TQ, 2 * LANES), BF16),
                        pltpu.VMEM((T, 2, NSA_GROUP * TQ, LANES), F32),
                        *(T * _flash_buffers(NSA_GROUP * TQ, LANES))],
        compiler_params=_attention_params(),
        name="nsa_attention",
    )(main, main, main, main, main, main, ckv, ckv, aux, ssel, swin, scmp, ov)


def _ones_column():
    return jnp.where(lax.broadcasted_iota(jnp.int32, (TK, LANES), 1) == 0, 1.0, 0.0).astype(BF16)


def _causal_flash(q, k_sc, v_sc, bufs, strip, i):
    n_far = jnp.maximum(2 * i - 1, 0)
    win = pl.multiple_of(i * TQD, SUB)
    tail = _tail_steps(win, i >= 1, q.shape[0], strip, i >= 1)
    return _flash_chain(q, k_sc, v_sc, bufs, FRONT, n_far, tail)


def _diff_kernel(q_ref, k_ref, v_ref, lam_ref, sub_ref, strip_ref, o_ref, k_sc, v_sc, *bufs, lam_init, seq):
    step = pl.program_id(2)
    hd = HEAD_DIM

    @pl.when(step == 0)
    def _():
        k_sc[0:FRONT, :] = jnp.zeros((FRONT, k_sc.shape[1]), BF16)
        v_sc[0:FRONT, :] = jnp.zeros((FRONT, v_sc.shape[1]), BF16)
        ones = _ones_column()

        def fill(c, _):
            r0 = pl.multiple_of(c * TK, TK)
            k_sc[pl.ds(FRONT + r0, TK), :] = k_ref[pl.ds(r0, TK), :]
            v_sc[pl.ds(FRONT + r0, TK), 0:LANES] = v_ref[pl.ds(r0, TK), :]
            v_sc[pl.ds(FRONT + r0, TK), LANES:2 * LANES] = ones
            return 0

        lax.fori_loop(0, seq // TK, fill, 0)

    lane = lax.broadcasted_iota(jnp.int32, (TQD, 2 * hd), 1)
    strip = lambda lo, hi: jnp.concatenate([strip_ref[0, :, lo:hi]] * 2, axis=0)
    lam_p = lam_ref[...]
    lam = (jnp.exp(jnp.sum(lam_p[0:1] * lam_p[1:2], axis=1, keepdims=True))
           - jnp.exp(jnp.sum(lam_p[2:3] * lam_p[3:4], axis=1, keepdims=True)) + lam_init)

    for part in range(DIFF_TILES_PER_STEP):
        tile_rows = slice(part * TQD, (part + 1) * TQD)
        q = q_ref[tile_rows, :].astype(F32) * QSCALE
        qst = jnp.concatenate([jnp.where(lane < hd, q, 0.0), jnp.where(lane < hd, 0.0, q)],
                              axis=0).astype(BF16)
        acc = _causal_flash(qst, k_sc, v_sc, bufs[3 * part:3 * part + 3], strip, step * DIFF_TILES_PER_STEP + part)
        o = acc[:, :2 * hd] / acc[:, 2 * hd:2 * hd + 1]
        d = o[:TQD] - lam * o[TQD:]
        o_ref[tile_rows, :] = (_rms(d, sub_ref[...]) * (1.0 - lam_init)).astype(o_ref.dtype)


def _diff(main, lam_p, subln, strips, batch, seq, lam_init):
    per_step = DIFF_TILES_PER_STEP * TQD
    nq = seq // per_step
    blk = 2 * HEAD_DIM
    q0, k0, v0 = 1024 // blk, 1536 // blk, 2048 // blk
    return pl.pallas_call(
        functools.partial(_diff_kernel, lam_init=lam_init, seq=seq),
        out_shape=jax.ShapeDtypeStruct((batch * seq, DIFF_HEADS * blk), BF16),
        grid=(batch, DIFF_HEADS, nq),
        in_specs=[pl.BlockSpec((per_step, blk), lambda b, h, i: (b * nq + i, q0 + h)),
                  pl.BlockSpec((seq, blk), lambda b, h, i: (b, k0 + h)),
                  pl.BlockSpec((seq, blk), lambda b, h, i: (b, v0 + h)),
                  pl.BlockSpec((4, HEAD_DIM), lambda b, h, i: (0, 0)),
                  pl.BlockSpec((1, blk), lambda b, h, i: (0, 0)),
                  pl.BlockSpec((1, TQD, 3 * TK), lambda b, h, i: (NSA_HEADS + h, 0, 0))],
        out_specs=pl.BlockSpec((per_step, blk), lambda b, h, i: (b * nq + i, h)),
        scratch_shapes=[pltpu.VMEM((FRONT + seq, LANES), BF16), pltpu.VMEM((FRONT + seq, 2 * LANES), BF16),
                        *(DIFF_TILES_PER_STEP * _flash_buffers(2 * TQD, 2 * LANES))],
        compiler_params=_attention_params(),
        name="diff_attention",
    )(main, main, main, lam_p, subln.reshape(1, blk), strips)


def _moba_kernel(q_ref, k_ref, v_ref, strip_ref, o_ref, kaug, va_sc, vb_sc, kmean, *bufs, seq):
    step = pl.program_id(2)
    hd = HEAD_DIM
    nb = seq // MOBA_BLOCK

    @pl.when(step == 0)
    def _():
        kmean[...] = jnp.zeros(kmean.shape, F32)
        kaug[0:FRONT, :] = jnp.zeros((FRONT, kaug.shape[1]), BF16)
        va_sc[0:FRONT, :] = jnp.zeros((FRONT, LANES), BF16)
        vb_sc[0:FRONT, :] = jnp.zeros((FRONT, LANES), BF16)
        col = lax.broadcasted_iota(jnp.int32, (MOBA_BLOCK, LANES), 1)
        ones_tail = jnp.where(lax.broadcasted_iota(jnp.int32, (MOBA_BLOCK, hd), 1) == 0, 1.0, 0.0).astype(BF16)

        def fill(n, _):
            r0 = pl.multiple_of(n * MOBA_BLOCK, MOBA_BLOCK)
            kb = k_ref[pl.ds(r0, MOBA_BLOCK), :]
            vb = v_ref[pl.ds(r0, MOBA_BLOCK), :]
            kaug[pl.ds(FRONT + r0, MOBA_BLOCK), 0:LANES] = jnp.where(col == n, 1.0, 0.0).astype(BF16)
            kaug[pl.ds(FRONT + r0, MOBA_BLOCK), LANES:2 * LANES] = kb
            va_sc[pl.ds(FRONT + r0, MOBA_BLOCK), :] = jnp.concatenate([vb[:, :hd], ones_tail], axis=1)
            vb_sc[pl.ds(FRONT + r0, MOBA_BLOCK), :] = jnp.concatenate([vb[:, hd:], ones_tail], axis=1)
            kmean[pl.ds(n, 1), :] = jnp.mean(kb.astype(F32), axis=0, keepdims=True)
            return 0

        lax.fori_loop(0, nb, fill, 0)

    lane = lax.broadcasted_iota(jnp.int32, (TQD, LANES), 1)
    n_idx = lax.broadcasted_iota(jnp.int32, (nb, TQD), 0)
    km = kmean[0:nb, :]
    lane_k = lax.broadcasted_iota(jnp.int32, km.shape, 1)
    no_block = jnp.full((LANES - nb, TQD), NEG, F32)
    strip = lambda lo, hi: jnp.concatenate([strip_ref[0, :, lo:hi], strip_ref[1, :, lo:hi]], axis=0)

    for part in range(MOBA_TILES_PER_STEP):
        i = step * MOBA_TILES_PER_STEP + part
        tile_rows = slice(part * TQD, (part + 1) * TQD)
        qblk = q_ref[tile_rows, :]
        qf = qblk.astype(F32) * QSCALE
        own = (i * TQD + lax.broadcasted_iota(jnp.int32, (nb, TQD), 1)) // MOBA_BLOCK
        past = n_idx < own
        qrows = []
        for hh in range(2):
            mine = (lane < hd) if hh == 0 else (lane >= hd)
            mine_k = (lane_k < hd) if hh == 0 else (lane_k >= hd)
            gate_t = _dot_nt(jnp.where(mine_k, km, 0.0).astype(BF16), qblk)
            sel_t = _topk_rows(jnp.where(past, gate_t, NEG), n_idx, MOBA_TOPK, -3.0e38)
            allowed = ((sel_t > 0.5) & past) | (n_idx == own)
            maskneg = jnp.concatenate([jnp.where(allowed, 0.0, NEG), no_block], axis=0).T
            qrows.append(jnp.concatenate([maskneg, jnp.where(mine, qf, 0.0)], axis=1))
        qaug = jnp.concatenate(qrows, axis=0).astype(BF16)
        acc = _causal_flash(qaug, kaug, (va_sc, vb_sc), bufs[3 * part:3 * part + 3], strip, i)
        o = acc[:, :hd] / acc[:, hd:hd + 1]
        o_ref[tile_rows, :] = jnp.concatenate([o[:TQD], o[TQD:]], axis=1).astype(o_ref.dtype)


def _moba(main, strips, batch, seq):
    per_step = MOBA_TILES_PER_STEP * TQD
    nq = seq // per_step
    pairs = MOBA_HEADS // 2
    return pl.pallas_call(
        functools.partial(_moba_kernel, seq=seq),
        out_shape=jax.ShapeDtypeStruct((batch * seq, MOBA_HEADS * HEAD_DIM), BF16),
        grid=(batch, pairs, nq),
        in_specs=[pl.BlockSpec((per_step, LANES), lambda b, h, i: (b * nq + i, h)),
                  pl.BlockSpec((seq, LANES), lambda b, h, i: (b, pairs + h)),
                  pl.BlockSpec((seq, LANES), lambda b, h, i: (b, 2 * pairs + h)),
                  pl.BlockSpec((2, TQD, 3 * TK), lambda b, h, i: (h, 0, 0))],
        out_specs=pl.BlockSpec((per_step, LANES), lambda b, h, i: (b * nq + i, h)),
        scratch_shapes=[pltpu.VMEM((FRONT + seq, 2 * LANES), BF16), pltpu.VMEM((FRONT + seq, LANES), BF16),
                        pltpu.VMEM((FRONT + seq, LANES), BF16),
                        pltpu.VMEM((LANES, LANES), F32),
                        *(MOBA_TILES_PER_STEP * _flash_buffers(2 * TQD, LANES))],
        compiler_params=_attention_params(),
        name="moba_attention",
    )(main, main, main, strips)


def _cmp_strip_geometry():
    r = np.arange(TQ)[:, None]
    cc = np.arange(CMP_NEAR)[None, :]
    rel_c = r + CMP_STRIDE * CMP_PAD - (CMP_BLOCK - 1) - CMP_STRIDE * cc
    return rel_c, rel_c >= 0


def _overlap_padded(seq):
    nc = (seq - CMP_BLOCK) // CMP_STRIDE + 1
    ncp = seq // CMP_STRIDE
    tok = np.arange(nc)[:, None] * CMP_STRIDE + np.arange(CMP_BLOCK)[None, :]
    ov = np.zeros((CMP_PAD + ncp + 2 * CMP_PER_TQ, LANES), np.float32)
    np.add.at(ov, (CMP_PAD + np.repeat(np.arange(nc), CMP_BLOCK), (tok // SEL_BLOCK).reshape(-1)),
              1.0 / CMP_BLOCK)
    return jnp.asarray(ov, dtype=BF16)


def _strips(bias_table):
    rel_c, valid_c = _cmp_strip_geometry()
    far = 1 << 30
    return (_toeplitz_strips(bias_table, TQD, 3 * TK, TK, 0, far, MOBA_HEADS, "bias_strip_diag"),
            _toeplitz_strips(bias_table, TQ, 3 * TK, 2 * TK, 0, far, NSA_HEADS, "bias_strip_sel"),
            _toeplitz_strips(bias_table, TQ, 3 * TK, 2 * TK, 0, WINDOW, NSA_HEADS, "bias_strip_win"),
            _bias_strips(bias_table, rel_c, valid_c, 0, NSA_HEADS, "bias_strip_cmp"))


def _even_heads(xf, batch, seq, bias_table, g_mix, w_in, pos_k, w1_k, w2_k, pos_v, w1_v, w2_v,
                lam_q1, lam_k1, lam_q2, lam_k2, subln, strips=None):
    d = xf.shape[1]
    strip_a, strip_sel, strip_win, strip_cmp = strips if strips is not None else _strips(bias_table)
    w_main = jnp.concatenate([w_in[:, 0:512], w_in[:, 768:1280], w_in[:, 1304:2840]], axis=1).astype(BF16)
    w_aux = jnp.concatenate([w_in[:, 512:768], w_in[:, 1280:1304],
                             jnp.zeros((d, 3 * LANES - 280), F32)], axis=1).astype(BF16)
    main0, kc0, kc1, vc0, vc1, gates = _in_proj_even(xf, g_mix, w_main, w_aux, 1024, w_main.shape[1] // 2)

    nch = seq // CMP_STRIDE
    chunks = lambda a: a.reshape(batch, nch, CMP_STRIDE * HEAD_DIM)
    pos = jnp.stack([pos_k, pos_v]).reshape(2, 2, CMP_STRIDE * HEAD_DIM)
    ckv = _compress(chunks(kc0), chunks(kc1), chunks(vc0), chunks(vc1), pos,
                    jnp.stack([w1_k, w1_v]).astype(BF16), jnp.stack([w2_k, w2_v]).astype(BF16))

    o_a = _nsa(main0, gates, ckv, strip_sel, strip_win, strip_cmp, _overlap_padded(seq), batch, seq)
    lam_p = jnp.stack([lam_q1, lam_k1, lam_q2, lam_k2])
    lam_init = 0.8 - 0.6 * math.exp(-0.3 * 0)
    o_b = _diff(main0, lam_p, subln, strip_a, batch, seq, lam_init)
    return o_a, o_b, ckv


def _odd_heads(xf, batch, seq, bias_table, g_mix, w_in, strips=None):
    strip_a = (strips if strips is not None else _strips(bias_table))[0]
    main1 = _norm_matmul(xf, g_mix, w_in.astype(BF16), BF16, 1024, w_in.shape[1] // 2, "in_proj_odd")
    return _moba(main1, strip_a, batch, seq)


def kernel(x, bias_table, norm_mix, norm_mlp, norm_final, mlp_w1, mlp_w2, ev_w_in, ev_w_out,
           ev_cmp_pos_k, ev_cmp_w1_k, ev_cmp_w2_k, ev_cmp_pos_v, ev_cmp_w1_v, ev_cmp_w2_v,
           ev_lam_q1, ev_lam_k1, ev_lam_q2, ev_lam_k2, ev_subln, od_w_in, od_w_out):
    batch, seq, d = x.shape
    assert d == D_MODEL and (batch * seq) % 1024 == 0
    assert seq % (max(DIFF_TILES_PER_STEP, MOBA_TILES_PER_STEP) * TQD) == 0
    assert seq % (NSA_TILES_PER_STEP * TQ) == 0
    assert SEL_TOPN * SEL_BLOCK <= seq <= LANES * SEL_BLOCK and seq >= MOBA_TOPK * MOBA_BLOCK
    xf = x.reshape(batch * seq, d)
    strips = _strips(bias_table)

    o_a, o_b, _ = _even_heads(xf, batch, seq, bias_table, norm_mix[0], ev_w_in[0],
                              ev_cmp_pos_k[0], ev_cmp_w1_k[0], ev_cmp_w2_k[0],
                              ev_cmp_pos_v[0], ev_cmp_w1_v[0], ev_cmp_w2_v[0],
                              ev_lam_q1[0], ev_lam_k1[0], ev_lam_q2[0], ev_lam_k2[0], ev_subln[0], strips)
    wo = ev_w_out[0].astype(BF16)
    x1 = _post(xf, o_a, o_b, 0, wo[:512], wo[512:], norm_mlp[0], mlp_w1[0].astype(BF16),
               mlp_w2[0].astype(BF16), norm_final, False, 1024, 1024, "post_even")

    o_m = _odd_heads(x1, batch, seq, bias_table, norm_mix[1], od_w_in[0], strips)
    wo = od_w_out[0].astype(BF16)
    out = _post(x1, o_m, o_m, 1, wo[:512], wo[512:], norm_mlp[1], mlp_w1[1].astype(BF16),
                mlp_w2[1].astype(BF16), norm_final, True, 1024, 1024, "post_odd")
    return out.reshape(batch, seq, d)
```

```python
import functools
import math

import numpy as np
import jax
import jax.numpy as jnp
from jax import lax
from jax.experimental import pallas as pl
from jax.experimental.pallas import tpu as pltpu

F32 = jnp.float32
BF16 = jnp.bfloat16

D_MODEL = 1024
HEAD_DIM = 64
EPS = 1e-6
NEG = -1e30
NUM_BUCKETS = 32
MAX_DISTANCE = 128
NSA_HEADS = 8
NSA_KV_HEADS = 2
NSA_GROUP = 4
CMP_BLOCK = 32
CMP_STRIDE = 16
SEL_BLOCK = 64
SEL_TOPN = 16
WINDOW = 512
FORCE_BONUS = 1e3
DIFF_HEADS = 4
MOBA_HEADS = 16
MOBA_BLOCK = 256
MOBA_TOPK = 3
D_FF = 4 * D_MODEL
SCALE = HEAD_DIM ** -0.5

LANES = 128
TQ = 256
TQD = 512
NSA_TILES_PER_STEP = 2
DIFF_TILES_PER_STEP = 4
MOBA_TILES_PER_STEP = 4
TK = 256
FAR_REL = 113
CMP_PER_TQ = TQ // CMP_STRIDE
CMP_NEAR = 128
CMP_PAD = CMP_NEAR - 2 * CMP_PER_TQ
SUB = 2 * TK
BAND_PAD = 3 * TK
FRONT = 2 * TK
LOG2E = math.log2(math.e)
QSCALE = SCALE * LOG2E
VMEM_LIMIT = 56 * 1024 * 1024


def _attention_params():
    return pltpu.CompilerParams(dimension_semantics=("arbitrary", "arbitrary", "arbitrary"),
                                vmem_limit_bytes=VMEM_LIMIT)


def _dot_nt(a, b):
    return lax.dot_general(a, b, (((1,), (1,)), ((), ())), preferred_element_type=F32)


def _dot(a, b):
    return jnp.dot(a, b, preferred_element_type=F32)


def _rms(x, g):
    return x * lax.rsqrt(jnp.mean(x * x, axis=-1, keepdims=True) + EPS) * g


def _t5_bucket_np(rel):
    n = np.maximum(rel, 0)
    max_exact = NUM_BUCKETS // 2
    large = max_exact + (np.log(np.maximum(n, 1).astype(np.float32) / np.float32(max_exact))
                         / np.float32(math.log(MAX_DISTANCE / max_exact))
                         * np.float32(NUM_BUCKETS - max_exact)).astype(np.int32)
    large = np.minimum(large, NUM_BUCKETS - 1)
    return np.where(n < max_exact, n, large).astype(np.int32)


def _strip_kernel(tbl_ref, bucket_ref, valid_ref, o_ref, *, head0):
    h = pl.program_id(0) + head0
    bucket = bucket_ref[...]
    last = tbl_ref[NUM_BUCKETS - 1, h]
    acc = jnp.zeros(bucket.shape, F32)
    for b in range(NUM_BUCKETS - 1):
        acc = jnp.where(bucket == b, tbl_ref[b, h] - last, acc)
    o_ref[0] = jnp.where(valid_ref[...] != 0, acc * LOG2E, NEG)


def _toeplitz_kernel(tbl_ref, bucket_ref, valid_ref, o_ref, *, rows, cols):
    h = pl.program_id(0)
    bucket = bucket_ref[...]
    last = tbl_ref[NUM_BUCKETS - 1, h]
    acc = jnp.zeros(bucket.shape, F32)
    for b in range(NUM_BUCKETS - 1):
        acc = jnp.where(bucket == b, tbl_ref[b, h] - last, acc)
    line = jnp.where(valid_ref[...] != 0, acc * LOG2E, NEG)
    wide = jnp.broadcast_to(line[0:1], (rows, line.shape[1]))
    o_ref[0] = pltpu.roll(wide, 0, 1, stride=1, stride_axis=0)[:, :cols]


def _toeplitz_strips(table, rows, cols, off, lo, hi, nheads, name):
    width = -(-(rows + cols) // LANES) * LANES
    j = np.arange(width)
    rel = off - np.where(j < cols, j, j - width)
    valid = (rel >= lo) & (rel < hi)
    tile8 = lambda a: jnp.asarray(np.broadcast_to(a[None, :], (8, width)).astype(np.int32))
    return pl.pallas_call(
        functools.partial(_toeplitz_kernel, rows=rows, cols=cols),
        out_shape=jax.ShapeDtypeStruct((nheads, rows, cols), F32),
        grid=(nheads,),
        in_specs=[pl.BlockSpec(memory_space=pltpu.SMEM),
                  pl.BlockSpec((8, width), lambda h: (0, 0)),
                  pl.BlockSpec((8, width), lambda h: (0, 0))],
        out_specs=pl.BlockSpec((1, rows, cols), lambda h: (h, 0, 0)),
        name=name,
    )(table, tile8(_t5_bucket_np(rel)), tile8(valid))


def _bias_strips(table, rel, valid, head0, nheads, name):
    rows, cols = rel.shape
    bucket = jnp.asarray(_t5_bucket_np(rel))
    validi = jnp.asarray(valid.astype(np.int32))
    return pl.pallas_call(
        functools.partial(_strip_kernel, head0=head0),
        out_shape=jax.ShapeDtypeStruct((nheads, rows, cols), F32),
        grid=(nheads,),
        in_specs=[pl.BlockSpec(memory_space=pltpu.SMEM),
                  pl.BlockSpec((rows, cols), lambda h: (0, 0)),
                  pl.BlockSpec((rows, cols), lambda h: (0, 0))],
        out_specs=pl.BlockSpec((1, rows, cols), lambda h: (h, 0, 0)),
        name=name,
    )(table, bucket, validi)


def _norm_matmul_kernel(x_ref, g_ref, w_ref, o_ref, xn_ref):
    @pl.when(pl.program_id(1) == 0)
    def _():
        xn_ref[...] = _rms(x_ref[...], g_ref[...]).astype(BF16)

    o_ref[...] = _dot(xn_ref[...], w_ref[...]).astype(o_ref.dtype)


def _norm_matmul(x, g, w, out_dtype, tm, tn, name):
    m, d = x.shape
    n = w.shape[1]
    return pl.pallas_call(
        _norm_matmul_kernel,
        out_shape=jax.ShapeDtypeStruct((m, n), out_dtype),
        grid=(m // tm, n // tn),
        in_specs=[pl.BlockSpec((tm, d), lambda i, j: (i, 0)),
                  pl.BlockSpec((1, d), lambda i, j: (0, 0)),
                  pl.BlockSpec((d, tn), lambda i, j: (0, j))],
        out_specs=pl.BlockSpec((tm, tn), lambda i, j: (i, j)),
        scratch_shapes=[pltpu.VMEM((tm, d), BF16)],
        compiler_params=pltpu.CompilerParams(
            dimension_semantics=("arbitrary", "arbitrary"), vmem_limit_bytes=VMEM_LIMIT),
        name=name,
    )(x, g.reshape(1, d), w)


def _in_proj_even_kernel(x_ref, g_ref, w_ref, wa_ref, o_ref, kc0_ref, kc1_ref, vc0_ref, vc1_ref, gate_ref,
                         xn_ref):
    hd = HEAD_DIM

    @pl.when(pl.program_id(1) == 0)
    def _():
        xn = _rms(x_ref[...], g_ref[...]).astype(BF16)
        xn_ref[...] = xn
        aux = _dot(xn, wa_ref[...])
        kc0_ref[...] = aux[:, 0:hd]
        kc1_ref[...] = aux[:, hd:2 * hd]
        vc0_ref[...] = aux[:, 2 * hd:3 * hd]
        vc1_ref[...] = aux[:, 3 * hd:4 * hd]
        gate_ref[...] = aux[:, 4 * hd:]

    o_ref[...] = _dot(xn_ref[...], w_ref[...]).astype(o_ref.dtype)


def _in_proj_even(x, g, w_main, w_aux, tm, tn):
    m, d = x.shape
    n = w_main.shape[1]
    na = w_aux.shape[1]
    row_spec = lambda width: pl.BlockSpec((tm, width), lambda i, j: (i, 0))
    cmp_shape = jax.ShapeDtypeStruct((m, HEAD_DIM), F32)
    return pl.pallas_call(
        _in_proj_even_kernel,
        out_shape=(jax.ShapeDtypeStruct((m, n), BF16), cmp_shape, cmp_shape, cmp_shape, cmp_shape,
                   jax.ShapeDtypeStruct((m, na - 4 * HEAD_DIM), F32)),
        grid=(m // tm, n // tn),
        in_specs=[row_spec(d),
                  pl.BlockSpec((1, d), lambda i, j: (0, 0)),
                  pl.BlockSpec((d, tn), lambda i, j: (0, j)),
                  pl.BlockSpec((d, na), lambda i, j: (0, 0))],
        out_specs=(pl.BlockSpec((tm, tn), lambda i, j: (i, j)), row_spec(HEAD_DIM), row_spec(HEAD_DIM),
                   row_spec(HEAD_DIM), row_spec(HEAD_DIM), row_spec(na - 4 * HEAD_DIM)),
        scratch_shapes=[pltpu.VMEM((tm, d), BF16)],
        compiler_params=pltpu.CompilerParams(
            dimension_semantics=("arbitrary", "arbitrary"), vmem_limit_bytes=VMEM_LIMIT),
        name="in_proj_even",
    )(x, g.reshape(1, d), w_main, w_aux)


def _post_kernel(x_ref, a0_ref, a1_ref, wo0_ref, wo1_ref, g_ref, w1_ref, w2_ref, gf_ref,
                 o_ref, acc_ref, xn_ref, *, final_norm):
    f = pl.program_id(1)

    @pl.when(f == 0)
    def _():
        x1 = x_ref[...] + _dot(a0_ref[...], wo0_ref[...]) + _dot(a1_ref[...], wo1_ref[...])
        acc_ref[...] = x1
        xn_ref[...] = _rms(x1, g_ref[...]).astype(BF16)

    h = jnp.square(jnp.maximum(_dot(xn_ref[...], w1_ref[...]), 0.0))
    acc_ref[...] += _dot(h.astype(BF16), w2_ref[...])

    @pl.when(f == pl.num_programs(1) - 1)
    def _():
        y = acc_ref[...]
        if final_norm:
            y = _rms(y, gf_ref[...])
        o_ref[...] = y


def _post(x, a0, a1, a1_colblk, wo0, wo1, g, w1, w2, gf, final_norm, tm, tf, name):
    m, d = x.shape
    k0 = wo0.shape[0]
    k1 = wo1.shape[0]
    ff = w1.shape[1]
    return pl.pallas_call(
        functools.partial(_post_kernel, final_norm=final_norm),
        out_shape=jax.ShapeDtypeStruct((m, d), F32),
        grid=(m // tm, ff // tf),
        in_specs=[pl.BlockSpec((tm, d), lambda i, f: (i, 0)),
                  pl.BlockSpec((tm, k0), lambda i, f: (i, 0)),
                  pl.BlockSpec((tm, k1), lambda i, f: (i, a1_colblk)),
                  pl.BlockSpec((k0, d), lambda i, f: (0, 0)),
                  pl.BlockSpec((k1, d), lambda i, f: (0, 0)),
                  pl.BlockSpec((1, d), lambda i, f: (0, 0)),
                  pl.BlockSpec((d, tf), lambda i, f: (0, f)),
                  pl.BlockSpec((tf, d), lambda i, f: (f, 0)),
                  pl.BlockSpec((1, d), lambda i, f: (0, 0))],
        out_specs=pl.BlockSpec((tm, d), lambda i, f: (i, 0)),
        scratch_shapes=[pltpu.VMEM((tm, d), F32), pltpu.VMEM((tm, d), BF16)],
        compiler_params=pltpu.CompilerParams(
            dimension_semantics=("arbitrary", "arbitrary"), vmem_limit_bytes=VMEM_LIMIT),
        name=name,
    )(x, a0, a1, wo0, wo1, g.reshape(1, d), w1, w2, gf.reshape(1, d))


def _compress_kernel(kc0_ref, kc1_ref, vc0_ref, vc1_ref, pos_ref, w1_ref, w2_ref, o_ref):
    half = CMP_STRIDE * HEAD_DIM
    for kv, refs in enumerate(((kc0_ref, kc1_ref), (vc0_ref, vc1_ref))):
        w1a = w1_ref[kv, :half, :]
        w1b = w1_ref[kv, half:, :]
        pos = pos_ref[kv].astype(BF16)
        pa = jnp.broadcast_to(pos[0:1], (8, half))
        pb = jnp.broadcast_to(pos[1:2], (8, half))
        posterm = (_dot(pa, w1a) + _dot(pb, w1b))[0:1]
        for grp, x_ref in enumerate(refs):
            x = x_ref[0].astype(BF16)
            first = _dot(x, w1a)
            second = _dot(x, w1b)
            second = pltpu.roll(second, second.shape[0] - 1, 0)
            hid = jax.nn.gelu(first + second + posterm)
            o_ref[kv, grp] = _dot(hid.astype(BF16), w2_ref[kv])


def _compress(kc0, kc1, vc0, vc1, pos, w1, w2):
    batch, nch, wide = kc0.shape
    x_spec = pl.BlockSpec((1, nch, wide), lambda b: (b, 0, 0))
    full = lambda a: pl.BlockSpec(a.shape, lambda b: (0,) * a.ndim)
    return pl.pallas_call(
        _compress_kernel,
        out_shape=jax.ShapeDtypeStruct((2, batch * NSA_KV_HEADS, nch, HEAD_DIM), F32),
        grid=(batch,),
        in_specs=[x_spec, x_spec, x_spec, x_spec, full(pos), full(w1), full(w2)],
        out_specs=pl.BlockSpec((2, NSA_KV_HEADS, nch, HEAD_DIM), lambda b: (0, b, 0, 0)),
        compiler_params=pltpu.CompilerParams(
            dimension_semantics=("arbitrary",), vmem_limit_bytes=VMEM_LIMIT),
        name="nsa_compress",
    )(kc0, kc1, vc0, vc1, pos, w1, w2)


def _softmax_part(m, s):
    m_new = jnp.maximum(m, jnp.max(s, axis=1, keepdims=True))
    return m_new, jnp.exp2(m - m_new), jnp.exp2(s - m_new).astype(BF16)


def _pv(p, v_sc, r0, width):
    if not isinstance(v_sc, (tuple, list)):
        return _dot(p, v_sc[pl.ds(r0, width), :])
    share = p.shape[0] // len(v_sc)
    return jnp.concatenate([_dot(p[g * share:(g + 1) * share], v[pl.ds(r0, width), :])
                            for g, v in enumerate(v_sc)], axis=0)


def _flash_chain(q, k_sc, v_sc, bufs, far0, n_far, tail, side_work=None):
    s_sc, p_sc, acc_sc = bufs
    rows = q.shape[0]
    assert SUB == 2 * TK and tail[0][1] == SUB
    tail0 = tail[0][0]

    def qk(r0):
        return _dot_nt(q, k_sc[pl.ds(pl.multiple_of(r0, TK), SUB), :])

    n_steps = n_far // 2
    s_sc[...] = qk(jnp.where(n_steps > 0, far0, tail0))
    p_sc[...] = jnp.zeros(p_sc.shape, BF16)
    acc_sc[...] = jnp.zeros(acc_sc.shape, F32)

    def trips(start, n_sub, n_trips, after):
        def body(j, m):
            base = pl.multiple_of(start + j * (n_sub * SUB), SUB)
            acc = acc_sc[...] + _pv(p_sc[...], v_sc, base - SUB, SUB)
            s_cur = s_sc[...]
            for u in range(n_sub):
                m, alpha, p = _softmax_part(m, s_cur)
                if u + 1 < n_sub:
                    s_cur = qk(base + (u + 1) * SUB)
                    acc = alpha * acc + _pv(p, v_sc, base + u * SUB, SUB)
                else:
                    s_sc[...] = qk(jnp.where(j + 1 < n_trips, base + n_sub * SUB, after))
                    p_sc[...] = p
                    acc_sc[...] = alpha * acc
            return m
        return body

    m = jnp.full((rows, 1), NEG, F32)
    start, left = far0, n_steps
    for n_sub in (4, 2, 1):
        n_trips = left // n_sub
        left = left - n_trips * n_sub
        nxt = start + n_trips * (n_sub * SUB)
        after = jnp.where(left > 0, nxt, tail0)
        m = lax.fori_loop(0, n_trips, trips(start, n_sub, n_trips, after), m)
        start = nxt
    if side_work is not None:
        side_work()
    base = pl.multiple_of(far0 + n_steps * SUB, SUB)
    acc = acc_sc[...] + _pv(p_sc[...], v_sc, base - SUB, SUB)
    m, alpha, p = _softmax_part(m, s_sc[...] + tail[0][2]())
    acc = alpha * acc + _pv(p, v_sc, tail0, SUB)
    return _flash_tail((m, acc), q, k_sc, v_sc, tail[1:])


def _flash_buffers(rows, ncols):
    return [pltpu.VMEM((rows, SUB), F32), pltpu.VMEM((rows, SUB), BF16), pltpu.VMEM((rows, ncols), F32)]


def _flash_tail(carry, q, k_sc, v_sc, tail):
    m, acc = carry
    for r0, width, extra in tail:
        m, alpha, p = _softmax_part(m, _dot_nt(q, k_sc[pl.ds(r0, width), :]) + extra())
        acc = alpha * acc + _pv(p, v_sc, r0, width)
    return acc


def _tile_gate(cond):
    return jnp.where(cond, 0.0, NEG)


def _tail_steps(win, odd_ok, rows, strip, near_ok):
    def first():
        return jnp.concatenate([jnp.broadcast_to(_tile_gate(odd_ok), (rows, TK)),
                                strip(0, TK) + _tile_gate(near_ok)], axis=1)

    return [(win, SUB, first), (win + SUB, SUB, lambda: strip(TK, 3 * TK))]


def _topk_rows(score, idx, k, floor):
    sel = jnp.zeros(score.shape, F32)
    nrow = score.shape[0]
    for _ in range(k):
        mx = jnp.max(score, axis=0, keepdims=True)
        first = jnp.min(jnp.where(score == mx, idx, nrow), axis=0, keepdims=True)
        pick = idx == first
        sel = jnp.where(pick, 1.0, sel)
        score = jnp.where(pick, floor, score)
    return sel


def _nsa_kernel(q_ref, qn_ref, ks_ref, vs_ref, kw_ref, vw_ref, kc_ref, vc_ref, gate_ref,
                ssel_ref, swin_ref, scmp_ref, ov_ref, o_ref,
                kaug, vaug, kwp, vwp, kcp, vcp, qaug_sc, ocw_sc, *bufs, seq):
    g = pl.program_id(1)
    step = pl.program_id(2)
    ncp = seq // CMP_STRIDE
    hd = HEAD_DIM
    rows = NSA_GROUP * TQ

    @pl.when(step == 0)
    def _():
        lane = lax.broadcasted_iota(jnp.int32, (BAND_PAD, LANES), 1)
        padflag = jnp.where(lane == hd, 1.0, 0.0).astype(BF16)
        kaug[0:BAND_PAD, 0:LANES] = jnp.zeros((BAND_PAD, LANES), BF16)
        kaug[0:BAND_PAD, LANES:2 * LANES] = padflag
        kwp[0:BAND_PAD, :] = padflag
        vaug[0:BAND_PAD, :] = jnp.zeros((BAND_PAD, LANES), BF16)
        vwp[0:BAND_PAD, :] = jnp.zeros((BAND_PAD, LANES), BF16)
        onescol = jnp.where(lax.broadcasted_iota(jnp.int32, (TK, LANES), 1) == hd, 1.0, 0.0)
        zero_hd = jnp.zeros((TK, hd), F32)
        col = lax.broadcasted_iota(jnp.int32, (TK, LANES), 1)
        row = lax.broadcasted_iota(jnp.int32, (TK, LANES), 0)

        def fill(c, _):
            src = pl.multiple_of(c * TK, TK)
            dst = pl.multiple_of(BAND_PAD + c * TK, TK)

            def pick(ref):
                both = ref[pl.ds(src, TK), :].astype(F32)
                return jnp.concatenate([jnp.where(g == 0, both[:, :hd], both[:, hd:]), zero_hd], axis=1)

            kaug[pl.ds(dst, TK), 0:LANES] = jnp.where((src + row) // SEL_BLOCK == col, 1.0, 0.0).astype(BF16)
            kaug[pl.ds(dst, TK), LANES:2 * LANES] = pick(ks_ref).astype(BF16)
            kwp[pl.ds(dst, TK), :] = pick(kw_ref).astype(BF16)
            vaug[pl.ds(dst, TK), :] = (pick(vs_ref) + onescol).astype(BF16)
            vwp[pl.ds(dst, TK), :] = (pick(vw_ref) + onescol).astype(BF16)
            return 0

        lax.fori_loop(0, seq // TK, fill, 0)
        lane_c = lax.broadcasted_iota(jnp.int32, (CMP_PAD, LANES), 1)
        kcp[0:CMP_PAD, :] = jnp.where(lane_c == hd, 1.0, 0.0).astype(BF16)
        vcp[0:CMP_PAD, :] = jnp.zeros((CMP_PAD, LANES), BF16)
        zc = jnp.zeros((ncp, hd), F32)
        onesc = jnp.where(lax.broadcasted_iota(jnp.int32, (ncp, LANES), 1) == hd, 1.0, 0.0)
        kcp[CMP_PAD:CMP_PAD + ncp, :] = jnp.concatenate([kc_ref[0, 0], zc], axis=1).astype(BF16)
        vcp[CMP_PAD:CMP_PAD + ncp, :] = (jnp.concatenate([vc_ref[0, 0], zc], axis=1) + onesc).astype(BF16)
        tail = kcp.shape[0] - CMP_PAD - ncp
        kcp[CMP_PAD + ncp:, :] = jnp.zeros((tail, LANES), BF16)
        vcp[CMP_PAD + ncp:, :] = jnp.zeros((tail, LANES), BF16)

    def stack(ref, lo, hi):
        return jnp.concatenate([ref[r, :, lo:hi] for r in range(NSA_GROUP)], axis=0)

    def prepare(qsrc_ref, t, slot):
        qblk = qsrc_ref[...].astype(F32) * QSCALE
        lane_t = lax.broadcasted_iota(jnp.int32, (TQ, hd), 1)
        qtail = jnp.where(lane_t == 0, NEG, 0.0)
        qs = [qblk[:, r * hd:(r + 1) * hd] for r in range(NSA_GROUP)]
        q128 = jnp.concatenate([jnp.concatenate([qs[r], qtail], axis=1) for r in range(NSA_GROUP)],
                               axis=0).astype(BF16)

        s_far = _dot_nt(q128, kcp[CMP_PAD:CMP_PAD + ncp, :])
        c_idx = lax.broadcasted_iota(jnp.int32, (1, ncp), 1)
        far_ok = c_idx < CMP_PER_TQ * t - CMP_PAD
        near0 = pl.multiple_of(CMP_PER_TQ * t, CMP_PER_TQ)
        s_near = _dot_nt(q128, kcp[pl.ds(near0, CMP_NEAR), :]) + stack(scmp_ref, 0, CMP_NEAR)
        near_ok = s_near > 0.5 * NEG
        s_far = jnp.where(far_ok, s_far, NEG)
        m_c = jnp.maximum(jnp.max(s_far, axis=1, keepdims=True), jnp.max(s_near, axis=1, keepdims=True))
        p_far = jnp.where(far_ok, jnp.exp2(s_far - m_c), 0.0)
        p_near = jnp.where(near_ok, jnp.exp2(s_near - m_c), 0.0)
        den = jnp.maximum(jnp.sum(p_far, axis=1, keepdims=True) + jnp.sum(p_near, axis=1, keepdims=True), 1e-30)
        pc_far = p_far / den
        pc_near = p_near / den
        ocw_sc[slot, 0] = (_dot(pc_far.astype(BF16), vcp[CMP_PAD:CMP_PAD + ncp, :])
                           + _dot(pc_near.astype(BF16), vcp[pl.ds(near0, CMP_NEAR), :]))

        band0 = pl.multiple_of(TQ * t, TK) + BAND_PAD - 2 * TK
        steps = [(band0, 2 * TK, lambda: stack(swin_ref, 0, 2 * TK)),
                 (band0 + 2 * TK, TK, lambda: stack(swin_ref, 2 * TK, 3 * TK))]
        acc = _flash_tail((jnp.full((rows, 1), NEG, F32), jnp.zeros((rows, LANES), F32)), q128, kwp, vwp, steps)
        ocw_sc[slot, 1] = acc / acc[:, hd:hd + 1]

        pcs_far = sum(pc_far[r * TQ:(r + 1) * TQ] for r in range(NSA_GROUP))
        pcs_near = sum(pc_near[r * TQ:(r + 1) * TQ] for r in range(NSA_GROUP))

        def hi_lo_dot(p, w):
            hi = p.astype(BF16)
            lo = (p - hi.astype(F32)).astype(BF16)
            return _dot(hi, w) + _dot(lo, w)

        imp = (hi_lo_dot(pcs_far, ov_ref[CMP_PAD:CMP_PAD + ncp, :])
               + hi_lo_dot(pcs_near, ov_ref[pl.ds(near0, CMP_NEAR), :]))
        imp_t = imp.T
        n_idx = lax.broadcasted_iota(jnp.int32, (LANES, TQ), 0)
        cur = (TQ * t + lax.broadcasted_iota(jnp.int32, (LANES, TQ), 1)) // SEL_BLOCK
        eligible = n_idx <= cur
        forced = ((n_idx == 0) | (n_idx == cur) | (n_idx == cur - 1)) & eligible
        rest = jnp.where(forced, -3.0e38, jnp.where(eligible, imp_t, -1.0))
        sel_t = _topk_rows(rest, n_idx, SEL_TOPN - 3, -3.0e38)
        maskneg = jnp.where(forced | (sel_t > 0.5), 0.0, NEG).T
        qaug_sc[slot] = jnp.concatenate(
            [jnp.concatenate([maskneg, qs[r], qtail], axis=1) for r in range(NSA_GROUP)],
            axis=0).astype(BF16)

    @pl.when(step == 0)
    def _():
        prepare(q_ref.at[0:TQ], 0, 0)

    n_tiles = NSA_TILES_PER_STEP * pl.num_programs(2)
    for part in range(NSA_TILES_PER_STEP):
        i = step * NSA_TILES_PER_STEP + part
        slot = part
        tok = slice(part * TQ, (part + 1) * TQ)
        n_far = jnp.maximum(i - 2, 0)
        win = pl.multiple_of(TQ * i, TK)
        tail = _tail_steps(win, n_far % 2 == 1, rows, functools.partial(stack, ssel_ref), True)
        if part + 1 < NSA_TILES_PER_STEP:
            prepare_next = functools.partial(prepare, q_ref.at[(part + 1) * TQ:(part + 2) * TQ], i + 1, slot + 1)
        else:
            prepare_next = functools.partial(prepare, qn_ref, jnp.minimum(i + 1, n_tiles - 1), 0)
        acc = _flash_chain(qaug_sc[slot], kaug, vaug, bufs[3 * part:3 * part + 3], BAND_PAD, n_far, tail,
                           prepare_next)
        o_sel = acc[:, :hd] / acc[:, hd:hd + 1]

        o_cmp = ocw_sc[slot, 0]
        o_win = ocw_sc[slot, 1]
        sig = jax.nn.sigmoid(gate_ref[tok, :])
        outs = []
        for r in range(NSA_GROUP):
            def gcol(branch):
                c0 = branch * NSA_HEADS + r
                c1 = c0 + NSA_GROUP
                return jnp.where(g == 0, sig[:, c0:c0 + 1], sig[:, c1:c1 + 1])
            sl = slice(r * TQ, (r + 1) * TQ)
            outs.append(gcol(0) * o_cmp[sl, :hd] + gcol(1) * o_sel[sl] + gcol(2) * o_win[sl, :hd])
        o_ref[tok, :] = jnp.concatenate(outs, axis=1).astype(o_ref.dtype)


def _nsa(main, aux, ckv, ssel, swin, scmp, ov, batch, seq):
    T = NSA_TILES_PER_STEP
    per_step = T * TQ
    nq = seq // TQ
    ns = seq // per_step
    ncp = seq // CMP_STRIDE
    gw = NSA_GROUP * HEAD_DIM
    once = pl.Buffered(1)
    kv_spec = lambda colblk: pl.BlockSpec((seq, LANES), lambda b, g, i: (b, colblk), pipeline_mode=once)
    strip_spec = lambda cols: pl.BlockSpec((NSA_GROUP, TQ, cols), lambda b, g, i: (g, 0, 0), pipeline_mode=once)
    return pl.pallas_call(
        functools.partial(_nsa_kernel, seq=seq),
        out_shape=jax.ShapeDtypeStruct((batch * seq, NSA_HEADS * HEAD_DIM), BF16),
        grid=(batch, NSA_KV_HEADS, ns),
        in_specs=[pl.BlockSpec((per_step, gw), lambda b, g, i: (b * ns + i, g)),
                  pl.BlockSpec((TQ, gw), lambda b, g, i: (b * nq + jnp.minimum(T * (i + 1), nq - 1), g)),
                  kv_spec(4), kv_spec(5), kv_spec(6), kv_spec(7),
                  pl.BlockSpec((1, 1, ncp, HEAD_DIM), lambda b, g, i: (0, b * NSA_KV_HEADS + g, 0, 0)),
                  pl.BlockSpec((1, 1, ncp, HEAD_DIM), lambda b, g, i: (1, b * NSA_KV_HEADS + g, 0, 0)),
                  pl.BlockSpec((per_step, LANES), lambda b, g, i: (b * ns + i, 0)),
                  strip_spec(3 * TK), strip_spec(3 * TK), strip_spec(CMP_NEAR),
                  pl.BlockSpec(ov.shape, lambda b, g, i: (0, 0))],
        out_specs=pl.BlockSpec((per_step, gw), lambda b, g, i: (b * ns + i, g)),
        scratch_shapes=[pltpu.VMEM((BAND_PAD + seq, 2 * LANES), BF16),
                        pltpu.VMEM((BAND_PAD + seq, LANES), BF16),
                        pltpu.VMEM((BAND_PAD + seq, LANES), BF16),
                        pltpu.VMEM((BAND_PAD + seq, LANES), BF16),
                        pltpu.VMEM((ov.shape[0], LANES), BF16),
                        pltpu.VMEM((ov.shape[0], LANES), BF16),
                        pltpu.VMEM((T, NSA_GROUP * TQ, 2 * LANES), BF16),
                        pltpu.VMEM((T, 2, NSA_GROUP * TQ, LANES), F32),
                        *(T * _flash_buffers(NSA_GROUP * TQ, LANES))],
        compiler_params=_attention_params(),
        name="nsa_attention",
    )(main, main, main, main, main, main, ckv, ckv, aux, ssel, swin, scmp, ov)


def _ones_column():
    return jnp.where(lax.broadcasted_iota(jnp.int32, (TK, LANES), 1) == 0, 1.0, 0.0).astype(BF16)


def _causal_flash(q, k_sc, v_sc, bufs, strip, i):
    n_far = jnp.maximum(2 * i - 1, 0)
    win = pl.multiple_of(i * TQD, SUB)
    tail = _tail_steps(win, i >= 1, q.shape[0], strip, i >= 1)
    return _flash_chain(q, k_sc, v_sc, bufs, FRONT, n_far, tail)


def _diff_kernel(q_ref, k_ref, v_ref, lam_ref, sub_ref, strip_ref, o_ref, k_sc, v_sc, *bufs, lam_init, seq):
    step = pl.program_id(2)
    hd = HEAD_DIM

    @pl.when(step == 0)
    def _():
        k_sc[0:FRONT, :] = jnp.zeros((FRONT, k_sc.shape[1]), BF16)
        v_sc[0:FRONT, :] = jnp.zeros((FRONT, v_sc.shape[1]), BF16)
        ones = _ones_column()

        def fill(c, _):
            r0 = pl.multiple_of(c * TK, TK)
            k_sc[pl.ds(FRONT + r0, TK), :] = k_ref[pl.ds(r0, TK), :]
            v_sc[pl.ds(FRONT + r0, TK), 0:LANES] = v_ref[pl.ds(r0, TK), :]
            v_sc[pl.ds(FRONT + r0, TK), LANES:2 * LANES] = ones
            return 0

        lax.fori_loop(0, seq // TK, fill, 0)

    lane = lax.broadcasted_iota(jnp.int32, (TQD, 2 * hd), 1)
    strip = lambda lo, hi: jnp.concatenate([strip_ref[0, :, lo:hi]] * 2, axis=0)
    lam_p = lam_ref[...]
    lam = (jnp.exp(jnp.sum(lam_p[0:1] * lam_p[1:2], axis=1, keepdims=True))
           - jnp.exp(jnp.sum(lam_p[2:3] * lam_p[3:4], axis=1, keepdims=True)) + lam_init)

    for part in range(DIFF_TILES_PER_STEP):
        tile_rows = slice(part * TQD, (part + 1) * TQD)
        q = q_ref[tile_rows, :].astype(F32) * QSCALE
        qst = jnp.concatenate([jnp.where(lane < hd, q, 0.0), jnp.where(lane < hd, 0.0, q)],
                              axis=0).astype(BF16)
        acc = _causal_flash(qst, k_sc, v_sc, bufs[3 * part:3 * part + 3], strip, step * DIFF_TILES_PER_STEP + part)
        o = acc[:, :2 * hd] / acc[:, 2 * hd:2 * hd + 1]
        d = o[:TQD] - lam * o[TQD:]
        o_ref[tile_rows, :] = (_rms(d, sub_ref[...]) * (1.0 - lam_init)).astype(o_ref.dtype)


def _diff(main, lam_p, subln, strips, batch, seq, lam_init):
    per_step = DIFF_TILES_PER_STEP * TQD
    nq = seq // per_step
    blk = 2 * HEAD_DIM
    q0, k0, v0 = 1024 // blk, 1536 // blk, 2048 // blk
    return pl.pallas_call(
        functools.partial(_diff_kernel, lam_init=lam_init, seq=seq),
        out_shape=jax.ShapeDtypeStruct((batch * seq, DIFF_HEADS * blk), BF16),
        grid=(batch, DIFF_HEADS, nq),
        in_specs=[pl.BlockSpec((per_step, blk), lambda b, h, i: (b * nq + i, q0 + h)),
                  pl.BlockSpec((seq, blk), lambda b, h, i: (b, k0 + h)),
                  pl.BlockSpec((seq, blk), lambda b, h, i: (b, v0 + h)),
                  pl.BlockSpec((4, HEAD_DIM), lambda b, h, i: (0, 0)),
                  pl.BlockSpec((1, blk), lambda b, h, i: (0, 0)),
                  pl.BlockSpec((1, TQD, 3 * TK), lambda b, h, i: (NSA_HEADS + h, 0, 0))],
        out_specs=pl.BlockSpec((per_step, blk), lambda b, h, i: (b * nq + i, h)),
        scratch_shapes=[pltpu.VMEM((FRONT + seq, LANES), BF16), pltpu.VMEM((FRONT + seq, 2 * LANES), BF16),
                        *(DIFF_TILES_PER_STEP * _flash_buffers(2 * TQD, 2 * LANES))],
        compiler_params=_attention_params(),
        name="diff_attention",
    )(main, main, main, lam_p, subln.reshape(1, blk), strips)


def _moba_kernel(q_ref, k_ref, v_ref, strip_ref, o_ref, kaug, va_sc, vb_sc, kmean, *bufs, seq):
    step = pl.program_id(2)
    hd = HEAD_DIM
    nb = seq // MOBA_BLOCK

    @pl.when(step == 0)
    def _():
        kmean[...] = jnp.zeros(kmean.shape, F32)
        kaug[0:FRONT, :] = jnp.zeros((FRONT, kaug.shape[1]), BF16)
        va_sc[0:FRONT, :] = jnp.zeros((FRONT, LANES), BF16)
        vb_sc[0:FRONT, :] = jnp.zeros((FRONT, LANES), BF16)
        col = lax.broadcasted_iota(jnp.int32, (MOBA_BLOCK, LANES), 1)
        ones_tail = jnp.where(lax.broadcasted_iota(jnp.int32, (MOBA_BLOCK, hd), 1) == 0, 1.0, 0.0).astype(BF16)

        def fill(n, _):
            r0 = pl.multiple_of(n * MOBA_BLOCK, MOBA_BLOCK)
            kb = k_ref[pl.ds(r0, MOBA_BLOCK), :]
            vb = v_ref[pl.ds(r0, MOBA_BLOCK), :]
            kaug[pl.ds(FRONT + r0, MOBA_BLOCK), 0:LANES] = jnp.where(col == n, 1.0, 0.0).astype(BF16)
            kaug[pl.ds(FRONT + r0, MOBA_BLOCK), LANES:2 * LANES] = kb
            va_sc[pl.ds(FRONT + r0, MOBA_BLOCK), :] = jnp.concatenate([vb[:, :hd], ones_tail], axis=1)
            vb_sc[pl.ds(FRONT + r0, MOBA_BLOCK), :] = jnp.concatenate([vb[:, hd:], ones_tail], axis=1)
            kmean[pl.ds(n, 1), :] = jnp.mean(kb.astype(F32), axis=0, keepdims=True)
            return 0

        lax.fori_loop(0, nb, fill, 0)

    lane = lax.broadcasted_iota(jnp.int32, (TQD, LANES), 1)
    n_idx = lax.broadcasted_iota(jnp.int32, (nb, TQD), 0)
    km = kmean[0:nb, :]
    lane_k = lax.broadcasted_iota(jnp.int32, km.shape, 1)
    no_block = jnp.full((LANES - nb, TQD), NEG, F32)
    strip = lambda lo, hi: jnp.concatenate([strip_ref[0, :, lo:hi], strip_ref[1, :, lo:hi]], axis=0)

    for part in range(MOBA_TILES_PER_STEP):
        i = step * MOBA_TILES_PER_STEP + part
        tile_rows = slice(part * TQD, (part + 1) * TQD)
        qblk = q_ref[tile_rows, :]
        qf = qblk.astype(F32) * QSCALE
        own = (i * TQD + lax.broadcasted_iota(jnp.int32, (nb, TQD), 1)) // MOBA_BLOCK
        past = n_idx < own
        qrows = []
        for hh in range(2):
            mine = (lane < hd) if hh == 0 else (lane >= hd)
            mine_k = (lane_k < hd) if hh == 0 else (lane_k >= hd)
            gate_t = _dot_nt(jnp.where(mine_k, km, 0.0).astype(BF16), qblk)
            sel_t = _topk_rows(jnp.where(past, gate_t, NEG), n_idx, MOBA_TOPK, -3.0e38)
            allowed = ((sel_t > 0.5) & past) | (n_idx == own)
            maskneg = jnp.concatenate([jnp.where(allowed, 0.0, NEG), no_block], axis=0).T
            qrows.append(jnp.concatenate([maskneg, jnp.where(mine, qf, 0.0)], axis=1))
        qaug = jnp.concatenate(qrows, axis=0).astype(BF16)
        acc = _causal_flash(qaug, kaug, (va_sc, vb_sc), bufs[3 * part:3 * part + 3], strip, i)
        o = acc[:, :hd] / acc[:, hd:hd + 1]
        o_ref[tile_rows, :] = jnp.concatenate([o[:TQD], o[TQD:]], axis=1).astype(o_ref.dtype)


def _moba(main, strips, batch, seq):
    per_step = MOBA_TILES_PER_STEP * TQD
    nq = seq // per_step
    pairs = MOBA_HEADS // 2
    return pl.pallas_call(
        functools.partial(_moba_kernel, seq=seq),
        out_shape=jax.ShapeDtypeStruct((batch * seq, MOBA_HEADS * HEAD_DIM), BF16),
        grid=(batch, pairs, nq),
        in_specs=[pl.BlockSpec((per_step, LANES), lambda b, h, i: (b * nq + i, h)),
                  pl.BlockSpec((seq, LANES), lambda b, h, i: (b, pairs + h)),
                  pl.BlockSpec((seq, LANES), lambda b, h, i: (b, 2 * pairs + h)),
                  pl.BlockSpec((2, TQD, 3 * TK), lambda b, h, i: (h, 0, 0))],
        out_specs=pl.BlockSpec((per_step, LANES), lambda b, h, i: (b * nq + i, h)),
        scratch_shapes=[pltpu.VMEM((FRONT + seq, 2 * LANES), BF16), pltpu.VMEM((FRONT + seq, LANES), BF16),
                        pltpu.VMEM((FRONT + seq, LANES), BF16),
                        pltpu.VMEM((LANES, LANES), F32),
                        *(MOBA_TILES_PER_STEP * _flash_buffers(2 * TQD, LANES))],
        compiler_params=_attention_params(),
        name="moba_attention",
    )(main, main, main, strips)


def _cmp_strip_geometry():
    r = np.arange(TQ)[:, None]
    cc = np.arange(CMP_NEAR)[None, :]
    rel_c = r + CMP_STRIDE * CMP_PAD - (CMP_BLOCK - 1) - CMP_STRIDE * cc
    return rel_c, rel_c >= 0


def _overlap_padded(seq):
    nc = (seq - CMP_BLOCK) // CMP_STRIDE + 1
    ncp = seq // CMP_STRIDE
    tok = np.arange(nc)[:, None] * CMP_STRIDE + np.arange(CMP_BLOCK)[None, :]
    ov = np.zeros((CMP_PAD + ncp + 2 * CMP_PER_TQ, LANES), np.float32)
    np.add.at(ov, (CMP_PAD + np.repeat(np.arange(nc), CMP_BLOCK), (tok // SEL_BLOCK).reshape(-1)),
              1.0 / CMP_BLOCK)
    return jnp.asarray(ov, dtype=BF16)


def _strips(bias_table):
    rel_c, valid_c = _cmp_strip_geometry()
    far = 1 << 30
    return (_toeplitz_strips(bias_table, TQD, 3 * TK, TK, 0, far, MOBA_HEADS, "bias_strip_diag"),
            _toeplitz_strips(bias_table, TQ, 3 * TK, 2 * TK, 0, far, NSA_HEADS, "bias_strip_sel"),
            _toeplitz_strips(bias_table, TQ, 3 * TK, 2 * TK, 0, WINDOW, NSA_HEADS, "bias_strip_win"),
            _bias_strips(bias_table, rel_c, valid_c, 0, NSA_HEADS, "bias_strip_cmp"))


def _even_heads(xf, batch, seq, bias_table, g_mix, w_in, pos_k, w1_k, w2_k, pos_v, w1_v, w2_v,
                lam_q1, lam_k1, lam_q2, lam_k2, subln, strips=None):
    d = xf.shape[1]
    strip_a, strip_sel, strip_win, strip_cmp = strips if strips is not None else _strips(bias_table)
    w_main = jnp.concatenate([w_in[:, 0:512], w_in[:, 768:1280], w_in[:, 1304:2840]], axis=1).astype(BF16)
    w_aux = jnp.concatenate([w_in[:, 512:768], w_in[:, 1280:1304],
                             jnp.zeros((d, 3 * LANES - 280), F32)], axis=1).astype(BF16)
    main0, kc0, kc1, vc0, vc1, gates = _in_proj_even(xf, g_mix, w_main, w_aux, 1024, w_main.shape[1] // 2)

    nch = seq // CMP_STRIDE
    chunks = lambda a: a.reshape(batch, nch, CMP_STRIDE * HEAD_DIM)
    pos = jnp.stack([pos_k, pos_v]).reshape(2, 2, CMP_STRIDE * HEAD_DIM)
    ckv = _compress(chunks(kc0), chunks(kc1), chunks(vc0), chunks(vc1), pos,
                    jnp.stack([w1_k, w1_v]).astype(BF16), jnp.stack([w2_k, w2_v]).astype(BF16))

    o_a = _nsa(main0, gates, ckv, strip_sel, strip_win, strip_cmp, _overlap_padded(seq), batch, seq)
    lam_p = jnp.stack([lam_q1, lam_k1, lam_q2, lam_k2])
    lam_init = 0.8 - 0.6 * math.exp(-0.3 * 0)
    o_b = _diff(main0, lam_p, subln, strip_a, batch, seq, lam_init)
    return o_a, o_b, ckv


def _odd_heads(xf, batch, seq, bias_table, g_mix, w_in, strips=None):
    strip_a = (strips if strips is not None else _strips(bias_table))[0]
    main1 = _norm_matmul(xf, g_mix, w_in.astype(BF16), BF16, 1024, w_in.shape[1] // 2, "in_proj_odd")
    return _moba(main1, strip_a, batch, seq)


def kernel(x, bias_table, norm_mix, norm_mlp, norm_final, mlp_w1, mlp_w2, ev_w_in, ev_w_out,
           ev_cmp_pos_k, ev_cmp_w1_k, ev_cmp_w2_k, ev_cmp_pos_v, ev_cmp_w1_v, ev_cmp_w2_v,
           ev_lam_q1, ev_lam_k1, ev_lam_q2, ev_lam_k2, ev_subln, od_w_in, od_w_out):
    batch, seq, d = x.shape
    assert d == D_MODEL and (batch * seq) % 1024 == 0
    assert seq % (max(DIFF_TILES_PER_STEP, MOBA_TILES_PER_STEP) * TQD) == 0
    assert seq % (NSA_TILES_PER_STEP * TQ) == 0
    assert SEL_TOPN * SEL_BLOCK <= seq <= LANES * SEL_BLOCK and seq >= MOBA_TOPK * MOBA_BLOCK
    xf = x.reshape(batch * seq, d)
    strips = _strips(bias_table)

    o_a, o_b, _ = _even_heads(xf, batch, seq, bias_table, norm_mix[0], ev_w_in[0],
                              ev_cmp_pos_k[0], ev_cmp_w1_k[0], ev_cmp_w2_k[0],
                              ev_cmp_pos_v[0], ev_cmp_w1_v[0], ev_cmp_w2_v[0],
                              ev_lam_q1[0], ev_lam_k1[0], ev_lam_q2[0], ev_lam_k2[0], ev_subln[0], strips)
    wo = ev_w_out[0].astype(BF16)
    x1 = _post(xf, o_a, o_b, 0, wo[:512], wo[512:], norm_mlp[0], mlp_w1[0].astype(BF16),
               mlp_w2[0].astype(BF16), norm_final, False, 1024, 2048, "post_even")

    o_m = _odd_heads(x1, batch, seq, bias_table, norm_mix[1], od_w_in[0], strips)
    wo = od_w_out[0].astype(BF16)
    out = _post(x1, o_m, o_m, 1, wo[:512], wo[512:], norm_mlp[1], mlp_w1[1].astype(BF16),
                mlp_w2[1].astype(BF16), norm_final, True, 1024, 2048, "post_odd")
    return out.reshape(batch, seq, d)
```

```python
import functools
import math

import numpy as np
import jax
import jax.numpy as jnp
from jax import lax
from jax.experimental import pallas as pl
from jax.experimental.pallas import tpu as pltpu

F32 = jnp.float32
BF16 = jnp.bfloat16

D_MODEL = 1024
HEAD_DIM = 64
EPS = 1e-6
NEG = -1e30
NUM_BUCKETS = 32
MAX_DISTANCE = 128
NSA_HEADS = 8
NSA_KV_HEADS = 2
NSA_GROUP = 4
CMP_BLOCK = 32
CMP_STRIDE = 16
SEL_BLOCK = 64
SEL_TOPN = 16
WINDOW = 512
FORCE_BONUS = 1e3
DIFF_HEADS = 4
MOBA_HEADS = 16
MOBA_BLOCK = 256
MOBA_TOPK = 3
D_FF = 4 * D_MODEL
SCALE = HEAD_DIM ** -0.5

LANES = 128
TQ = 256
TQD = 512
NSA_TILES_PER_STEP = 2
DIFF_TILES_PER_STEP = 4
MOBA_TILES_PER_STEP = 4
TK = 256
FAR_REL = 113
CMP_PER_TQ = TQ // CMP_STRIDE
CMP_NEAR = 128
CMP_PAD = CMP_NEAR - 2 * CMP_PER_TQ
SUB = 2 * TK
BAND_PAD = 3 * TK
FRONT = 2 * TK
LOG2E = math.log2(math.e)
QSCALE = SCALE * LOG2E
VMEM_LIMIT = 56 * 1024 * 1024


def _attention_params():
    return pltpu.CompilerParams(dimension_semantics=("arbitrary", "arbitrary", "arbitrary"),
                                vmem_limit_bytes=VMEM_LIMIT)


def _dot_nt(a, b):
    return lax.dot_general(a, b, (((1,), (1,)), ((), ())), preferred_element_type=F32)


def _dot(a, b):
    return jnp.dot(a, b, preferred_element_type=F32)


def _rms(x, g):
    return x * lax.rsqrt(jnp.mean(x * x, axis=-1, keepdims=True) + EPS) * g


def _t5_bucket_np(rel):
    n = np.maximum(rel, 0)
    max_exact = NUM_BUCKETS // 2
    large = max_exact + (np.log(np.maximum(n, 1).astype(np.float32) / np.float32(max_exact))
                         / np.float32(math.log(MAX_DISTANCE / max_exact))
                         * np.float32(NUM_BUCKETS - max_exact)).astype(np.int32)
    large = np.minimum(large, NUM_BUCKETS - 1)
    return np.where(n < max_exact, n, large).astype(np.int32)


def _strip_kernel(tbl_ref, bucket_ref, valid_ref, o_ref, *, head0):
    h = pl.program_id(0) + head0
    bucket = bucket_ref[...]
    last = tbl_ref[NUM_BUCKETS - 1, h]
    acc = jnp.zeros(bucket.shape, F32)
    for b in range(NUM_BUCKETS - 1):
        acc = jnp.where(bucket == b, tbl_ref[b, h] - last, acc)
    o_ref[0] = jnp.where(valid_ref[...] != 0, acc * LOG2E, NEG)


def _toeplitz_kernel(tbl_ref, bucket_ref, valid_ref, o_ref, *, rows, cols):
    h = pl.program_id(0)
    bucket = bucket_ref[...]
    last = tbl_ref[NUM_BUCKETS - 1, h]
    acc = jnp.zeros(bucket.shape, F32)
    for b in range(NUM_BUCKETS - 1):
        acc = jnp.where(bucket == b, tbl_ref[b, h] - last, acc)
    line = jnp.where(valid_ref[...] != 0, acc * LOG2E, NEG)
    wide = jnp.broadcast_to(line[0:1], (rows, line.shape[1]))
    o_ref[0] = pltpu.roll(wide, 0, 1, stride=1, stride_axis=0)[:, :cols]


def _toeplitz_strips(table, rows, cols, off, lo, hi, nheads, name):
    width = -(-(rows + cols) // LANES) * LANES
    j = np.arange(width)
    rel = off - np.where(j < cols, j, j - width)
    valid = (rel >= lo) & (rel < hi)
    tile8 = lambda a: jnp.asarray(np.broadcast_to(a[None, :], (8, width)).astype(np.int32))
    return pl.pallas_call(
        functools.partial(_toeplitz_kernel, rows=rows, cols=cols),
        out_shape=jax.ShapeDtypeStruct((nheads, rows, cols), F32),
        grid=(nheads,),
        in_specs=[pl.BlockSpec(memory_space=pltpu.SMEM),
                  pl.BlockSpec((8, width), lambda h: (0, 0)),
                  pl.BlockSpec((8, width), lambda h: (0, 0))],
        out_specs=pl.BlockSpec((1, rows, cols), lambda h: (h, 0, 0)),
        name=name,
    )(table, tile8(_t5_bucket_np(rel)), tile8(valid))


def _bias_strips(table, rel, valid, head0, nheads, name):
    rows, cols = rel.shape
    bucket = jnp.asarray(_t5_bucket_np(rel))
    validi = jnp.asarray(valid.astype(np.int32))
    return pl.pallas_call(
        functools.partial(_strip_kernel, head0=head0),
        out_shape=jax.ShapeDtypeStruct((nheads, rows, cols), F32),
        grid=(nheads,),
        in_specs=[pl.BlockSpec(memory_space=pltpu.SMEM),
                  pl.BlockSpec((rows, cols), lambda h: (0, 0)),
                  pl.BlockSpec((rows, cols), lambda h: (0, 0))],
        out_specs=pl.BlockSpec((1, rows, cols), lambda h: (h, 0, 0)),
        name=name,
    )(table, bucket, validi)


def _norm_matmul_kernel(x_ref, g_ref, w_ref, o_ref, xn_ref):
    @pl.when(pl.program_id(1) == 0)
    def _():
        xn_ref[...] = _rms(x_ref[...], g_ref[...]).astype(BF16)

    o_ref[...] = _dot(xn_ref[...], w_ref[...]).astype(o_ref.dtype)


def _norm_matmul(x, g, w, out_dtype, tm, tn, name):
    m, d = x.shape
    n = w.shape[1]
    return pl.pallas_call(
        _norm_matmul_kernel,
        out_shape=jax.ShapeDtypeStruct((m, n), out_dtype),
        grid=(m // tm, n // tn),
        in_specs=[pl.BlockSpec((tm, d), lambda i, j: (i, 0)),
                  pl.BlockSpec((1, d), lambda i, j: (0, 0)),
                  pl.BlockSpec((d, tn), lambda i, j: (0, j))],
        out_specs=pl.BlockSpec((tm, tn), lambda i, j: (i, j)),
        scratch_shapes=[pltpu.VMEM((tm, d), BF16)],
        compiler_params=pltpu.CompilerParams(
            dimension_semantics=("arbitrary", "arbitrary"), vmem_limit_bytes=VMEM_LIMIT),
        name=name,
    )(x, g.reshape(1, d), w)


def _in_proj_even_kernel(x_ref, g_ref, w_ref, wa_ref, o_ref, kc0_ref, kc1_ref, vc0_ref, vc1_ref, gate_ref,
                         xn_ref):
    hd = HEAD_DIM

    @pl.when(pl.program_id(1) == 0)
    def _():
        xn = _rms(x_ref[...], g_ref[...]).astype(BF16)
        xn_ref[...] = xn
        aux = _dot(xn, wa_ref[...])
        kc0_ref[...] = aux[:, 0:hd]
        kc1_ref[...] = aux[:, hd:2 * hd]
        vc0_ref[...] = aux[:, 2 * hd:3 * hd]
        vc1_ref[...] = aux[:, 3 * hd:4 * hd]
        gate_ref[...] = aux[:, 4 * hd:]

    o_ref[...] = _dot(xn_ref[...], w_ref[...]).astype(o_ref.dtype)


def _in_proj_even(x, g, w_main, w_aux, tm, tn):
    m, d = x.shape
    n = w_main.shape[1]
    na = w_aux.shape[1]
    row_spec = lambda width: pl.BlockSpec((tm, width), lambda i, j: (i, 0))
    cmp_shape = jax.ShapeDtypeStruct((m, HEAD_DIM), F32)
    return pl.pallas_call(
        _in_proj_even_kernel,
        out_shape=(jax.ShapeDtypeStruct((m, n), BF16), cmp_shape, cmp_shape, cmp_shape, cmp_shape,
                   jax.ShapeDtypeStruct((m, na - 4 * HEAD_DIM), F32)),
        grid=(m // tm, n // tn),
        in_specs=[row_spec(d),
                  pl.BlockSpec((1, d), lambda i, j: (0, 0)),
                  pl.BlockSpec((d, tn), lambda i, j: (0, j)),
                  pl.BlockSpec((d, na), lambda i, j: (0, 0))],
        out_specs=(pl.BlockSpec((tm, tn), lambda i, j: (i, j)), row_spec(HEAD_DIM), row_spec(HEAD_DIM),
                   row_spec(HEAD_DIM), row_spec(HEAD_DIM), row_spec(na - 4 * HEAD_DIM)),
        scratch_shapes=[pltpu.VMEM((tm, d), BF16)],
        compiler_params=pltpu.CompilerParams(
            dimension_semantics=("arbitrary", "arbitrary"), vmem_limit_bytes=VMEM_LIMIT),
        name="in_proj_even",
    )(x, g.reshape(1, d), w_main, w_aux)


def _post_kernel(x_ref, a0_ref, a1_ref, wo0_ref, wo1_ref, g_ref, w1_ref, w2_ref, gf_ref,
                 o_ref, acc_ref, xn_ref, *, final_norm):
    f = pl.program_id(1)

    @pl.when(f == 0)
    def _():
        x1 = x_ref[...] + _dot(a0_ref[...], wo0_ref[...]) + _dot(a1_ref[...], wo1_ref[...])
        acc_ref[...] = x1
        xn_ref[...] = _rms(x1, g_ref[...]).astype(BF16)

    h = jnp.square(jnp.maximum(_dot(xn_ref[...], w1_ref[...]), 0.0))
    acc_ref[...] += _dot(h.astype(BF16), w2_ref[...])

    @pl.when(f == pl.num_programs(1) - 1)
    def _():
        y = acc_ref[...]
        if final_norm:
            y = _rms(y, gf_ref[...])
        o_ref[...] = y


def _post(x, a0, a1, a1_colblk, wo0, wo1, g, w1, w2, gf, final_norm, tm, tf, name):
    m, d = x.shape
    k0 = wo0.shape[0]
    k1 = wo1.shape[0]
    ff = w1.shape[1]
    return pl.pallas_call(
        functools.partial(_post_kernel, final_norm=final_norm),
        out_shape=jax.ShapeDtypeStruct((m, d), F32),
        grid=(m // tm, ff // tf),
        in_specs=[pl.BlockSpec((tm, d), lambda i, f: (i, 0)),
                  pl.BlockSpec((tm, k0), lambda i, f: (i, 0)),
                  pl.BlockSpec((tm, k1), lambda i, f: (i, a1_colblk)),
                  pl.BlockSpec((k0, d), lambda i, f: (0, 0)),
                  pl.BlockSpec((k1, d), lambda i, f: (0, 0)),
                  pl.BlockSpec((1, d), lambda i, f: (0, 0)),
                  pl.BlockSpec((d, tf), lambda i, f: (0, f)),
                  pl.BlockSpec((tf, d), lambda i, f: (f, 0)),
                  pl.BlockSpec((1, d), lambda i, f: (0, 0))],
        out_specs=pl.BlockSpec((tm, d), lambda i, f: (i, 0)),
        scratch_shapes=[pltpu.VMEM((tm, d), F32), pltpu.VMEM((tm, d), BF16)],
        compiler_params=pltpu.CompilerParams(
            dimension_semantics=("arbitrary", "arbitrary"), vmem_limit_bytes=VMEM_LIMIT),
        name=name,
    )(x, a0, a1, wo0, wo1, g.reshape(1, d), w1, w2, gf.reshape(1, d))


def _compress_kernel(kc0_ref, kc1_ref, vc0_ref, vc1_ref, pos_ref, w1_ref, w2_ref, o_ref):
    half = CMP_STRIDE * HEAD_DIM
    for kv, refs in enumerate(((kc0_ref, kc1_ref), (vc0_ref, vc1_ref))):
        w1a = w1_ref[kv, :half, :]
        w1b = w1_ref[kv, half:, :]
        pos = pos_ref[kv].astype(BF16)
        pa = jnp.broadcast_to(pos[0:1], (8, half))
        pb = jnp.broadcast_to(pos[1:2], (8, half))
        posterm = (_dot(pa, w1a) + _dot(pb, w1b))[0:1]
        for grp, x_ref in enumerate(refs):
            x = x_ref[0].astype(BF16)
            first = _dot(x, w1a)
            second = _dot(x, w1b)
            second = pltpu.roll(second, second.shape[0] - 1, 0)
            hid = jax.nn.gelu(first + second + posterm)
            o_ref[kv, grp] = _dot(hid.astype(BF16), w2_ref[kv])


def _compress(kc0, kc1, vc0, vc1, pos, w1, w2):
    batch, nch, wide = kc0.shape
    x_spec = pl.BlockSpec((1, nch, wide), lambda b: (b, 0, 0))
    full = lambda a: pl.BlockSpec(a.shape, lambda b: (0,) * a.ndim)
    return pl.pallas_call(
        _compress_kernel,
        out_shape=jax.ShapeDtypeStruct((2, batch * NSA_KV_HEADS, nch, HEAD_DIM), F32),
        grid=(batch,),
        in_specs=[x_spec, x_spec, x_spec, x_spec, full(pos), full(w1), full(w2)],
        out_specs=pl.BlockSpec((2, NSA_KV_HEADS, nch, HEAD_DIM), lambda b: (0, b, 0, 0)),
        compiler_params=pltpu.CompilerParams(
            dimension_semantics=("arbitrary",), vmem_limit_bytes=VMEM_LIMIT),
        name="nsa_compress",
    )(kc0, kc1, vc0, vc1, pos, w1, w2)


def _softmax_part(m, s):
    m_new = jnp.maximum(m, jnp.max(s, axis=1, keepdims=True))
    return m_new, jnp.exp2(m - m_new), jnp.exp2(s - m_new).astype(BF16)


def _pv(p, v_sc, r0, width):
    if not isinstance(v_sc, (tuple, list)):
        return _dot(p, v_sc[pl.ds(r0, width), :])
    share = p.shape[0] // len(v_sc)
    return jnp.concatenate([_dot(p[g * share:(g + 1) * share], v[pl.ds(r0, width), :])
                            for g, v in enumerate(v_sc)], axis=0)


def _flash_chain(q, k_sc, v_sc, bufs, far0, n_far, tail, side_work=None):
    s_sc, p_sc, acc_sc = bufs
    rows = q.shape[0]
    assert SUB == 2 * TK and tail[0][1] == SUB
    tail0 = tail[0][0]

    def qk(r0):
        return _dot_nt(q, k_sc[pl.ds(pl.multiple_of(r0, TK), SUB), :])

    n_steps = n_far // 2
    s_sc[...] = qk(jnp.where(n_steps > 0, far0, tail0))
    p_sc[...] = jnp.zeros(p_sc.shape, BF16)
    acc_sc[...] = jnp.zeros(acc_sc.shape, F32)

    def trips(start, n_sub, n_trips, after):
        def body(j, m):
            base = pl.multiple_of(start + j * (n_sub * SUB), SUB)
            acc = acc_sc[...] + _pv(p_sc[...], v_sc, base - SUB, SUB)
            s_cur = s_sc[...]
            for u in range(n_sub):
                m, alpha, p = _softmax_part(m, s_cur)
                if u + 1 < n_sub:
                    s_cur = qk(base + (u + 1) * SUB)
                    acc = alpha * acc + _pv(p, v_sc, base + u * SUB, SUB)
                else:
                    s_sc[...] = qk(jnp.where(j + 1 < n_trips, base + n_sub * SUB, after))
                    p_sc[...] = p
                    acc_sc[...] = alpha * acc
            return m
        return body

    m = jnp.full((rows, 1), NEG, F32)
    start, left = far0, n_steps
    for n_sub in (4, 2, 1):
        n_trips = left // n_sub
        left = left - n_trips * n_sub
        nxt = start + n_trips * (n_sub * SUB)
        after = jnp.where(left > 0, nxt, tail0)
        m = lax.fori_loop(0, n_trips, trips(start, n_sub, n_trips, after), m)
        start = nxt
    if side_work is not None:
        side_work()
    base = pl.multiple_of(far0 + n_steps * SUB, SUB)
    acc = acc_sc[...] + _pv(p_sc[...], v_sc, base - SUB, SUB)
    m, alpha, p = _softmax_part(m, s_sc[...] + tail[0][2]())
    acc = alpha * acc + _pv(p, v_sc, tail0, SUB)
    return _flash_tail((m, acc), q, k_sc, v_sc, tail[1:])


def _flash_buffers(rows, ncols):
    return [pltpu.VMEM((rows, SUB), F32), pltpu.VMEM((rows, SUB), BF16), pltpu.VMEM((rows, ncols), F32)]


def _flash_tail(carry, q, k_sc, v_sc, tail):
    m, acc = carry
    for r0, width, extra in tail:
        m, alpha, p = _softmax_part(m, _dot_nt(q, k_sc[pl.ds(r0, width), :]) + extra())
        acc = alpha * acc + _pv(p, v_sc, r0, width)
    return acc


def _tile_gate(cond):
    return jnp.where(cond, 0.0, NEG)


def _tail_steps(win, odd_ok, rows, strip, near_ok):
    def first():
        return jnp.concatenate([jnp.broadcast_to(_tile_gate(odd_ok), (rows, TK)),
                                strip(0, TK) + _tile_gate(near_ok)], axis=1)

    return [(win, SUB, first), (win + SUB, SUB, lambda: strip(TK, 3 * TK))]


def _topk_rows(score, idx, k, floor):
    sel = jnp.zeros(score.shape, F32)
    nrow = score.shape[0]
    for _ in range(k):
        mx = jnp.max(score, axis=0, keepdims=True)
        first = jnp.min(jnp.where(score == mx, idx, nrow), axis=0, keepdims=True)
        pick = idx == first
        sel = jnp.where(pick, 1.0, sel)
        score = jnp.where(pick, floor, score)
    return sel


def _nsa_kernel(q_ref, qn_ref, ks_ref, vs_ref, kw_ref, vw_ref, kc_ref, vc_ref, gate_ref,
                ssel_ref, swin_ref, scmp_ref, ov_ref, o_ref,
                kaug, vaug, kwp, vwp, kcp, vcp, qaug_sc, ocw_sc, *bufs, seq):
    g = pl.program_id(1)
    step = pl.program_id(2)
    ncp = seq // CMP_STRIDE
    hd = HEAD_DIM
    rows = NSA_GROUP * TQ

    @pl.when(step == 0)
    def _():
        lane = lax.broadcasted_iota(jnp.int32, (BAND_PAD, LANES), 1)
        padflag = jnp.where(lane == hd, 1.0, 0.0).astype(BF16)
        kaug[0:BAND_PAD, 0:LANES] = jnp.zeros((BAND_PAD, LANES), BF16)
        kaug[0:BAND_PAD, LANES:2 * LANES] = padflag
        kwp[0:BAND_PAD, :] = padflag
        vaug[0:BAND_PAD, :] = jnp.zeros((BAND_PAD, LANES), BF16)
        vwp[0:BAND_PAD, :] = jnp.zeros((BAND_PAD, LANES), BF16)
        onescol = jnp.where(lax.broadcasted_iota(jnp.int32, (TK, LANES), 1) == hd, 1.0, 0.0)
        zero_hd = jnp.zeros((TK, hd), F32)
        col = lax.broadcasted_iota(jnp.int32, (TK, LANES), 1)
        row = lax.broadcasted_iota(jnp.int32, (TK, LANES), 0)

        def fill(c, _):
            src = pl.multiple_of(c * TK, TK)
            dst = pl.multiple_of(BAND_PAD + c * TK, TK)

            def pick(ref):
                both = ref[pl.ds(src, TK), :].astype(F32)
                return jnp.concatenate([jnp.where(g == 0, both[:, :hd], both[:, hd:]), zero_hd], axis=1)

            kaug[pl.ds(dst, TK), 0:LANES] = jnp.where((src + row) // SEL_BLOCK == col, 1.0, 0.0).astype(BF16)
            kaug[pl.ds(dst, TK), LANES:2 * LANES] = pick(ks_ref).astype(BF16)
            kwp[pl.ds(dst, TK), :] = pick(kw_ref).astype(BF16)
            vaug[pl.ds(dst, TK), :] = (pick(vs_ref) + onescol).astype(BF16)
            vwp[pl.ds(dst, TK), :] = (pick(vw_ref) + onescol).astype(BF16)
            return 0

        lax.fori_loop(0, seq // TK, fill, 0)
        lane_c = lax.broadcasted_iota(jnp.int32, (CMP_PAD, LANES), 1)
        kcp[0:CMP_PAD, :] = jnp.where(lane_c == hd, 1.0, 0.0).astype(BF16)
        vcp[0:CMP_PAD, :] = jnp.zeros((CMP_PAD, LANES), BF16)
        zc = jnp.zeros((ncp, hd), F32)
        onesc = jnp.where(lax.broadcasted_iota(jnp.int32, (ncp, LANES), 1) == hd, 1.0, 0.0)
        kcp[CMP_PAD:CMP_PAD + ncp, :] = jnp.concatenate([kc_ref[0, 0], zc], axis=1).astype(BF16)
        vcp[CMP_PAD:CMP_PAD + ncp, :] = (jnp.concatenate([vc_ref[0, 0], zc], axis=1) + onesc).astype(BF16)
        tail = kcp.shape[0] - CMP_PAD - ncp
        kcp[CMP_PAD + ncp:, :] = jnp.zeros((tail, LANES), BF16)
        vcp[CMP_PAD + ncp:, :] = jnp.zeros((tail, LANES), BF16)

    def stack(ref, lo, hi):
        return jnp.concatenate([ref[r, :, lo:hi] for r in range(NSA_GROUP)], axis=0)

    def prepare(qsrc_ref, t, slot):
        qblk = qsrc_ref[...].astype(F32) * QSCALE
        lane_t = lax.broadcasted_iota(jnp.int32, (TQ, hd), 1)
        qtail = jnp.where(lane_t == 0, NEG, 0.0)
        qs = [qblk[:, r * hd:(r + 1) * hd] for r in range(NSA_GROUP)]
        q128 = jnp.concatenate([jnp.concatenate([qs[r], qtail], axis=1) for r in range(NSA_GROUP)],
                               axis=0).astype(BF16)

        s_far = _dot_nt(q128, kcp[CMP_PAD:CMP_PAD + ncp, :])
        c_idx = lax.broadcasted_iota(jnp.int32, (1, ncp), 1)
        far_ok = c_idx < CMP_PER_TQ * t - CMP_PAD
        near0 = pl.multiple_of(CMP_PER_TQ * t, CMP_PER_TQ)
        s_near = _dot_nt(q128, kcp[pl.ds(near0, CMP_NEAR), :]) + stack(scmp_ref, 0, CMP_NEAR)
        near_ok = s_near > 0.5 * NEG
        s_far = jnp.where(far_ok, s_far, NEG)
        m_c = jnp.maximum(jnp.max(s_far, axis=1, keepdims=True), jnp.max(s_near, axis=1, keepdims=True))
        p_far = jnp.where(far_ok, jnp.exp2(s_far - m_c), 0.0)
        p_near = jnp.where(near_ok, jnp.exp2(s_near - m_c), 0.0)
        den = jnp.maximum(jnp.sum(p_far, axis=1, keepdims=True) + jnp.sum(p_near, axis=1, keepdims=True), 1e-30)
        pc_far = p_far / den
        pc_near = p_near / den
        ocw_sc[slot, 0] = (_dot(pc_far.astype(BF16), vcp[CMP_PAD:CMP_PAD + ncp, :])
                           + _dot(pc_near.astype(BF16), vcp[pl.ds(near0, CMP_NEAR), :]))

        band0 = pl.multiple_of(TQ * t, TK) + BAND_PAD - 2 * TK
        steps = [(band0, 2 * TK, lambda: stack(swin_ref, 0, 2 * TK)),
                 (band0 + 2 * TK, TK, lambda: stack(swin_ref, 2 * TK, 3 * TK))]
        acc = _flash_tail((jnp.full((rows, 1), NEG, F32), jnp.zeros((rows, LANES), F32)), q128, kwp, vwp, steps)
        ocw_sc[slot, 1] = acc / acc[:, hd:hd + 1]

        pcs_far = sum(pc_far[r * TQ:(r + 1) * TQ] for r in range(NSA_GROUP))
        pcs_near = sum(pc_near[r * TQ:(r + 1) * TQ] for r in range(NSA_GROUP))

        def hi_lo_dot(p, w):
            hi = p.astype(BF16)
            lo = (p - hi.astype(F32)).astype(BF16)
            return _dot(hi, w) + _dot(lo, w)

        imp = (hi_lo_dot(pcs_far, ov_ref[CMP_PAD:CMP_PAD + ncp, :])
               + hi_lo_dot(pcs_near, ov_ref[pl.ds(near0, CMP_NEAR), :]))
        imp_t = imp.T
        n_idx = lax.broadcasted_iota(jnp.int32, (LANES, TQ), 0)
        cur = (TQ * t + lax.broadcasted_iota(jnp.int32, (LANES, TQ), 1)) // SEL_BLOCK
        eligible = n_idx <= cur
        forced = ((n_idx == 0) | (n_idx == cur) | (n_idx == cur - 1)) & eligible
        rest = jnp.where(forced, -3.0e38, jnp.where(eligible, imp_t, -1.0))
        sel_t = _topk_rows(rest, n_idx, SEL_TOPN - 3, -3.0e38)
        maskneg = jnp.where(forced | (sel_t > 0.5), 0.0, NEG).T
        qaug_sc[slot] = jnp.concatenate(
            [jnp.concatenate([maskneg, qs[r], qtail], axis=1) for r in range(NSA_GROUP)],
            axis=0).astype(BF16)

    @pl.when(step == 0)
    def _():
        prepare(q_ref.at[0:TQ], 0, 0)

    n_tiles = NSA_TILES_PER_STEP * pl.num_programs(2)
    for part in range(NSA_TILES_PER_STEP):
        i = step * NSA_TILES_PER_STEP + part
        slot = part
        tok = slice(part * TQ, (part + 1) * TQ)
        n_far = jnp.maximum(i - 2, 0)
        win = pl.multiple_of(TQ * i, TK)
        tail = _tail_steps(win, n_far % 2 == 1, rows, functools.partial(stack, ssel_ref), True)
        if part + 1 < NSA_TILES_PER_STEP:
            prepare_next = functools.partial(prepare, q_ref.at[(part + 1) * TQ:(part + 2) * TQ], i + 1, slot + 1)
        else:
            prepare_next = functools.partial(prepare, qn_ref, jnp.minimum(i + 1, n_tiles - 1), 0)
        acc = _flash_chain(qaug_sc[slot], kaug, vaug, bufs[3 * part:3 * part + 3], BAND_PAD, n_far, tail,
                           prepare_next)
        o_sel = acc[:, :hd] / acc[:, hd:hd + 1]

        o_cmp = ocw_sc[slot, 0]
        o_win = ocw_sc[slot, 1]
        sig = jax.nn.sigmoid(gate_ref[tok, :])
        outs = []
        for r in range(NSA_GROUP):
            def gcol(branch):
                c0 = branch * NSA_HEADS + r
                c1 = c0 + NSA_GROUP
                return jnp.where(g == 0, sig[:, c0:c0 + 1], sig[:, c1:c1 + 1])
            sl = slice(r * TQ, (r + 1) * TQ)
            outs.append(gcol(0) * o_cmp[sl, :hd] + gcol(1) * o_sel[sl] + gcol(2) * o_win[sl, :hd])
        o_ref[tok, :] = jnp.concatenate(outs, axis=1).astype(o_ref.dtype)


def _nsa(main, aux, ckv, ssel, swin, scmp, ov, batch, seq):
    T = NSA_TILES_PER_STEP
    per_step = T * TQ
    nq = seq // TQ
    ns = seq // per_step
    ncp = seq // CMP_STRIDE
    gw = NSA_GROUP * HEAD_DIM
    once = pl.Buffered(1)
    kv_spec = lambda colblk: pl.BlockSpec((seq, LANES), lambda b, g, i: (b, colblk), pipeline_mode=once)
    strip_spec = lambda cols: pl.BlockSpec((NSA_GROUP, TQ, cols), lambda b, g, i: (g, 0, 0), pipeline_mode=once)
    return pl.pallas_call(
        functools.partial(_nsa_kernel, seq=seq),
        out_shape=jax.ShapeDtypeStruct((batch * seq, NSA_HEADS * HEAD_DIM), BF16),
        grid=(batch, NSA_KV_HEADS, ns),
        in_specs=[pl.BlockSpec((per_step, gw), lambda b, g, i: (b * ns + i, g)),
                  pl.BlockSpec((TQ, gw), lambda b, g, i: (b * nq + jnp.minimum(T * (i + 1), nq - 1), g)),
                  kv_spec(4), kv_spec(5), kv_spec(6), kv_spec(7),
                  pl.BlockSpec((1, 1, ncp, HEAD_DIM), lambda b, g, i: (0, b * NSA_KV_HEADS + g, 0, 0)),
                  pl.BlockSpec((1, 1, ncp, HEAD_DIM), lambda b, g, i: (1, b * NSA_KV_HEADS + g, 0, 0)),
                  pl.BlockSpec((per_step, LANES), lambda b, g, i: (b * ns + i, 0)),
                  strip_spec(3 * TK), strip_spec(3 * TK), strip_spec(CMP_NEAR),
                  pl.BlockSpec(ov.shape, lambda b, g, i: (0, 0))],
        out_specs=pl.BlockSpec((per_step, gw), lambda b, g, i: (b * ns + i, g)),
        scratch_shapes=[pltpu.VMEM((BAND_PAD + seq, 2 * LANES), BF16),
                        pltpu.VMEM((BAND_PAD + seq, LANES), BF16),
                        pltpu.VMEM((BAND_PAD + seq, LANES), BF16),
                        pltpu.VMEM((BAND_PAD + seq, LANES), BF16),
                        pltpu.VMEM((ov.shape[0], LANES), BF16),
                        pltpu.VMEM((ov.shape[0], LANES), BF16),
                        pltpu.VMEM((T, NSA_GROUP * TQ, 2 * LANES), BF16),
                        pltpu.VMEM((T, 2, NSA_GROUP * TQ, LANES), F32),
                        *(T * _flash_buffers(NSA_GROUP * TQ, LANES))],
        compiler_params=_attention_params(),
        name="nsa_attention",
    )(main, main, main, main, main, main, ckv, ckv, aux, ssel, swin, scmp, ov)


def _ones_column():
    return jnp.where(lax.broadcasted_iota(jnp.int32, (TK, LANES), 1) == 0, 1.0, 0.0).astype(BF16)


def _causal_flash(q, k_sc, v_sc, bufs, strip, i):
    n_far = jnp.maximum(2 * i - 1, 0)
    win = pl.multiple_of(i * TQD, SUB)
    tail = _tail_steps(win, i >= 1, q.shape[0], strip, i >= 1)
    return _flash_chain(q, k_sc, v_sc, bufs, FRONT, n_far, tail)


def _diff_kernel(q_ref, k_ref, v_ref, lam_ref, sub_ref, strip_ref, o_ref, k_sc, v_sc, *bufs, lam_init, seq):
    step = pl.program_id(2)
    hd = HEAD_DIM

    @pl.when(step == 0)
    def _():
        k_sc[0:FRONT, :] = jnp.zeros((FRONT, k_sc.shape[1]), BF16)
        v_sc[0:FRONT, :] = jnp.zeros((FRONT, v_sc.shape[1]), BF16)
        ones = _ones_column()

        def fill(c, _):
            r0 = pl.multiple_of(c * TK, TK)
            k_sc[pl.ds(FRONT + r0, TK), :] = k_ref[pl.ds(r0, TK), :]
            v_sc[pl.ds(FRONT + r0, TK), 0:LANES] = v_ref[pl.ds(r0, TK), :]
            v_sc[pl.ds(FRONT + r0, TK), LANES:2 * LANES] = ones
            return 0

        lax.fori_loop(0, seq // TK, fill, 0)

    lane = lax.broadcasted_iota(jnp.int32, (TQD, 2 * hd), 1)
    strip = lambda lo, hi: jnp.concatenate([strip_ref[0, :, lo:hi]] * 2, axis=0)
    lam_p = lam_ref[...]
    lam = (jnp.exp(jnp.sum(lam_p[0:1] * lam_p[1:2], axis=1, keepdims=True))
           - jnp.exp(jnp.sum(lam_p[2:3] * lam_p[3:4], axis=1, keepdims=True)) + lam_init)

    for part in range(DIFF_TILES_PER_STEP):
        tile_rows = slice(part * TQD, (part + 1) * TQD)
        q = q_ref[tile_rows, :].astype(F32) * QSCALE
        qst = jnp.concatenate([jnp.where(lane < hd, q, 0.0), jnp.where(lane < hd, 0.0, q)],
                              axis=0).astype(BF16)
        acc = _causal_flash(qst, k_sc, v_sc, bufs[3 * part:3 * part + 3], strip, step * DIFF_TILES_PER_STEP + part)
        o = acc[:, :2 * hd] / acc[:, 2 * hd:2 * hd + 1]
        d = o[:TQD] - lam * o[TQD:]
        o_ref[tile_rows, :] = (_rms(d, sub_ref[...]) * (1.0 - lam_init)).astype(o_ref.dtype)


def _diff(main, lam_p, subln, strips, batch, seq, lam_init):
    per_step = DIFF_TILES_PER_STEP * TQD
    nq = seq // per_step
    blk = 2 * HEAD_DIM
    q0, k0, v0 = 1024 // blk, 1536 // blk, 2048 // blk
    return pl.pallas_call(
        functools.partial(_diff_kernel, lam_init=lam_init, seq=seq),
        out_shape=jax.ShapeDtypeStruct((batch * seq, DIFF_HEADS * blk), BF16),
        grid=(batch, DIFF_HEADS, nq),
        in_specs=[pl.BlockSpec((per_step, blk), lambda b, h, i: (b * nq + i, q0 + h)),
                  pl.BlockSpec((seq, blk), lambda b, h, i: (b, k0 + h)),
                  pl.BlockSpec((seq, blk), lambda b, h, i: (b, v0 + h)),
                  pl.BlockSpec((4, HEAD_DIM), lambda b, h, i: (0, 0)),
                  pl.BlockSpec((1, blk), lambda b, h, i: (0, 0)),
                  pl.BlockSpec((1, TQD, 3 * TK), lambda b, h, i: (NSA_HEADS + h, 0, 0))],
        out_specs=pl.BlockSpec((per_step, blk), lambda b, h, i: (b * nq + i, h)),
        scratch_shapes=[pltpu.VMEM((FRONT + seq, LANES), BF16), pltpu.VMEM((FRONT + seq, 2 * LANES), BF16),
                        *(DIFF_TILES_PER_STEP * _flash_buffers(2 * TQD, 2 * LANES))],
        compiler_params=_attention_params(),
        name="diff_attention",
    )(main, main, main, lam_p, subln.reshape(1, blk), strips)


def _moba_kernel(q_ref, k_ref, v_ref, strip_ref, o_ref, kaug, va_sc, vb_sc, kmean, *bufs, seq):
    step = pl.program_id(2)
    hd = HEAD_DIM
    nb = seq // MOBA_BLOCK

    @pl.when(step == 0)
    def _():
        kmean[...] = jnp.zeros(kmean.shape, F32)
        kaug[0:FRONT, :] = jnp.zeros((FRONT, kaug.shape[1]), BF16)
        va_sc[0:FRONT, :] = jnp.zeros((FRONT, LANES), BF16)
        vb_sc[0:FRONT, :] = jnp.zeros((FRONT, LANES), BF16)
        col = lax.broadcasted_iota(jnp.int32, (MOBA_BLOCK, LANES), 1)
        ones_tail = jnp.where(lax.broadcasted_iota(jnp.int32, (MOBA_BLOCK, hd), 1) == 0, 1.0, 0.0).astype(BF16)

        def fill(n, _):
            r0 = pl.multiple_of(n * MOBA_BLOCK, MOBA_BLOCK)
            kb = k_ref[pl.ds(r0, MOBA_BLOCK), :]
            vb = v_ref[pl.ds(r0, MOBA_BLOCK), :]
            kaug[pl.ds(FRONT + r0, MOBA_BLOCK), 0:LANES] = jnp.where(col == n, 1.0, 0.0).astype(BF16)
            kaug[pl.ds(FRONT + r0, MOBA_BLOCK), LANES:2 * LANES] = kb
            va_sc[pl.ds(FRONT + r0, MOBA_BLOCK), :] = jnp.concatenate([vb[:, :hd], ones_tail], axis=1)
            vb_sc[pl.ds(FRONT + r0, MOBA_BLOCK), :] = jnp.concatenate([vb[:, hd:], ones_tail], axis=1)
            kmean[pl.ds(n, 1), :] = jnp.mean(kb.astype(F32), axis=0, keepdims=True)
            return 0

        lax.fori_loop(0, nb, fill, 0)

    lane = lax.broadcasted_iota(jnp.int32, (TQD, LANES), 1)
    n_idx = lax.broadcasted_iota(jnp.int32, (nb, TQD), 0)
    km = kmean[0:nb, :]
    lane_k = lax.broadcasted_iota(jnp.int32, km.shape, 1)
    no_block = jnp.full((LANES - nb, TQD), NEG, F32)
    strip = lambda lo, hi: jnp.concatenate([strip_ref[0, :, lo:hi], strip_ref[1, :, lo:hi]], axis=0)

    for part in range(MOBA_TILES_PER_STEP):
        i = step * MOBA_TILES_PER_STEP + part
        tile_rows = slice(part * TQD, (part + 1) * TQD)
        qblk = q_ref[tile_rows, :]
        qf = qblk.astype(F32) * QSCALE
        own = (i * TQD + lax.broadcasted_iota(jnp.int32, (nb, TQD), 1)) // MOBA_BLOCK
        past = n_idx < own
        qrows = []
        for hh in range(2):
            mine = (lane < hd) if hh == 0 else (lane >= hd)
            mine_k = (lane_k < hd) if hh == 0 else (lane_k >= hd)
            gate_t = _dot_nt(jnp.where(mine_k, km, 0.0).astype(BF16), qblk)
            sel_t = _topk_rows(jnp.where(past, gate_t, NEG), n_idx, MOBA_TOPK, -3.0e38)
            allowed = ((sel_t > 0.5) & past) | (n_idx == own)
            maskneg = jnp.concatenate([jnp.where(allowed, 0.0, NEG), no_block], axis=0).T
            qrows.append(jnp.concatenate([maskneg, jnp.where(mine, qf, 0.0)], axis=1))
        qaug = jnp.concatenate(qrows, axis=0).astype(BF16)
        acc = _causal_flash(qaug, kaug, (va_sc, vb_sc), bufs[3 * part:3 * part + 3], strip, i)
        o = acc[:, :hd] / acc[:, hd:hd + 1]
        o_ref[tile_rows, :] = jnp.concatenate([o[:TQD], o[TQD:]], axis=1).astype(o_ref.dtype)


def _moba(main, strips, batch, seq):
    per_step = MOBA_TILES_PER_STEP * TQD
    nq = seq // per_step
    pairs = MOBA_HEADS // 2
    return pl.pallas_call(
        functools.partial(_moba_kernel, seq=seq),
        out_shape=jax.ShapeDtypeStruct((batch * seq, MOBA_HEADS * HEAD_DIM), BF16),
        grid=(batch, pairs, nq),
        in_specs=[pl.BlockSpec((per_step, LANES), lambda b, h, i: (b * nq + i, h)),
                  pl.BlockSpec((seq, LANES), lambda b, h, i: (b, pairs + h)),
                  pl.BlockSpec((seq, LANES), lambda b, h, i: (b, 2 * pairs + h)),
                  pl.BlockSpec((2, TQD, 3 * TK), lambda b, h, i: (h, 0, 0))],
        out_specs=pl.BlockSpec((per_step, LANES), lambda b, h, i: (b * nq + i, h)),
        scratch_shapes=[pltpu.VMEM((FRONT + seq, 2 * LANES), BF16), pltpu.VMEM((FRONT + seq, LANES), BF16),
                        pltpu.VMEM((FRONT + seq, LANES), BF16),
                        pltpu.VMEM((LANES, LANES), F32),
                        *(MOBA_TILES_PER_STEP * _flash_buffers(2 * TQD, LANES))],
        compiler_params=_attention_params(),
        name="moba_attention",
    )(main, main, main, strips)


def _cmp_strip_geometry():
    r = np.arange(TQ)[:, None]
    cc = np.arange(CMP_NEAR)[None, :]
    rel_c = r + CMP_STRIDE * CMP_PAD - (CMP_BLOCK - 1) - CMP_STRIDE * cc
    return rel_c, rel_c >= 0


def _overlap_padded(seq):
    nc = (seq - CMP_BLOCK) // CMP_STRIDE + 1
    ncp = seq // CMP_STRIDE
    tok = np.arange(nc)[:, None] * CMP_STRIDE + np.arange(CMP_BLOCK)[None, :]
    ov = np.zeros((CMP_PAD + ncp + 2 * CMP_PER_TQ, LANES), np.float32)
    np.add.at(ov, (CMP_PAD + np.repeat(np.arange(nc), CMP_BLOCK), (tok // SEL_BLOCK).reshape(-1)),
              1.0 / CMP_BLOCK)
    return jnp.asarray(ov, dtype=BF16)


def _strips(bias_table):
    rel_c, valid_c = _cmp_strip_geometry()
    far = 1 << 30
    return (_toeplitz_strips(bias_table, TQD, 3 * TK, TK, 0, far, MOBA_HEADS, "bias_strip_diag"),
            _toeplitz_strips(bias_table, TQ, 3 * TK, 2 * TK, 0, far, NSA_HEADS, "bias_strip_sel"),
            _toeplitz_strips(bias_table, TQ, 3 * TK, 2 * TK, 0, WINDOW, NSA_HEADS, "bias_strip_win"),
            _bias_strips(bias_table, rel_c, valid_c, 0, NSA_HEADS, "bias_strip_cmp"))


def _even_heads(xf, batch, seq, bias_table, g_mix, w_in, pos_k, w1_k, w2_k, pos_v, w1_v, w2_v,
                lam_q1, lam_k1, lam_q2, lam_k2, subln, strips=None):
    d = xf.shape[1]
    strip_a, strip_sel, strip_win, strip_cmp = strips if strips is not None else _strips(bias_table)
    w_main = jnp.concatenate([w_in[:, 0:512], w_in[:, 768:1280], w_in[:, 1304:2840]], axis=1).astype(BF16)
    w_aux = jnp.concatenate([w_in[:, 512:768], w_in[:, 1280:1304],
                             jnp.zeros((d, 3 * LANES - 280), F32)], axis=1).astype(BF16)
    main0, kc0, kc1, vc0, vc1, gates = _in_proj_even(xf, g_mix, w_main, w_aux, 1024, w_main.shape[1])

    nch = seq // CMP_STRIDE
    chunks = lambda a: a.reshape(batch, nch, CMP_STRIDE * HEAD_DIM)
    pos = jnp.stack([pos_k, pos_v]).reshape(2, 2, CMP_STRIDE * HEAD_DIM)
    ckv = _compress(chunks(kc0), chunks(kc1), chunks(vc0), chunks(vc1), pos,
                    jnp.stack([w1_k, w1_v]).astype(BF16), jnp.stack([w2_k, w2_v]).astype(BF16))

    o_a = _nsa(main0, gates, ckv, strip_sel, strip_win, strip_cmp, _overlap_padded(seq), batch, seq)
    lam_p = jnp.stack([lam_q1, lam_k1, lam_q2, lam_k2])
    lam_init = 0.8 - 0.6 * math.exp(-0.3 * 0)
    o_b = _diff(main0, lam_p, subln, strip_a, batch, seq, lam_init)
    return o_a, o_b, ckv


def _odd_heads(xf, batch, seq, bias_table, g_mix, w_in, strips=None):
    strip_a = (strips if strips is not None else _strips(bias_table))[0]
    main1 = _norm_matmul(xf, g_mix, w_in.astype(BF16), BF16, 1024, w_in.shape[1], "in_proj_odd")
    return _moba(main1, strip_a, batch, seq)


def kernel(x, bias_table, norm_mix, norm_mlp, norm_final, mlp_w1, mlp_w2, ev_w_in, ev_w_out,
           ev_cmp_pos_k, ev_cmp_w1_k, ev_cmp_w2_k, ev_cmp_pos_v, ev_cmp_w1_v, ev_cmp_w2_v,
           ev_lam_q1, ev_lam_k1, ev_lam_q2, ev_lam_k2, ev_subln, od_w_in, od_w_out):
    batch, seq, d = x.shape
    assert d == D_MODEL and (batch * seq) % 1024 == 0
    assert seq % (max(DIFF_TILES_PER_STEP, MOBA_TILES_PER_STEP) * TQD) == 0
    assert seq % (NSA_TILES_PER_STEP * TQ) == 0
    assert SEL_TOPN * SEL_BLOCK <= seq <= LANES * SEL_BLOCK and seq >= MOBA_TOPK * MOBA_BLOCK
    xf = x.reshape(batch * seq, d)
    strips = _strips(bias_table)

    o_a, o_b, _ = _even_heads(xf, batch, seq, bias_table, norm_mix[0], ev_w_in[0],
                              ev_cmp_pos_k[0], ev_cmp_w1_k[0], ev_cmp_w2_k[0],
                              ev_cmp_pos_v[0], ev_cmp_w1_v[0], ev_cmp_w2_v[0],
                              ev_lam_q1[0], ev_lam_k1[0], ev_lam_q2[0], ev_lam_k2[0], ev_subln[0], strips)
    wo = ev_w_out[0].astype(BF16)
    x1 = _post(xf, o_a, o_b, 0, wo[:512], wo[512:], norm_mlp[0], mlp_w1[0].astype(BF16),
               mlp_w2[0].astype(BF16), norm_final, False, 1024, 2048, "post_even")

    o_m = _odd_heads(x1, batch, seq, bias_table, norm_mix[1], od_w_in[0], strips)
    wo = od_w_out[0].astype(BF16)
    out = _post(x1, o_m, o_m, 1, wo[:512], wo[512:], norm_mlp[1], mlp_w1[1].astype(BF16),
                mlp_w2[1].astype(BF16), norm_final, True, 1024, 2048, "post_odd")
    return out.reshape(batch, seq, d)
```
